```python
import jax, jax.numpy as jnp
from jax import lax
import numpy as np

D_MODEL = 2048
BATCH = 8
SEQ = 4096
DEPTH = 4
DEC_BATCH = 16
DEC_SEQ = 64
PAST_LEN = 4096

CHUNK = 64
SUB = 16
NSUB = CHUNK // SUB
D_A = D_MODEL // 2
HGRN_DK = 128
HGRN_HEADS = D_A // HGRN_DK
HGRN_DV = D_A // HGRN_HEADS
D_B = D_MODEL // 2
HEAD_DIM = 64
N_Q = D_B // HEAD_DIM
N_KV = N_Q // 4
GROUP = N_Q // N_KV
WINDOW = 128
N_LOOKBACK = WINDOW // CHUNK
BAND = (N_LOOKBACK + 1) * CHUNK
ROT_DIM = HEAD_DIM // 4
ROPE_THETA = 500000.0
ATTN_SCALE = HEAD_DIM ** -0.5
EPS = 1e-6
NEG = -1e30
IN_SIZES = (D_A, D_A, D_A, D_A, D_A, N_Q * HEAD_DIM, N_KV * HEAD_DIM, N_KV * HEAD_DIM, D_B, D_MODEL, D_MODEL)
N_IN = sum(IN_SIZES)

kernel_name = 'hybrid_hgrn2_swa_sink_stream_step'


def _split_points():
    return [int(s) for s in np.cumsum(IN_SIZES)[:-1]]


def _rms_norm(x, g):
    xf = x.astype(jnp.float32)
    y = xf * lax.rsqrt(jnp.mean(xf * xf, axis=-1, keepdims=True) + EPS)
    return (y * g.astype(jnp.float32)).astype(x.dtype)


def _partial_rope(x, pos):
    inv = (ROPE_THETA ** (-np.arange(0, ROT_DIM, 2) / ROT_DIM)).astype(np.float32)
    ang = pos.astype(jnp.float32)[:, None] * inv[None, :]
    cos = jnp.cos(ang)[None, :, None, :]
    sin = jnp.sin(ang)[None, :, None, :]
    xr = x[..., :ROT_DIM].astype(jnp.float32)
    x1, x2 = xr[..., :ROT_DIM // 2], xr[..., ROT_DIM // 2:]
    rot = jnp.concatenate([x1 * cos - x2 * sin, x2 * cos + x1 * sin], axis=-1).astype(x.dtype)
    return jnp.concatenate([rot, x[..., ROT_DIM:]], axis=-1)


def _lower_bounds(lb_logits):
    p = jax.nn.softmax(lb_logits.astype(jnp.float32), axis=0)
    return jnp.cumsum(p, axis=0) - p[:1]


def _hgrn_chunk(s, q, g, k, v):
    bsz, nh, length, dk = q.shape
    b = jnp.cumsum(g, axis=2)
    o_inter = jnp.einsum('bhtd,bhde->bhte', q * jnp.exp(b), s)
    bs = b.reshape(bsz, nh, NSUB, SUB, dk)
    b_ref = jnp.concatenate([jnp.zeros_like(bs[:, :, :1, -1]), bs[:, :, :-1, -1]], axis=2)
    q_sub = q.reshape(bsz, nh, NSUB, SUB, dk)
    k_sub = k.reshape(bsz, nh, NSUB, SUB, dk)
    q_rel = q_sub * jnp.exp(bs - b_ref[:, :, :, None, :])
    earlier = (np.arange(length)[None, :] < np.arange(NSUB)[:, None] * SUB)[:, :, None]
    k_rel = k[:, :, None] * jnp.exp(jnp.where(earlier, b_ref[:, :, :, None, :] - b[:, :, None], NEG))
    a_off = jnp.einsum('bhjtd,bhjsd->bhjts', q_rel, k_rel)
    causal = (np.arange(SUB)[:, None] >= np.arange(SUB)[None, :])[:, :, None]
    decay = jnp.exp(jnp.where(causal, bs[:, :, :, :, None] - bs[:, :, :, None], NEG))
    a_diag = jnp.einsum('bhjtd,bhjsd,bhjtsd->bhjts', q_sub, k_sub, decay)
    a = a_off + jnp.einsum('bhjts,jm->bhjtms', a_diag,
                           jnp.eye(NSUB, dtype=a_diag.dtype)).reshape(bsz, nh, NSUB, SUB, length)
    o_intra = jnp.einsum('bhjts,bhse->bhjte', a, v).reshape(bsz, nh, length, -1)
    b_end = b[:, :, -1]
    s_new = jnp.exp(b_end)[..., None] * s + jnp.einsum('bhsd,bhse->bhde', k * jnp.exp(b_end[:, :, None] - b), v)
    return s_new, o_inter + o_intra


def _hgrn2(q, g, k, v, s0):
    bsz, nh, t, _ = q.shape
    pad = (-t) % CHUNK
    if pad:
        pw = ((0, 0), (0, 0), (0, pad), (0, 0))
        q, g, k, v = jnp.pad(q, pw), jnp.pad(g, pw), jnp.pad(k, pw), jnp.pad(v, pw)
    n = (t + pad) // CHUNK

    def to_chunks(a):
        return a.reshape(bsz, nh, n, CHUNK, a.shape[-1]).transpose(2, 0, 1, 3, 4)

    def step(s, xs):
        s_new, o = _hgrn_chunk(s, *xs)
        return s_new, o

    s_fin, o = lax.scan(step, s0, (to_chunks(q), to_chunks(g), to_chunks(k), to_chunks(v)))
    o = o.transpose(1, 2, 0, 3, 4).reshape(bsz, nh, n * CHUNK, -1)[:, :, :t]
    return o, s_fin


def _sink_attention(q, k, v, sinks, mask):
    s = jnp.einsum('...qkgd,...skd->...kgqs', q, k).astype(jnp.float32) * ATTN_SCALE
    s = jnp.where(mask, s, NEG)
    sink = sinks.astype(jnp.float32).reshape(N_KV, GROUP)[:, :, None, None]
    m = jnp.maximum(s.max(axis=-1, keepdims=True), sink)
    p = jnp.exp(s - m)
    w = p / (p.sum(axis=-1, keepdims=True) + jnp.exp(sink - m))
    return jnp.einsum('...kgqs,...skd->...qkgd', w.astype(v.dtype), v)


def _swa_prompt(q, k, v, sinks):
    bsz, t = q.shape[0], q.shape[1]
    n = t // CHUNK
    qc = q.reshape(bsz, n, CHUNK, N_KV, GROUP, HEAD_DIM)

    def band(a):
        ap = jnp.pad(a, ((0, 0), (WINDOW, 0), (0, 0), (0, 0))).reshape(bsz, n + N_LOOKBACK, CHUNK, N_KV, HEAD_DIM)
        return jnp.concatenate([ap[:, m:m + n] for m in range(N_LOOKBACK + 1)], axis=2)

    key_chunk = np.arange(n)[:, None] - N_LOOKBACK + np.arange(BAND)[None, :] // CHUNK
    mask = (key_chunk >= 0)[:, None, None, None, :]
    o = _sink_attention(qc, band(k), band(v), sinks, mask)
    return o.reshape(bsz, t, D_B)


def _layer(x, c, pos, s0, cache_k, cache_v, ada_w, ada_b, norm_g, w_in, lb, hgrn_g, sinks, w_pa, w_pb, w_out):
    bsz, t, _ = x.shape
    f32 = jnp.float32
    mod = jnp.dot(c, ada_w) + ada_b
    shift, scale, gate = jnp.split(mod, 3, axis=-1)
    h = _rms_norm(x, norm_g) * (1 + scale[:, None]) + shift[:, None]
    qa, fa, ia, ga, za, qb, kb, vb, zb, ma, mb = jnp.split(h @ w_in, _split_points(), axis=-1)

    a = fa.astype(f32)
    lbf = lb.astype(f32)
    log_f = jax.nn.log_sigmoid(a) + jnp.log1p(lbf * jnp.exp(-a))
    k_in = (1 - lbf) * jax.nn.sigmoid(-a)

    def heads(u):
        return u.astype(f32).reshape(bsz, t, HGRN_HEADS, -1).transpose(0, 2, 1, 3)

    o_a, s_new = _hgrn2(heads(jax.nn.silu(qa.astype(f32))), heads(log_f), heads(k_in), heads(ia), s0.astype(f32))
    o_a = _rms_norm(o_a.transpose(0, 2, 1, 3), hgrn_g.reshape(HGRN_HEADS, HGRN_DV)).reshape(bsz, t, D_A)
    u_a = (o_a * jax.nn.sigmoid(ga.astype(f32)) * jax.nn.silu(za.astype(f32))).astype(x.dtype)

    q = _partial_rope(qb.reshape(bsz, t, N_Q, HEAD_DIM), pos)
    k = _partial_rope(kb.reshape(bsz, t, N_KV, HEAD_DIM), pos)
    v = vb.reshape(bsz, t, N_KV, HEAD_DIM)
    if cache_k is None:
        o_b = _swa_prompt(q, k, v, sinks)
        k_new, v_new = k[:, -WINDOW:], v[:, -WINDOW:]
    else:
        kk = jnp.concatenate([cache_k.astype(k.dtype), k], axis=1)
        vv = jnp.concatenate([cache_v.astype(v.dtype), v], axis=1)
        o_b = _sink_attention(q.reshape(bsz, t, N_KV, GROUP, HEAD_DIM), kk, vv, sinks, True).reshape(bsz, t, D_B)
        k_new, v_new = k, v
    u_b = o_b * jax.nn.silu(zb)

    merged = jax.nn.sigmoid(ma) * (u_a @ w_pa) + jax.nn.sigmoid(mb) * (u_b @ w_pb)
    y = x + gate[:, None] * (merged @ w_out)
    return y, s_new, k_new, v_new


def setup_inputs(seed: int = 0) -> dict:
    key = jax.random.key(seed)
    ks = jax.random.split(key, 20)
    nrm = jax.random.normal
    d = D_MODEL
    return {
        'x_prompt': nrm(ks[0], (BATCH, SEQ, d), jnp.float32),
        'x_sample': nrm(ks[1], (DEC_BATCH, DEC_SEQ, d), jnp.float32),
        'c_prompt': nrm(ks[2], (BATCH, d), jnp.float32),
        'c_sample': nrm(ks[3], (DEC_BATCH, d), jnp.float32),
        'state_hgrn': 0.5 * nrm(ks[4], (DEPTH, DEC_BATCH, HGRN_HEADS, HGRN_DK, HGRN_DV), jnp.float32),
        'cache_win_k': nrm(ks[5], (DEPTH, DEC_BATCH, WINDOW, N_KV, HEAD_DIM), jnp.float32),
        'cache_win_v': nrm(ks[6], (DEPTH, DEC_BATCH, WINDOW, N_KV, HEAD_DIM), jnp.float32),
        'ada_w': 0.5 * d ** -0.5 * nrm(ks[7], (DEPTH, d, 3 * d), jnp.float32),
        'ada_b': 0.02 * nrm(ks[8], (DEPTH, 3 * d), jnp.float32),
        'norm_g': 1.0 + 0.02 * nrm(ks[9], (DEPTH, d), jnp.float32),
        'w_in': d ** -0.5 * nrm(ks[10], (DEPTH, d, N_IN), jnp.float32),
        'lb_logits': 0.5 * nrm(ks[11], (DEPTH, D_A), jnp.float32),
        'hgrn_norm_g': 1.0 + 0.02 * nrm(ks[12], (DEPTH, D_A), jnp.float32),
        'sinks': 0.5 * nrm(ks[13], (DEPTH, N_Q), jnp.float32),
        'w_branch_a': D_A ** -0.5 * nrm(ks[14], (DEPTH, D_A, d), jnp.float32),
        'w_branch_b': D_B ** -0.5 * nrm(ks[15], (DEPTH, D_B, d), jnp.float32),
        'w_out': d ** -0.5 * nrm(ks[16], (DEPTH, d, d), jnp.float32),
        'final_norm_g': 1.0 + 0.02 * nrm(ks[17], (d,), jnp.float32),
    }


def reference(x_prompt, x_sample, c_prompt, c_sample, state_hgrn, cache_win_k, cache_win_v,
              ada_w, ada_b, norm_g, w_in, lb_logits, hgrn_norm_g, sinks, w_branch_a, w_branch_b, w_out,
              final_norm_g):
    lb = _lower_bounds(lb_logits)
    bp, tp = x_prompt.shape[0], x_prompt.shape[1]
    ts = x_sample.shape[1]
    pos_p = jnp.arange(tp)
    pos_s = PAST_LEN + jnp.arange(ts)
    hp, hs = x_prompt, x_sample
    sp_l, kp_l, vp_l, ss_l, ks_l, vs_l = [], [], [], [], [], []
    for l in range(DEPTH):
        w = (ada_w[l], ada_b[l], norm_g[l], w_in[l], lb[l], hgrn_norm_g[l], sinks[l],
             w_branch_a[l], w_branch_b[l], w_out[l])
        s0 = jnp.zeros((bp, HGRN_HEADS, HGRN_DK, HGRN_DV), jnp.float32)
        hp, s_p, k_p, v_p = _layer(hp, c_prompt, pos_p, s0, None, None, *w)
        hs, s_s, k_s, v_s = _layer(hs, c_sample, pos_s, state_hgrn[l], cache_win_k[l], cache_win_v[l], *w)
        sp_l.append(s_p)
        kp_l.append(k_p)
        vp_l.append(v_p)
        ss_l.append(s_s)
        ks_l.append(k_s)
        vs_l.append(v_s)
    y_prompt = _rms_norm(hp, final_norm_g)
    y_sample = _rms_norm(hs, final_norm_g)
    return (y_prompt, y_sample, jnp.stack(sp_l), jnp.stack(kp_l), jnp.stack(vp_l),
            jnp.stack(ss_l), jnp.stack(ks_l), jnp.stack(vs_l))
```

```python
import functools

import jax
import jax.numpy as jnp
import numpy as np
from jax import lax
from jax.experimental import pallas as pl
from jax.experimental.pallas import tpu as pltpu

F32 = jnp.float32
BF16 = jnp.bfloat16

D_MODEL = 2048
DEPTH = 4
PAST_LEN = 4096
CHUNK = 64
SUB = 16
NSUB = CHUNK // SUB
D_A = D_MODEL // 2
HGRN_DK = 128
HGRN_HEADS = D_A // HGRN_DK
HGRN_DV = D_A // HGRN_HEADS
D_B = D_MODEL // 2
HEAD_DIM = 64
N_Q = D_B // HEAD_DIM
N_KV = N_Q // 4
GROUP = N_Q // N_KV
WINDOW = 128
BAND = WINDOW + CHUNK
ROT_DIM = HEAD_DIM // 4
ROPE_THETA = 500000.0
ATTN_SCALE = HEAD_DIM ** -0.5
EPS = 1e-6
NEG = -1e30
N_IN = 5 * D_A + N_Q * HEAD_DIM + 2 * N_KV * HEAD_DIM + D_B + 2 * D_MODEL

LANES = 128
KV_W = N_KV * HEAD_DIM
VMEM_LIMIT = 56 * 1024 * 1024

COL_MA, COL_MB = 0, D_MODEL
COL_QA = 2 * D_MODEL
COL_FA, COL_IA, COL_GA, COL_ZA = (COL_QA + i * D_A for i in range(1, 5))
COL_QB = COL_QA + 5 * D_A
COL_ZB = COL_QB + D_B
COL_KV = COL_ZB + D_B
HEAD_PERM = [(2 * j) * GROUP + p if half == 0 else (2 * j + 1) * GROUP + p
             for j in range(N_KV // 2) for p in range(GROUP) for half in range(2)]


def _in_col_perm():
    sizes = (D_A, D_A, D_A, D_A, D_A, N_Q * HEAD_DIM, KV_W, KV_W, D_B, D_MODEL, D_MODEL)
    starts = np.concatenate([[0], np.cumsum(sizes)[:-1]])
    qa, fa, ia, ga, za, qb, kb, vb, zb, ma, mb = [np.arange(s, s + n) for s, n in zip(starts, sizes)]
    hp = np.concatenate([np.arange(h * HEAD_DIM, (h + 1) * HEAD_DIM) for h in HEAD_PERM])
    return np.concatenate([ma, mb, qa, fa, ia, ga, za, qb[hp], zb[hp], kb, vb]), hp


def _take_cols(w, idx, axis):
    breaks = np.flatnonzero(np.diff(idx) != 1) + 1
    runs = np.split(idx, breaks)
    return jnp.concatenate([lax.slice_in_dim(w, int(r[0]), int(r[-1]) + 1, axis=axis) for r in runs],
                           axis=axis)


def _cparams(sem):
    return pltpu.CompilerParams(dimension_semantics=sem, vmem_limit_bytes=VMEM_LIMIT)


def _mod_kernel(c_ref, w_ref, b_ref, o_ref):
    acc = jnp.dot(c_ref[...].astype(BF16), w_ref[0].astype(BF16), preferred_element_type=F32)
    o_ref[0] = acc + b_ref[0]


def _modulation(c_all, ada_w, ada_b):
    n, d3, tn = c_all.shape[0], 3 * D_MODEL, 768
    return pl.pallas_call(
        _mod_kernel,
        grid=(DEPTH, d3 // tn),
        in_specs=[pl.BlockSpec((n, D_MODEL), lambda l, j: (0, 0)),
                  pl.BlockSpec((1, D_MODEL, tn), lambda l, j: (l, 0, j)),
                  pl.BlockSpec((1, 1, tn), lambda l, j: (l, 0, j))],
        out_specs=pl.BlockSpec((1, n, tn), lambda l, j: (l, 0, j)),
        out_shape=jax.ShapeDtypeStruct((DEPTH, n, d3), F32),
        compiler_params=_cparams(("arbitrary", "arbitrary")),
        name="adaln_mod",
    )(c_all, ada_w, ada_b.reshape(DEPTH, 1, d3))


def _inproj_kernel(x_ref, shift_ref, scale_ref, g_ref, w_ref, o_ref, h_ref, *, nb, tt, rc):
    @pl.when(pl.program_id(2) == 0)
    def _():
        g = g_ref[...]
        for b in range(nb):
            mul = 1.0 + scale_ref[b]
            add = shift_ref[b]

            def rows(r, carry, b=b, mul=mul, add=add):
                r0 = pl.multiple_of(r * rc, rc)
                x = x_ref[b, pl.ds(r0, rc), :]
                y = x * lax.rsqrt(jnp.mean(x * x, axis=-1, keepdims=True) + EPS)
                h = (y * g) * mul + add
                h_ref[pl.ds(b * tt + r0, rc), :] = h.astype(BF16)
                return carry

            lax.fori_loop(0, tt // rc, rows, 0)

    acc = jnp.dot(h_ref[...], w_ref[...], preferred_element_type=F32)
    o_ref[...] = acc.reshape(o_ref.shape)


def _in_projection(x, shift, scale, norm_g, w, *, nb, tt, tn=512):
    bsz, t, _ = x.shape
    rc = min(tt, 128)
    kern = functools.partial(_inproj_kernel, nb=nb, tt=tt, rc=rc)
    return pl.pallas_call(
        kern,
        grid=(bsz // nb, t // tt, N_IN // tn),
        in_specs=[pl.BlockSpec((nb, tt, D_MODEL), lambda i, j, n: (i, j, 0)),
                  pl.BlockSpec((nb, 1, D_MODEL), lambda i, j, n: (i, 0, 0)),
                  pl.BlockSpec((nb, 1, D_MODEL), lambda i, j, n: (i, 0, 0)),
                  pl.BlockSpec((1, D_MODEL), lambda i, j, n: (0, 0)),
                  pl.BlockSpec((D_MODEL, tn), lambda i, j, n: (0, n))],
        out_specs=pl.BlockSpec((nb, tt, tn), lambda i, j, n: (i, j, n)),
        out_shape=jax.ShapeDtypeStruct((bsz, t, N_IN), F32),
        scratch_shapes=[pltpu.VMEM((nb * tt, D_MODEL), BF16)],
        compiler_params=_cparams(("arbitrary", "arbitrary", "arbitrary")),
        name="in_projection",
    )(x, shift, scale, norm_g, w)


def _hgrn_kernel(q_ref, f_ref, i_ref, g_ref, z_ref, lb_ref, ng_ref, s0_ref, ind_ref,
                 u_ref, sout_ref, st_ref, *, n_chunks):
    t = pl.program_id(2)

    @pl.when(t == 0)
    def _():
        st_ref[...] = s0_ref[0, 0].T

    lb = lb_ref[...]
    ng = ng_ref[...]
    ind = ind_ref[...]
    row = lax.broadcasted_iota(jnp.int32, (CHUNK, CHUNK), 0)
    col = lax.broadcasted_iota(jnp.int32, (CHUNK, CHUNK), 1)
    tri = (row >= col).astype(BF16)
    sub_bits = SUB.bit_length() - 1
    same_sub = jnp.right_shift(row, sub_bits) == jnp.right_shift(col, sub_bits)
    sub_row = lax.broadcasted_iota(jnp.int32, (SUB, HGRN_DK), 0)
    causal = [sub_row >= s for s in range(SUB)]
    nt_dims = (((1,), (1,)), ((), ()))

    def chunk(c, carry):
        r0 = pl.multiple_of(c * CHUNK, CHUNK)
        rows = pl.ds(r0, CHUNK)
        a = f_ref[0, rows, :]
        qa = q_ref[0, rows, :]
        v = i_ref[0, rows, :]
        log_f = jax.nn.log_sigmoid(a) + jnp.log1p(lb * jnp.exp(-a))
        k = (1.0 - lb) * jax.nn.sigmoid(-a)
        q = qa * jax.nn.sigmoid(qa)

        hi = log_f.astype(BF16)
        r1 = log_f - hi.astype(F32)
        mid = r1.astype(BF16)
        lo = (r1 - mid.astype(F32)).astype(BF16)
        b3 = jnp.dot(tri, jnp.concatenate([hi, mid, lo], axis=1), preferred_element_type=F32)
        b = (b3[:, :HGRN_DK] + b3[:, HGRN_DK:2 * HGRN_DK]) + b3[:, 2 * HGRN_DK:]

        st = st_ref[...]
        o = lax.dot_general((q * jnp.exp(b)).astype(BF16), st.astype(BF16), nt_dims,
                            preferred_element_type=F32)

        a_rows, lhs_rows = [], []
        for j in range(NSUB):
            lo_r, hi_r = j * SUB, (j + 1) * SUB
            qj, bj = q[lo_r:hi_r], b[lo_r:hi_r]
            if j == 0:
                a_rows.append(jnp.zeros((SUB, CHUNK), F32))
            else:
                b_ref = b[lo_r - 1:lo_r]
                q_rel = qj * jnp.exp(bj - b_ref)
                k_rel = k[:lo_r] * jnp.exp(b_ref - b[:lo_r])
                k_rel = jnp.concatenate([k_rel, jnp.zeros((CHUNK - lo_r, HGRN_DK), F32)], axis=0)
                a_rows.append(lax.dot_general(q_rel.astype(BF16), k_rel.astype(BF16), nt_dims,
                                              preferred_element_type=F32))
            pieces = []
            for s in range(SUB):
                ks = k[lo_r + s:lo_r + s + 1]
                bs = b[lo_r + s:lo_r + s + 1]
                decay = jnp.exp(jnp.where(causal[s], bj - bs, NEG))
                pieces.append(((qj * ks) * decay).astype(BF16))
            lhs_rows.append(jnp.concatenate(pieces, axis=1))
        a_off = jnp.concatenate(a_rows, axis=0)
        a_diag = jnp.dot(jnp.concatenate(lhs_rows, axis=0), ind, preferred_element_type=F32)
        a_mat = a_off + jnp.where(same_sub, a_diag, 0.0)
        o = o + jnp.dot(a_mat.astype(BF16), v.astype(BF16), preferred_element_type=F32)

        b_end = b[CHUNK - 1:CHUNK]
        k_dec = k * jnp.exp(b_end - b)
        st_ref[...] = st * jnp.exp(b_end) + jnp.dot(v.T.astype(BF16), k_dec.astype(BF16),
                                                    preferred_element_type=F32)

        y = o * lax.rsqrt(jnp.mean(o * o, axis=-1, keepdims=True) + EPS) * ng
        ga = g_ref[0, rows, :]
        za = z_ref[0, rows, :]
        u = y * jax.nn.sigmoid(ga) * (za * jax.nn.sigmoid(za))
        u_ref[0, rows, :] = u.astype(BF16)
        return carry

    lax.fori_loop(0, n_chunks, chunk, 0)

    @pl.when(t == pl.num_programs(2) - 1)
    def _():
        sout_ref[0, 0] = st_ref[...].T


def _diag_indicator():
    s_of_row = np.arange(SUB * HGRN_DK) // HGRN_DK
    return jnp.asarray((s_of_row[:, None] == (np.arange(CHUNK) % SUB)[None, :]), dtype=BF16)


def _hgrn(p, lb, ng, s0, *, tt):
    bsz, t, _ = p.shape
    n_chunks = tt // CHUNK

    def col(c0):
        return pl.BlockSpec((1, tt, HGRN_DK), lambda b, h, j, c0=c0: (b, j, c0 // HGRN_DK + h))

    vec = pl.BlockSpec((1, HGRN_DK), lambda b, h, j: (0, h))
    state = pl.BlockSpec((1, 1, HGRN_DK, HGRN_DV), lambda b, h, j: (b, h, 0, 0))
    return pl.pallas_call(
        functools.partial(_hgrn_kernel, n_chunks=n_chunks),
        grid=(bsz, HGRN_HEADS, t // tt),
        in_specs=[col(COL_QA), col(COL_FA), col(COL_IA), col(COL_GA), col(COL_ZA), vec, vec, state,
                  pl.BlockSpec((SUB * HGRN_DK, CHUNK), lambda b, h, j: (0, 0))],
        out_specs=[pl.BlockSpec((1, tt, HGRN_DV), lambda b, h, j: (b, j, h)), state],
        out_shape=[jax.ShapeDtypeStruct((bsz, t, D_A), BF16),
                   jax.ShapeDtypeStruct((bsz, HGRN_HEADS, HGRN_DK, HGRN_DV), F32)],
        scratch_shapes=[pltpu.VMEM((HGRN_DV, HGRN_DK), F32)],
        compiler_params=_cparams(("arbitrary", "arbitrary", "arbitrary")),
        name="hgrn2",
    )(p, p, p, p, p, lb, ng, s0, _diag_indicator())


def _rope(x, cos, sin_lo, sin_hi):
    half = ROT_DIM // 2
    return x * cos + pltpu.roll(x, LANES - half, 1) * sin_lo + pltpu.roll(x, half, 1) * sin_hi


def _attn_kernel(sink_ref, q_ref, kv_ref, z_ref, cos_ref, slo_ref, shi_ref, ck_ref, cv_ref,
                 u_ref, ko_ref, vo_ref, kbuf, vbuf, *, tt, masked, outw):
    t = pl.program_id(1)
    n_chunks = tt // CHUNK

    @pl.when(t == 0)
    def _():
        kbuf[0:WINDOW, :] = ck_ref[0]
        vbuf[0:WINDOW, :] = cv_ref[0]

    if tt >= WINDOW:
        @pl.when(t > 0)
        def _():
            kbuf[0:WINDOW, :] = kbuf[tt:tt + WINDOW, :]
            vbuf[0:WINDOW, :] = vbuf[tt:tt + WINDOW, :]

    cos, slo, shi = cos_ref[...], slo_ref[...], shi_ref[...]
    for j in range(KV_W // LANES):
        kbuf[WINDOW:WINDOW + tt, j * LANES:(j + 1) * LANES] = _rope(
            kv_ref[0, :, j * LANES:(j + 1) * LANES], cos, slo, shi)
    vbuf[WINDOW:WINDOW + tt, :] = kv_ref[0, :, KV_W:2 * KV_W]

    lane = lax.broadcasted_iota(jnp.int32, (CHUNK, LANES), 1)
    low = lane < HEAD_DIM
    key_lane = lax.broadcasted_iota(jnp.int32, (1, BAND), 1)
    nt_dims = (((1,), (1,)), ((), ()))

    def chunk(c, carry):
        r0 = pl.multiple_of(c * CHUNK, CHUNK)
        rows = pl.ds(r0, CHUNK)
        band = pl.ds(r0, BAND)
        cq, sl, sh = cos_ref[rows, :], slo_ref[rows, :], shi_ref[rows, :]
        if masked:
            first_key = (N_LOOKBACK_CHUNKS - (t * n_chunks + c)) * CHUNK
            key_ok = key_lane >= first_key
        for j in range(KV_W // LANES):
            kj = kbuf[band, j * LANES:(j + 1) * LANES].astype(BF16)
            vj = vbuf[band, j * LANES:(j + 1) * LANES].astype(BF16)
            qs, sinks = [], []
            for p in range(GROUP):
                slab = GROUP * j + p
                x = _rope(q_ref[0, rows, slab * LANES:(slab + 1) * LANES], cq, sl, sh) * ATTN_SCALE
                qs += [jnp.where(low, x, 0.0), jnp.where(low, 0.0, x)]
                sinks += [jnp.full((CHUNK, 1), sink_ref[HEAD_PERM[2 * slab]], F32),
                          jnp.full((CHUNK, 1), sink_ref[HEAD_PERM[2 * slab + 1]], F32)]
            qst = jnp.concatenate(qs, axis=0).astype(BF16)
            sink = jnp.concatenate(sinks, axis=0)
            s = lax.dot_general(qst, kj, nt_dims, preferred_element_type=F32)
            if masked:
                s = jnp.where(key_ok, s, NEG)
            m = jnp.maximum(jnp.max(s, axis=-1, keepdims=True), sink)
            pr = jnp.exp(s - m)
            w = pr / (jnp.sum(pr, axis=-1, keepdims=True) + jnp.exp(sink - m))
            o = jnp.dot(w.astype(BF16), vj, preferred_element_type=F32)
            for p in range(GROUP):
                slab = GROUP * j + p
                ob = jnp.where(low, o[2 * p * CHUNK:(2 * p + 1) * CHUNK],
                               o[(2 * p + 1) * CHUNK:(2 * p + 2) * CHUNK])
                zb = z_ref[0, rows, slab * LANES:(slab + 1) * LANES]
                u_ref[0, rows, slab * LANES:(slab + 1) * LANES] = (
                    ob * (zb * jax.nn.sigmoid(zb))).astype(BF16)
        return carry

    lax.fori_loop(0, n_chunks, chunk, 0)

    @pl.when(t == pl.num_programs(1) - 1)
    def _():
        ko_ref[0] = kbuf[WINDOW + tt - outw:WINDOW + tt, :]
        vo_ref[0] = vbuf[WINDOW + tt - outw:WINDOW + tt, :]


N_LOOKBACK_CHUNKS = WINDOW // CHUNK


def _rope_tables(pos):
    half = ROT_DIM // 2
    inv = (ROPE_THETA ** (-np.arange(0, ROT_DIM, 2) / ROT_DIM)).astype(np.float32)
    ang = pos.astype(F32)[:, None] * inv[None, :]
    cos, sin = jnp.cos(ang), jnp.sin(ang)
    n = pos.shape[0]
    rest = HEAD_DIM - ROT_DIM
    cos_h = jnp.concatenate([cos, cos, jnp.ones((n, rest), F32)], axis=1)
    slo_h = jnp.concatenate([-sin, jnp.zeros((n, half + rest), F32)], axis=1)
    shi_h = jnp.concatenate([jnp.zeros((n, half), F32), sin, jnp.zeros((n, rest), F32)], axis=1)
    reps = LANES // HEAD_DIM
    return tuple(jnp.tile(a, (1, reps)) for a in (cos_h, slo_h, shi_h))


def _attention(p, sinks, tables, cache_k, cache_v, *, tt, masked, outw):
    bsz, t, _ = p.shape
    tab = pl.BlockSpec((tt, LANES), lambda b, j, s: (j, 0))
    cache = pl.BlockSpec((1, WINDOW, KV_W), lambda b, j, s: (b, 0, 0))
    win = pl.BlockSpec((1, outw, KV_W), lambda b, j, s: (b, 0, 0))
    grid_spec = pltpu.PrefetchScalarGridSpec(
        num_scalar_prefetch=1,
        grid=(bsz, t // tt),
        in_specs=[pl.BlockSpec((1, tt, D_B), lambda b, j, s: (b, j, COL_QB // D_B)),
                  pl.BlockSpec((1, tt, 2 * KV_W), lambda b, j, s: (b, j, COL_KV // (2 * KV_W))),
                  pl.BlockSpec((1, tt, D_B), lambda b, j, s: (b, j, COL_ZB // D_B)),
                  tab, tab, tab, cache, cache],
        out_specs=[pl.BlockSpec((1, tt, D_B), lambda b, j, s: (b, j, 0)), win, win],
        scratch_shapes=[pltpu.VMEM((WINDOW + tt, KV_W), F32), pltpu.VMEM((WINDOW + tt, KV_W), F32)],
    )
    return pl.pallas_call(
        functools.partial(_attn_kernel, tt=tt, masked=masked, outw=outw),
        grid_spec=grid_spec,
        out_shape=[jax.ShapeDtypeStruct((bsz, t, D_B), BF16),
                   jax.ShapeDtypeStruct((bsz, outw, KV_W), F32),
                   jax.ShapeDtypeStruct((bsz, outw, KV_W), F32)],
        compiler_params=_cparams(("arbitrary", "arbitrary")),
        name="swa_attention",
    )(sinks, p, p, p, *tables, cache_k, cache_v)


def _out_kernel(ua_ref, ub_ref, ma_ref, mb_ref, x_ref, gate_ref, wpa_ref, wpb_ref, wout_ref, fg_ref,
                y_ref, *, nb, tt, final):
    m = nb * tt
    pa = jnp.dot(ua_ref[...].reshape(m, D_A), wpa_ref[...], preferred_element_type=F32)
    pb = jnp.dot(ub_ref[...].reshape(m, D_B), wpb_ref[...], preferred_element_type=F32)
    ma = ma_ref[...].reshape(m, D_MODEL)
    mb = mb_ref[...].reshape(m, D_MODEL)
    merged = jax.nn.sigmoid(ma) * pa + jax.nn.sigmoid(mb) * pb
    o = jnp.dot(merged.astype(BF16), wout_ref[...], preferred_element_type=F32)
    y = x_ref[...] + gate_ref[...] * o.reshape(nb, tt, D_MODEL)
    if final:
        y = (y * lax.rsqrt(jnp.mean(y * y, axis=-1, keepdims=True) + EPS)) * fg_ref[...]
    y_ref[...] = y


def _output(ua, ub, p, x, gate, wpa, wpb, wout, final_g, *, nb, tt, final):
    bsz, t, _ = x.shape
    tok = lambda i, j: (i, j, 0)
    const = lambda i, j: (0, 0)
    single = pl.Buffered(1)
    return pl.pallas_call(
        functools.partial(_out_kernel, nb=nb, tt=tt, final=final),
        grid=(bsz // nb, t // tt),
        in_specs=[pl.BlockSpec((nb, tt, D_A), tok),
                  pl.BlockSpec((nb, tt, D_B), tok),
                  pl.BlockSpec((nb, tt, D_MODEL), lambda i, j: (i, j, COL_MA // D_MODEL)),
                  pl.BlockSpec((nb, tt, D_MODEL), lambda i, j: (i, j, COL_MB // D_MODEL)),
                  pl.BlockSpec((nb, tt, D_MODEL), tok),
                  pl.BlockSpec((nb, 1, D_MODEL), lambda i, j: (i, 0, 0)),
                  pl.BlockSpec((D_A, D_MODEL), const, pipeline_mode=single),
                  pl.BlockSpec((D_B, D_MODEL), const, pipeline_mode=single),
                  pl.BlockSpec((D_MODEL, D_MODEL), const, pipeline_mode=single),
                  pl.BlockSpec((1, D_MODEL), const)],
        out_specs=pl.BlockSpec((nb, tt, D_MODEL), tok),
        out_shape=jax.ShapeDtypeStruct((bsz, t, D_MODEL), F32),
        compiler_params=_cparams(("arbitrary", "arbitrary")),
        name="merge_output",
    )(ua, ub, p, p, x, gate, wpa, wpb, wout, final_g)


def _tiles(t):
    if t >= 1024:
        return dict(proj=(1, 1024), out=(1, 256), hgrn_tt=512, attn_tt=512)
    return dict(proj=(16, t), out=(4, t), hgrn_tt=t, attn_tt=t)


def _layer(x, mod, w, lb, s0, cache_k, cache_v, tables, *, final, final_g):
    bsz, t, _ = x.shape
    tl = _tiles(t)
    shift, scale, gate = (mod[:, None, i * D_MODEL:(i + 1) * D_MODEL] for i in range(3))
    nb, tt = tl["proj"]
    p = _in_projection(x, shift, scale, w["norm_g"], w["w_in"], nb=min(nb, bsz), tt=tt)
    ua, s_new = _hgrn(p, lb, w["hgrn_g"], s0, tt=tl["hgrn_tt"])
    masked = cache_k is None
    if masked:
        cache_k = jnp.zeros((bsz, WINDOW, KV_W), F32)
        cache_v = cache_k
    outw = min(t, WINDOW)
    ub, k_new, v_new = _attention(p, w["sinks"], tables, cache_k, cache_v,
                                  tt=tl["attn_tt"], masked=masked, outw=outw)
    nb, tt = tl["out"]
    y = _output(ua, ub, p, x, gate, w["w_pa"], w["w_pb"], w["w_out"], final_g,
                nb=min(nb, bsz), tt=tt, final=final)
    shape = (bsz, outw, N_KV, HEAD_DIM)
    return y, s_new, k_new.reshape(shape), v_new.reshape(shape)


def kernel(x_prompt, x_sample, c_prompt, c_sample, state_hgrn, cache_win_k, cache_win_v, ada_w, ada_b,
           norm_g, w_in, lb_logits, hgrn_norm_g, sinks, w_branch_a, w_branch_b, w_out, final_norm_g):
    bp, tp = x_prompt.shape[0], x_prompt.shape[1]
    bs, ts = x_sample.shape[0], x_sample.shape[1]

    perm, head_cols = _in_col_perm()
    w_in_p = _take_cols(w_in, perm, 2).astype(BF16)
    w_pa = w_branch_a.astype(BF16)
    w_pb = _take_cols(w_branch_b, head_cols, 1).astype(BF16)
    w_o = w_out.astype(BF16)
    prob = jax.nn.softmax(lb_logits.astype(F32), axis=0)
    lb = jnp.cumsum(prob, axis=0) - prob[:1]
    final_g = final_norm_g.reshape(1, D_MODEL)

    mod = _modulation(jnp.concatenate([c_prompt, c_sample], axis=0), ada_w, ada_b)
    tab_p = _rope_tables(jnp.arange(tp))
    tab_s = _rope_tables(PAST_LEN + jnp.arange(ts))
    zero_state = jnp.zeros((bp, HGRN_HEADS, HGRN_DK, HGRN_DV), F32)

    hp, hs = x_prompt, x_sample
    outs = [[] for _ in range(6)]
    for l in range(DEPTH):
        w = dict(norm_g=norm_g[l].reshape(1, D_MODEL), w_in=w_in_p[l], hgrn_g=hgrn_norm_g[l].reshape(1, D_A),
                 sinks=sinks[l], w_pa=w_pa[l], w_pb=w_pb[l], w_out=w_o[l])
        lbl = lb[l].reshape(1, D_A)
        final = l == DEPTH - 1
        hp, s_p, k_p, v_p = _layer(hp, mod[l, :bp], w, lbl, zero_state, None, None, tab_p,
                                   final=final, final_g=final_g)
        hs, s_s, k_s, v_s = _layer(hs, mod[l, bp:], w, lbl, state_hgrn[l],
                                   cache_win_k[l].reshape(bs, WINDOW, KV_W),
                                   cache_win_v[l].reshape(bs, WINDOW, KV_W), tab_s,
                                   final=final, final_g=final_g)
        for acc, val in zip(outs, (s_p, k_p, v_p, s_s, k_s, v_s)):
            acc.append(val)
    return (hp, hs) + tuple(jnp.stack(o) for o in outs)
```

```python
import functools

import jax
import jax.numpy as jnp
import numpy as np
from jax import lax
from jax.experimental import pallas as pl
from jax.experimental.pallas import tpu as pltpu

F32 = jnp.float32
BF16 = jnp.bfloat16

D_MODEL = 2048
DEPTH = 4
PAST_LEN = 4096
CHUNK = 64
SUB = 16
NSUB = CHUNK // SUB
D_A = D_MODEL // 2
HGRN_DK = 128
HGRN_HEADS = D_A // HGRN_DK
HGRN_DV = D_A // HGRN_HEADS
D_B = D_MODEL // 2
HEAD_DIM = 64
N_Q = D_B // HEAD_DIM
N_KV = N_Q // 4
GROUP = N_Q // N_KV
WINDOW = 128
BAND = WINDOW + CHUNK
ROT_DIM = HEAD_DIM // 4
ROPE_THETA = 500000.0
ATTN_SCALE = HEAD_DIM ** -0.5
EPS = 1e-6
NEG = -1e30
LOG2E = 1.4426950408889634
N_IN = 5 * D_A + N_Q * HEAD_DIM + 2 * N_KV * HEAD_DIM + D_B + 2 * D_MODEL

LANES = 128
KV_W = N_KV * HEAD_DIM
VMEM_LIMIT = 56 * 1024 * 1024

COL_MA, COL_MB = 0, D_MODEL
COL_QA = 2 * D_MODEL
COL_FA, COL_IA, COL_GA, COL_ZA = (COL_QA + i * D_A for i in range(1, 5))
COL_QB = COL_QA + 5 * D_A
COL_ZB = COL_QB + D_B
COL_KV = COL_ZB + D_B
HEAD_PERM = [(2 * j) * GROUP + p if half == 0 else (2 * j + 1) * GROUP + p
             for j in range(N_KV // 2) for p in range(GROUP) for half in range(2)]


def _in_col_perm():
    sizes = (D_A, D_A, D_A, D_A, D_A, N_Q * HEAD_DIM, KV_W, KV_W, D_B, D_MODEL, D_MODEL)
    starts = np.concatenate([[0], np.cumsum(sizes)[:-1]])
    qa, fa, ia, ga, za, qb, kb, vb, zb, ma, mb = [np.arange(s, s + n) for s, n in zip(starts, sizes)]
    hp = np.concatenate([np.arange(h * HEAD_DIM, (h + 1) * HEAD_DIM) for h in HEAD_PERM])
    return np.concatenate([ma, mb, qa, fa, ia, ga, za, qb[hp], zb[hp], kb, vb]), hp


def _take_cols(w, idx, axis):
    breaks = np.flatnonzero(np.diff(idx) != 1) + 1
    runs = np.split(idx, breaks)
    return jnp.concatenate([lax.slice_in_dim(w, int(r[0]), int(r[-1]) + 1, axis=axis) for r in runs],
                           axis=axis)


def _cparams(sem):
    return pltpu.CompilerParams(dimension_semantics=sem, vmem_limit_bytes=VMEM_LIMIT)


def _mod_kernel(c_ref, w_ref, b_ref, o_ref):
    acc = jnp.dot(c_ref[...].astype(BF16), w_ref[0].astype(BF16), preferred_element_type=F32)
    o_ref[0] = acc + b_ref[0]


def _modulation(c_all, ada_w, ada_b):
    n, d3, tn = c_all.shape[0], 3 * D_MODEL, 768
    return pl.pallas_call(
        _mod_kernel,
        grid=(DEPTH, d3 // tn),
        in_specs=[pl.BlockSpec((n, D_MODEL), lambda l, j: (0, 0)),
                  pl.BlockSpec((1, D_MODEL, tn), lambda l, j: (l, 0, j)),
                  pl.BlockSpec((1, 1, tn), lambda l, j: (l, 0, j))],
        out_specs=pl.BlockSpec((1, n, tn), lambda l, j: (l, 0, j)),
        out_shape=jax.ShapeDtypeStruct((DEPTH, n, d3), F32),
        compiler_params=_cparams(("arbitrary", "arbitrary")),
        name="adaln_mod",
    )(c_all, ada_w, ada_b.reshape(DEPTH, 1, d3))


def _inproj_kernel(x_ref, shift_ref, scale_ref, g_ref, w_ref, o_ref, h_ref, *, nb, tt, rc):
    @pl.when(pl.program_id(2) == 0)
    def _():
        g = g_ref[...]
        for b in range(nb):
            mul = 1.0 + scale_ref[b]
            add = shift_ref[b]

            def rows(r, carry, b=b, mul=mul, add=add):
                r0 = pl.multiple_of(r * rc, rc)
                x = x_ref[b, pl.ds(r0, rc), :]
                y = x * lax.rsqrt(jnp.mean(x * x, axis=-1, keepdims=True) + EPS)
                h = (y * g) * mul + add
                h_ref[pl.ds(b * tt + r0, rc), :] = h.astype(BF16)
                return carry

            lax.fori_loop(0, tt // rc, rows, 0)

    acc = jnp.dot(h_ref[...], w_ref[...], preferred_element_type=F32)
    o_ref[...] = acc.reshape(o_ref.shape)


def _in_projection(x, shift, scale, norm_g, w, *, nb, tt, tn=512):
    bsz, t, _ = x.shape
    rc = min(tt, 128)
    kern = functools.partial(_inproj_kernel, nb=nb, tt=tt, rc=rc)
    return pl.pallas_call(
        kern,
        grid=(bsz // nb, t // tt, N_IN // tn),
        in_specs=[pl.BlockSpec((nb, tt, D_MODEL), lambda i, j, n: (i, j, 0)),
                  pl.BlockSpec((nb, 1, D_MODEL), lambda i, j, n: (i, 0, 0)),
                  pl.BlockSpec((nb, 1, D_MODEL), lambda i, j, n: (i, 0, 0)),
                  pl.BlockSpec((1, D_MODEL), lambda i, j, n: (0, 0)),
                  pl.BlockSpec((D_MODEL, tn), lambda i, j, n: (0, n))],
        out_specs=pl.BlockSpec((nb, tt, tn), lambda i, j, n: (i, j, n)),
        out_shape=jax.ShapeDtypeStruct((bsz, t, N_IN), F32),
        scratch_shapes=[pltpu.VMEM((nb * tt, D_MODEL), BF16)],
        compiler_params=_cparams(("arbitrary", "arbitrary", "arbitrary")),
        name="in_projection",
    )(x, shift, scale, norm_g, w)


HEADS_PER_LOOP = 8
SUBLANES = 8


def _hgrn_kernel(q_ref, f_ref, i_ref, g_ref, z_ref, lb_ref, ng_ref, s0_ref, ind_ref,
                 u_ref, sout_ref, st_ref, bp_ref, *, n_chunks):
    t = pl.program_id(1)

    @pl.when(t == 0)
    def _():
        for h in range(HGRN_HEADS):
            st_ref[h] = s0_ref[0, h].T

    row =lax.broadcasted_iota(jnp.int32, (CHUNK, CHUNK), 0)
    col = lax.broadcasted_iota(jnp.int32, (CHUNK, CHUNK), 1)
    tri = (row >= col).astype(BF16)
    sub_bits = SUB.bit_length() - 1
    same_sub = jnp.right_shift(row, sub_bits) == jnp.right_shift(col, sub_bits)
    sub_row = lax.broadcasted_iota(jnp.int32, (SUBLANES, HGRN_DK), 0)
    causal = [sub_row >= s for s in range(SUBLANES)]
    zeros8 = jnp.zeros((SUBLANES, HGRN_DK), F32)
    nt_dims = (((1,), (1,)), ((), ()))

    def head_chunk(h, rows):
        cols = slice(h * HGRN_DK, (h + 1) * HGRN_DK)
        lb = lb_ref[:, cols]
        a = f_ref[0, rows, cols]
        qa = q_ref[0, rows, cols]
        v = i_ref[0, rows, cols]
        a2 = a * LOG2E
        e = jnp.exp2(-a2)
        l1e = jnp.log(1.0 + e) * LOG2E
        log_f = jnp.log(1.0 + lb * e) * LOG2E - l1e
        log_k = (jnp.log(1.0 - lb) * LOG2E - a2) - l1e
        q = qa * jax.nn.sigmoid(qa)

        hi = log_f.astype(BF16)
        r1 = log_f - hi.astype(F32)
        mid = r1.astype(BF16)
        lo = (r1 - mid.astype(F32)).astype(BF16)
        b3 = jnp.dot(tri, jnp.concatenate([hi, mid, lo], axis=1), preferred_element_type=F32)
        b = (b3[:, :HGRN_DK] + b3[:, HGRN_DK:2 * HGRN_DK]) + b3[:, 2 * HGRN_DK:]
        bp = b - log_k
        bp_ref[h] = bp

        st = st_ref[h]
        o = lax.dot_general((q * jnp.exp2(b)).astype(BF16), st.astype(BF16), nt_dims,
                            preferred_element_type=F32)

        a_rows, lhs_rows = [], []
        for j in range(NSUB):
            lo_r, mid_r, hi_r = j * SUB, j * SUB + SUBLANES, (j + 1) * SUB
            q0, q1, b0, b1 = q[lo_r:mid_r], q[mid_r:hi_r], b[lo_r:mid_r], b[mid_r:hi_r]
            if j == 0:
                a_rows.append(jnp.zeros((SUB, CHUNK), F32))
            else:
                b_ref = b[lo_r - 1:lo_r]
                q_rel = q[lo_r:hi_r] * jnp.exp2(b[lo_r:hi_r] - b_ref)
                k_rel = jnp.concatenate([jnp.exp2(b_ref - bp[:lo_r]),
                                         jnp.zeros((CHUNK - lo_r, HGRN_DK), F32)], axis=0)
                a_rows.append(lax.dot_general(q_rel.astype(BF16), k_rel.astype(BF16), nt_dims,
                                              preferred_element_type=F32))
            pieces = []
            for s in range(SUB):
                bs = bp_ref[h, lo_r + s:lo_r + s + 1, :]
                if s < SUBLANES:
                    p0 = q0 * jnp.exp2(jnp.where(causal[s], b0 - bs, NEG))
                    p1 = q1 * jnp.exp2(b1 - bs)
                else:
                    p0 = zeros8
                    p1 = q1 * jnp.exp2(jnp.where(causal[s - SUBLANES], b1 - bs, NEG))
                pieces.append(jnp.concatenate([p0, p1], axis=0).astype(BF16))
            lhs_rows.append(jnp.concatenate(pieces, axis=1))
        a_off = jnp.concatenate(a_rows, axis=0)
        a_diag = jnp.dot(jnp.concatenate(lhs_rows, axis=0), ind_ref[...],
                         preferred_element_type=F32)
        a_mat = a_off + jnp.where(same_sub, a_diag, 0.0)
        o = o + jnp.dot(a_mat.astype(BF16), v.astype(BF16), preferred_element_type=F32)

        b_end = b[CHUNK - 1:CHUNK]
        k_dec = jnp.exp2(b_end - bp)
        st_ref[h] = st * jnp.exp2(b_end) +jnp.dot(v.T.astype(BF16), k_dec.astype(BF16),
                                                  preferred_element_type=F32)

        y = o * lax.rsqrt(jnp.mean(o * o, axis=-1, keepdims=True) + EPS) * ng_ref[:, cols]
        ga = g_ref[0, rows, cols]
        za = z_ref[0, rows, cols]
        u = y * jax.nn.sigmoid(ga) * (za * jax.nn.sigmoid(za))
        u_ref[0, rows, cols] = u.astype(BF16)

    for h0 in range(0, HGRN_HEADS, HEADS_PER_LOOP):
        def chunk(c, carry, h0=h0):
            rows = pl.ds(pl.multiple_of(c * CHUNK, CHUNK), CHUNK)
            for h in range(h0, h0 + HEADS_PER_LOOP):
                head_chunk(h, rows)
            return carry

        lax.fori_loop(0, n_chunks, chunk, 0)

    @pl.when(t == pl.num_programs(1) - 1)
    def _():
        for h in range(HGRN_HEADS):
            sout_ref[0, h] = st_ref[h].T


def _diag_indicator():
    s_of_row = np.arange(SUB * HGRN_DK) // HGRN_DK
    return jnp.asarray((s_of_row[:, None] == (np.arange(CHUNK) % SUB)[None, :]), dtype=BF16)


def _hgrn(p, lb, ng, s0, *, tt):
    bsz, t, _ = p.shape

    def col(c0):
        return pl.BlockSpec((1, tt, D_A), lambda b, j, c0=c0: (b, j, c0 // D_A))

    vec = pl.BlockSpec((1, D_A), lambda b, j: (0, 0))
    state = pl.BlockSpec((1, HGRN_HEADS, HGRN_DK, HGRN_DV), lambda b, j: (b, 0, 0, 0))
    return pl.pallas_call(
        functools.partial(_hgrn_kernel, n_chunks=tt // CHUNK),
        grid=(bsz, t // tt),
        in_specs=[col(COL_QA), col(COL_FA), col(COL_IA), col(COL_GA), col(COL_ZA), vec, vec, state,
                  pl.BlockSpec((SUB * HGRN_DK, CHUNK), lambda b, j: (0, 0))],
        out_specs=[pl.BlockSpec((1, tt, D_A), lambda b, j: (b, j, 0)), state],
        out_shape=[jax.ShapeDtypeStruct((bsz, t, D_A), BF16),
                   jax.ShapeDtypeStruct((bsz, HGRN_HEADS, HGRN_DK, HGRN_DV), F32)],
        scratch_shapes=[pltpu.VMEM((HGRN_HEADS, HGRN_DV, HGRN_DK), F32),
                        pltpu.VMEM((HGRN_HEADS, CHUNK, HGRN_DK), F32)],
        compiler_params=_cparams(("arbitrary", "arbitrary")),
        name="hgrn2",
    )(p, p, p, p, p, lb, ng, s0, _diag_indicator())


def _rope(x, cos, sin_lo, sin_hi):
    half = ROT_DIM // 2
    return x * cos + pltpu.roll(x, LANES - half, 1) * sin_lo + pltpu.roll(x, half, 1) * sin_hi


def _attn_kernel(sink_ref, q_ref, kv_ref, z_ref, cos_ref, slo_ref, shi_ref, ck_ref, cv_ref,
                 u_ref, ko_ref, vo_ref, kbuf, vbuf, *, tt, masked, outw):
    t = pl.program_id(1)
    n_chunks = tt // CHUNK

    @pl.when(t == 0)
    def _():
        kbuf[0:WINDOW, :] = ck_ref[0]
        vbuf[0:WINDOW, :] = cv_ref[0]

    if tt >= WINDOW:
        @pl.when(t > 0)
        def _():
            kbuf[0:WINDOW, :] = kbuf[tt:tt + WINDOW, :]
            vbuf[0:WINDOW, :] = vbuf[tt:tt + WINDOW, :]

    cos, slo, shi = cos_ref[...], slo_ref[...], shi_ref[...]
    for j in range(KV_W // LANES):
        kbuf[WINDOW:WINDOW + tt, j * LANES:(j + 1) * LANES] = _rope(
            kv_ref[0, :, j * LANES:(j + 1) * LANES], cos, slo, shi)
    vbuf[WINDOW:WINDOW + tt, :] = kv_ref[0, :, KV_W:2 * KV_W]

    lane = lax.broadcasted_iota(jnp.int32, (CHUNK, LANES), 1)
    low = lane < HEAD_DIM
    key_lane = lax.broadcasted_iota(jnp.int32, (1, BAND), 1)
    nt_dims = (((1,), (1,)), ((), ()))

    def chunk(c, carry):
        r0 = pl.multiple_of(c * CHUNK, CHUNK)
        rows = pl.ds(r0, CHUNK)
        band = pl.ds(r0, BAND)
        cq, sl, sh = cos_ref[rows, :], slo_ref[rows, :], shi_ref[rows, :]
        if masked:
            first_key = (N_LOOKBACK_CHUNKS - (t * n_chunks + c)) * CHUNK
            key_ok = key_lane >= first_key
        for j in range(KV_W // LANES):
            kj = kbuf[band, j * LANES:(j + 1) * LANES].astype(BF16)
            vj = vbuf[band, j * LANES:(j + 1) * LANES].astype(BF16)
            qs, sinks = [], []
            for p in range(GROUP):
                slab = GROUP * j + p
                x = _rope(q_ref[0, rows, slab * LANES:(slab + 1) * LANES], cq, sl, sh) * ATTN_SCALE
                qs += [jnp.where(low, x, 0.0), jnp.where(low, 0.0, x)]
                sinks += [jnp.full((CHUNK, 1), sink_ref[HEAD_PERM[2 * slab]], F32),
                          jnp.full((CHUNK, 1), sink_ref[HEAD_PERM[2 * slab + 1]], F32)]
            qst = jnp.concatenate(qs, axis=0).astype(BF16)
            sink = jnp.concatenate(sinks, axis=0)
            s = lax.dot_general(qst, kj, nt_dims, preferred_element_type=F32)
            if masked:
                s = jnp.where(key_ok, s, NEG)
            m = jnp.maximum(jnp.max(s, axis=-1, keepdims=True), sink)
            pr = jnp.exp(s - m)
            w = pr / (jnp.sum(pr, axis=-1, keepdims=True) + jnp.exp(sink - m))
            o = jnp.dot(w.astype(BF16), vj, preferred_element_type=F32)
            for p in range(GROUP):
                slab = GROUP * j + p
                ob = jnp.where(low, o[2 * p * CHUNK:(2 * p + 1) * CHUNK],
                               o[(2 * p + 1) * CHUNK:(2 * p + 2) * CHUNK])
                zb = z_ref[0, rows, slab * LANES:(slab + 1) * LANES]
                u_ref[0, rows, slab * LANES:(slab + 1) * LANES] = (
                    ob * (zb * jax.nn.sigmoid(zb))).astype(BF16)
        return carry

    lax.fori_loop(0, n_chunks, chunk, 0)

    @pl.when(t == pl.num_programs(1) - 1)
    def _():
        ko_ref[0] = kbuf[WINDOW + tt - outw:WINDOW + tt, :]
        vo_ref[0] = vbuf[WINDOW + tt - outw:WINDOW + tt, :]


N_LOOKBACK_CHUNKS = WINDOW // CHUNK


def _rope_tables(pos):
    half = ROT_DIM // 2
    inv = (ROPE_THETA ** (-np.arange(0, ROT_DIM, 2) / ROT_DIM)).astype(np.float32)
    ang = pos.astype(F32)[:, None] * inv[None, :]
    cos, sin = jnp.cos(ang), jnp.sin(ang)
    n = pos.shape[0]
    rest = HEAD_DIM - ROT_DIM
    cos_h = jnp.concatenate([cos, cos, jnp.ones((n, rest), F32)], axis=1)
    slo_h = jnp.concatenate([-sin, jnp.zeros((n, half + rest), F32)], axis=1)
    shi_h = jnp.concatenate([jnp.zeros((n, half), F32), sin, jnp.zeros((n, rest), F32)], axis=1)
    reps = LANES // HEAD_DIM
    return tuple(jnp.tile(a, (1, reps)) for a in (cos_h, slo_h, shi_h))


def _attention(p, sinks, tables, cache_k, cache_v, *, tt, masked, outw):
    bsz, t, _ = p.shape
    tab = pl.BlockSpec((tt, LANES), lambda b, j, s: (j, 0))
    cache = pl.BlockSpec((1, WINDOW, KV_W), lambda b, j, s: (b, 0, 0))
    win = pl.BlockSpec((1, outw, KV_W), lambda b, j, s: (b, 0, 0))
    grid_spec = pltpu.PrefetchScalarGridSpec(
        num_scalar_prefetch=1,
        grid=(bsz, t // tt),
        in_specs=[pl.BlockSpec((1, tt, D_B), lambda b, j, s: (b, j, COL_QB // D_B)),
                  pl.BlockSpec((1, tt, 2 * KV_W), lambda b, j, s: (b, j, COL_KV // (2 * KV_W))),
                  pl.BlockSpec((1, tt, D_B), lambda b, j, s: (b, j, COL_ZB // D_B)),
                  tab, tab, tab, cache, cache],
        out_specs=[pl.BlockSpec((1, tt, D_B), lambda b, j, s: (b, j, 0)), win, win],
        scratch_shapes=[pltpu.VMEM((WINDOW + tt, KV_W), F32), pltpu.VMEM((WINDOW + tt, KV_W), F32)],
    )
    return pl.pallas_call(
        functools.partial(_attn_kernel, tt=tt, masked=masked, outw=outw),
        grid_spec=grid_spec,
        out_shape=[jax.ShapeDtypeStruct((bsz, t, D_B), BF16),
                   jax.ShapeDtypeStruct((bsz, outw, KV_W), F32),
                   jax.ShapeDtypeStruct((bsz, outw, KV_W), F32)],
        compiler_params=_cparams(("arbitrary", "arbitrary")),
        name="swa_attention",
    )(sinks, p, p, p, *tables, cache_k, cache_v)


def _out_kernel(ua_ref, ub_ref, ma_ref, mb_ref, x_ref, gate_ref, wpa_ref, wpb_ref, wout_ref, fg_ref,
                y_ref, *, nb, tt, final):
    m = nb * tt
    pa = jnp.dot(ua_ref[...].reshape(m, D_A), wpa_ref[...], preferred_element_type=F32)
    pb = jnp.dot(ub_ref[...].reshape(m, D_B), wpb_ref[...], preferred_element_type=F32)
    ma = ma_ref[...].reshape(m, D_MODEL)
    mb = mb_ref[...].reshape(m, D_MODEL)
    merged = jax.nn.sigmoid(ma) * pa + jax.nn.sigmoid(mb) * pb
    o = jnp.dot(merged.astype(BF16), wout_ref[...], preferred_element_type=F32)
    y = x_ref[...] + gate_ref[...] * o.reshape(nb, tt, D_MODEL)
    if final:
        y = (y * lax.rsqrt(jnp.mean(y * y, axis=-1, keepdims=True) + EPS)) * fg_ref[...]
    y_ref[...] = y


def _output(ua, ub, p, x, gate, wpa, wpb, wout, final_g, *, nb, tt, final):
    bsz, t, _ = x.shape
    tok = lambda i, j: (i, j, 0)
    const = lambda i, j: (0, 0)
    single = pl.Buffered(1)
    return pl.pallas_call(
        functools.partial(_out_kernel, nb=nb, tt=tt, final=final),
        grid=(bsz // nb, t // tt),
        in_specs=[pl.BlockSpec((nb, tt, D_A), tok),
                  pl.BlockSpec((nb, tt, D_B), tok),
                  pl.BlockSpec((nb, tt, D_MODEL), lambda i, j: (i, j, COL_MA // D_MODEL)),
                  pl.BlockSpec((nb, tt, D_MODEL), lambda i, j: (i, j, COL_MB // D_MODEL)),
                  pl.BlockSpec((nb, tt, D_MODEL), tok),
                  pl.BlockSpec((nb, 1, D_MODEL), lambda i, j: (i, 0, 0)),
                  pl.BlockSpec((D_A, D_MODEL), const, pipeline_mode=single),
                  pl.BlockSpec((D_B, D_MODEL), const, pipeline_mode=single),
                  pl.BlockSpec((D_MODEL, D_MODEL), const, pipeline_mode=single),
                  pl.BlockSpec((1, D_MODEL), const)],
        out_specs=pl.BlockSpec((nb, tt, D_MODEL), tok),
        out_shape=jax.ShapeDtypeStruct((bsz, t, D_MODEL), F32),
        compiler_params=_cparams(("arbitrary", "arbitrary")),
        name="merge_output",
    )(ua, ub, p, p, x, gate, wpa, wpb, wout, final_g)


def _tiles(t):
    if t >= 1024:
        return dict(proj=(1, 1024), out=(1, 256), hgrn_tt=512, attn_tt=512)
    return dict(proj=(16, t), out=(4, t), hgrn_tt=t, attn_tt=t)


def _layer(x, mod, w, lb, s0, cache_k, cache_v, tables, *, final, final_g):
    bsz, t, _ = x.shape
    tl = _tiles(t)
    shift, scale, gate = (mod[:, None, i * D_MODEL:(i + 1) * D_MODEL] for i in range(3))
    nb, tt = tl["proj"]
    p = _in_projection(x, shift, scale, w["norm_g"], w["w_in"], nb=min(nb, bsz), tt=tt)
    ua, s_new = _hgrn(p, lb, w["hgrn_g"], s0, tt=tl["hgrn_tt"])
    masked = cache_k is None
    if masked:
        cache_k = jnp.zeros((bsz, WINDOW, KV_W), F32)
        cache_v = cache_k
    outw = min(t, WINDOW)
    ub, k_new, v_new = _attention(p, w["sinks"], tables, cache_k, cache_v,
                                  tt=tl["attn_tt"], masked=masked, outw=outw)
    nb, tt = tl["out"]
    y = _output(ua, ub, p, x, gate, w["w_pa"], w["w_pb"], w["w_out"], final_g,
                nb=min(nb, bsz), tt=tt, final=final)
    shape = (bsz, outw, N_KV, HEAD_DIM)
    return y, s_new, k_new.reshape(shape), v_new.reshape(shape)


def kernel(x_prompt, x_sample, c_prompt, c_sample, state_hgrn, cache_win_k, cache_win_v, ada_w, ada_b,
           norm_g, w_in, lb_logits, hgrn_norm_g, sinks, w_branch_a, w_branch_b, w_out, final_norm_g):
    bp, tp = x_prompt.shape[0], x_prompt.shape[1]
    bs, ts = x_sample.shape[0], x_sample.shape[1]

    perm, head_cols = _in_col_perm()
    w_in_p = _take_cols(w_in, perm, 2).astype(BF16)
    w_pa = w_branch_a.astype(BF16)
    w_pb = _take_cols(w_branch_b, head_cols, 1).astype(BF16)
    w_o = w_out.astype(BF16)
    prob = jax.nn.softmax(lb_logits.astype(F32), axis=0)
    lb = jnp.cumsum(prob, axis=0) - prob[:1]
    final_g = final_norm_g.reshape(1, D_MODEL)

    mod = _modulation(jnp.concatenate([c_prompt, c_sample], axis=0), ada_w, ada_b)
    tab_p = _rope_tables(jnp.arange(tp))
    tab_s = _rope_tables(PAST_LEN + jnp.arange(ts))
    zero_state = jnp.zeros((bp, HGRN_HEADS, HGRN_DK, HGRN_DV), F32)

    hp, hs = x_prompt, x_sample
    outs = [[] for _ in range(6)]
    for l in range(DEPTH):
        w = dict(norm_g=norm_g[l].reshape(1, D_MODEL), w_in=w_in_p[l], hgrn_g=hgrn_norm_g[l].reshape(1, D_A),
                 sinks=sinks[l], w_pa=w_pa[l], w_pb=w_pb[l], w_out=w_o[l])
        lbl = lb[l].reshape(1, D_A)
        final = l == DEPTH - 1
        hp, s_p, k_p, v_p = _layer(hp, mod[l, :bp], w, lbl, zero_state, None, None, tab_p,
                                   final=final, final_g=final_g)
        hs, s_s, k_s, v_s = _layer(hs, mod[l, bp:], w, lbl, state_hgrn[l],
                                   cache_win_k[l].reshape(bs, WINDOW, KV_W),
                                   cache_win_v[l].reshape(bs, WINDOW, KV_W), tab_s,
                                   final=final, final_g=final_g)
        for acc, val in zip(outs, (s_p, k_p, v_p, s_s, k_s, v_s)):
            acc.append(val)
    return (hp, hs) + tuple(jnp.stack(o) for o in outs)
```

```python
import functools

import jax
import jax.numpy as jnp
import numpy as np
from jax import lax
from jax.experimental import pallas as pl
from jax.experimental.pallas import tpu as pltpu

F32 = jnp.float32
BF16 = jnp.bfloat16

D_MODEL = 2048
DEPTH = 4
PAST_LEN = 4096
CHUNK = 64
SUB = 16
NSUB = CHUNK // SUB
D_A = D_MODEL // 2
HGRN_DK = 128
HGRN_HEADS = D_A // HGRN_DK
HGRN_DV = D_A // HGRN_HEADS
D_B = D_MODEL // 2
HEAD_DIM = 64
N_Q = D_B // HEAD_DIM
N_KV = N_Q // 4
GROUP = N_Q // N_KV
WINDOW = 128
N_LOOKBACK_CHUNKS = WINDOW // CHUNK
BAND = WINDOW + CHUNK
ROT_DIM = HEAD_DIM // 4
ROPE_THETA = 500000.0
ATTN_SCALE = HEAD_DIM ** -0.5
EPS = 1e-6
NEG = -1e30
LOG2E = 1.4426950408889634
N_IN = 5 * D_A + N_Q * HEAD_DIM + 2 * N_KV * HEAD_DIM + D_B + 2 * D_MODEL

LANES = 128
SUBLANES = 8
KV_W = N_KV * HEAD_DIM
VMEM_LIMIT = 56 * 1024 * 1024

N_F32 = D_A
N_B16 = N_IN - N_F32
COL_MA, COL_MB = 0, D_MODEL
COL_QA, COL_IA, COL_GA, COL_ZA = (2 * D_MODEL + i * D_A for i in range(4))
COL_QB = COL_ZA + D_A
COL_ZB = COL_QB + D_B
COL_KV = COL_ZB + D_B
HEAD_PERM = [(2 * j) * GROUP + p if half == 0 else (2 * j + 1) * GROUP + p
             for j in range(N_KV // 2) for p in range(GROUP) for half in range(2)]


def _in_col_perm():
    sizes = (D_A, D_A, D_A, D_A, D_A, N_Q * HEAD_DIM, KV_W, KV_W, D_B, D_MODEL, D_MODEL)
    starts = np.concatenate([[0], np.cumsum(sizes)[:-1]])
    qa, fa, ia, ga, za, qb, kb, vb, zb, ma, mb = [np.arange(s, s + n) for s, n in zip(starts, sizes)]
    hp = np.concatenate([np.arange(h * HEAD_DIM, (h + 1) * HEAD_DIM) for h in HEAD_PERM])
    return np.concatenate([fa, ma, mb, qa, ia, ga, za, qb[hp], zb[hp], kb, vb]), hp


def _take_cols(w, idx, axis):
    breaks = np.flatnonzero(np.diff(idx) != 1) + 1
    runs = np.split(idx, breaks)
    return jnp.concatenate([lax.slice_in_dim(w, int(r[0]), int(r[-1]) + 1, axis=axis) for r in runs],
                           axis=axis)


def _cparams(sem):
    return pltpu.CompilerParams(dimension_semantics=sem, vmem_limit_bytes=VMEM_LIMIT)


def _mod_kernel(c_ref, w_ref, b_ref, o_ref):
    acc = jnp.dot(c_ref[...].astype(BF16), w_ref[0].astype(BF16), preferred_element_type=F32)
    o_ref[0] = acc + b_ref[0]


def _modulation(c_all, ada_w, ada_b):
    n, d3, tn = c_all.shape[0], 3 * D_MODEL, 768
    return pl.pallas_call(
        _mod_kernel,
        grid=(DEPTH, d3 // tn),
        in_specs=[pl.BlockSpec((n, D_MODEL), lambda l, j: (0, 0)),
                  pl.BlockSpec((1, D_MODEL, tn), lambda l, j: (l, 0, j)),
                  pl.BlockSpec((1, 1, tn), lambda l, j: (l, 0, j))],
        out_specs=pl.BlockSpec((1, n, tn), lambda l, j: (l, 0, j)),
        out_shape=jax.ShapeDtypeStruct((DEPTH, n, d3), F32),
        compiler_params=_cparams(("arbitrary", "arbitrary")),
        name="adaln_mod",
    )(c_all, ada_w, ada_b.reshape(DEPTH, 1, d3))


def _inproj_kernel(x_ref, shift_ref, scale_ref, g_ref, w_ref, of_ref, ob_ref, h_ref, *, nb, tt, rc, nf):
    n = pl.program_id(2)

    @pl.when(n == 0)
    def _():
        g = g_ref[...]
        for b in range(nb):
            mul = 1.0 + scale_ref[b]
            add = shift_ref[b]

            def rows(r, carry, b=b, mul=mul, add=add):
                r0 = pl.multiple_of(r * rc, rc)
                x = x_ref[b, pl.ds(r0, rc), :]
                y = x * lax.rsqrt(jnp.mean(x * x, axis=-1, keepdims=True) + EPS)
                h = (y * g) * mul + add
                h_ref[pl.ds(b * tt + r0, rc), :] = h.astype(BF16)
                return carry

            lax.fori_loop(0, tt // rc, rows, 0)

    acc = jnp.dot(h_ref[...], w_ref[...], preferred_element_type=F32)

    @pl.when(n < nf)
    def _():
        of_ref[...] = acc.reshape(of_ref.shape)

    @pl.when(n >= nf)
    def _():
        ob_ref[...] = acc.astype(BF16).reshape(ob_ref.shape)


def _in_projection(x, shift, scale, norm_g, w, *, nb, tt, tn=512):
    bsz, t, _ = x.shape
    rc = min(tt, 128)
    nf = N_F32 // tn
    kern = functools.partial(_inproj_kernel, nb=nb, tt=tt, rc=rc, nf=nf)
    return pl.pallas_call(
        kern,
        grid=(bsz // nb, t // tt, N_IN // tn),
        in_specs=[pl.BlockSpec((nb, tt, D_MODEL), lambda i, j, n: (i, j, 0), pipeline_mode=pl.Buffered(1)),
                  pl.BlockSpec((nb, 1, D_MODEL), lambda i, j, n: (i, 0, 0)),
                  pl.BlockSpec((nb, 1, D_MODEL), lambda i, j, n: (i, 0, 0)),
                  pl.BlockSpec((1, D_MODEL), lambda i, j, n: (0, 0)),
                  pl.BlockSpec((D_MODEL, tn), lambda i, j, n: (0, n))],
        out_specs=[pl.BlockSpec((nb, tt, tn), lambda i, j, n: (i, j, jnp.minimum(n, nf - 1))),
                   pl.BlockSpec((nb, tt, tn), lambda i, j, n: (i, j, jnp.maximum(n - nf, 0)))],
        out_shape=[jax.ShapeDtypeStruct((bsz, t, N_F32), F32),
                   jax.ShapeDtypeStruct((bsz, t, N_B16), BF16)],
        scratch_shapes=[pltpu.VMEM((nb * tt, D_MODEL), BF16)],
        compiler_params=_cparams(("arbitrary", "arbitrary", "arbitrary")),
        name="in_projection",
    )(x, shift, scale, norm_g, w)


HEADS_PER_LOOP = 8


def _hgrn_kernel(q_ref, f_ref, i_ref, g_ref, z_ref, lb_ref, ng_ref, s0_ref, ind_ref,
                 u_ref, sout_ref, st_ref, bp_ref, *, n_chunks):
    t = pl.program_id(1)

    @pl.when(t == 0)
    def _():
        for h in range(HGRN_HEADS):
            st_ref[h] = s0_ref[0, h].T

    row = lax.broadcasted_iota(jnp.int32, (CHUNK, CHUNK), 0)
    col = lax.broadcasted_iota(jnp.int32, (CHUNK, CHUNK), 1)
    tri = (row >= col).astype(BF16)
    sub_bits = SUB.bit_length() - 1
    same_sub = jnp.right_shift(row, sub_bits) == jnp.right_shift(col, sub_bits)
    sub_row = lax.broadcasted_iota(jnp.int32, (SUBLANES, HGRN_DK), 0)
    causal = [sub_row >= s for s in range(SUBLANES)]
    zeros8 = jnp.zeros((SUBLANES, HGRN_DK), F32)
    nt_dims = (((1,), (1,)), ((), ()))
    tn_dims = (((0,), (0,)), ((), ()))

    def head_chunk(h, rows):
        cols = slice(h * HGRN_DK, (h + 1) * HGRN_DK)
        lb = lb_ref[:, cols]
        a = f_ref[0, rows, cols]
        qa = q_ref[0, rows, cols].astype(F32)
        v = i_ref[0, rows, cols]
        a2 = a * LOG2E
        e = jnp.exp2(-a2)
        l1e = jnp.log(1.0 + e) * LOG2E
        log_f = jnp.log(1.0 + lb * e) * LOG2E - l1e
        log_k = (jnp.log(1.0 - lb) * LOG2E - a2) - l1e
        q = qa * jax.nn.sigmoid(qa)

        hi = log_f.astype(BF16)
        r1 = log_f - hi.astype(F32)
        mid = r1.astype(BF16)
        lo = (r1 - mid.astype(F32)).astype(BF16)
        b3 = jnp.dot(tri, jnp.concatenate([hi, mid, lo], axis=1), preferred_element_type=F32)
        b = (b3[:, :HGRN_DK] + b3[:, HGRN_DK:2 * HGRN_DK]) + b3[:, 2 * HGRN_DK:]
        bp = b - log_k
        bp_ref[h] = bp

        st = st_ref[h]
        o = lax.dot_general((q * jnp.exp2(b)).astype(BF16), st.astype(BF16), nt_dims,
                            preferred_element_type=F32)

        a_rows, lhs_rows = [], []
        for j in range(NSUB):
            lo_r, mid_r, hi_r = j * SUB, j * SUB + SUBLANES, (j + 1) * SUB
            q0, q1, b0, b1 = q[lo_r:mid_r], q[mid_r:hi_r], b[lo_r:mid_r], b[mid_r:hi_r]
            if j == 0:
                a_rows.append(jnp.zeros((SUB, CHUNK), F32))
            else:
                b_ref = b[lo_r - 1:lo_r]
                q_rel = q[lo_r:hi_r] * jnp.exp2(b[lo_r:hi_r] - b_ref)
                k_rel = jnp.concatenate([jnp.exp2(b_ref - bp[:lo_r]),
                                         jnp.zeros((CHUNK - lo_r, HGRN_DK), F32)], axis=0)
                a_rows.append(lax.dot_general(q_rel.astype(BF16), k_rel.astype(BF16), nt_dims,
                                              preferred_element_type=F32))
            pieces = []
            for s in range(SUB):
                bs = bp_ref[h, lo_r + s:lo_r + s + 1, :]
                if s < SUBLANES:
                    p0 = q0 * jnp.exp2(jnp.where(causal[s], b0 - bs, NEG))
                    p1 = q1 * jnp.exp2(b1 - bs)
                else:
                    p0 = zeros8
                    p1 = q1 * jnp.exp2(jnp.where(causal[s - SUBLANES], b1 - bs, NEG))
                pieces.append(jnp.concatenate([p0, p1], axis=0).astype(BF16))
            lhs_rows.append(jnp.concatenate(pieces, axis=1))
        a_off = jnp.concatenate(a_rows, axis=0)
        a_diag = jnp.dot(jnp.concatenate(lhs_rows, axis=0), ind_ref[...],
                         preferred_element_type=F32)
        a_mat = a_off + jnp.where(same_sub, a_diag, 0.0)
        o = o + jnp.dot(a_mat.astype(BF16), v, preferred_element_type=F32)

        b_end = b[CHUNK - 1:CHUNK]
        k_dec = jnp.exp2(b_end - bp)
        st_ref[h] = st * jnp.exp2(b_end) + lax.dot_general(v, k_dec.astype(BF16), tn_dims,
                                                           preferred_element_type=F32)

        y = o * lax.rsqrt(jnp.mean(o * o, axis=-1, keepdims=True) + EPS) * ng_ref[:, cols]
        ga = g_ref[0, rows, cols].astype(F32)
        za = z_ref[0, rows, cols].astype(F32)
        u = y * jax.nn.sigmoid(ga) * (za * jax.nn.sigmoid(za))
        u_ref[0, rows, cols] = u.astype(BF16)

    for h0 in range(0, HGRN_HEADS, HEADS_PER_LOOP):
        def chunk(c, carry, h0=h0):
            rows = pl.ds(pl.multiple_of(c * CHUNK, CHUNK), CHUNK)
            for h in range(h0, h0 + HEADS_PER_LOOP):
                head_chunk(h, rows)
            return carry

        lax.fori_loop(0, n_chunks, chunk, 0)

    @pl.when(t == pl.num_programs(1) - 1)
    def _():
        for h in range(HGRN_HEADS):
            sout_ref[0, h] = st_ref[h].T


def _diag_indicator():
    s_of_row = np.arange(SUB * HGRN_DK) // HGRN_DK
    return jnp.asarray((s_of_row[:, None] == (np.arange(CHUNK) % SUB)[None, :]), dtype=BF16)


def _hgrn(pf, pb, lb, ng, s0, *, tt):
    bsz, t, _ = pb.shape

    def col(c0):
        return pl.BlockSpec((1, tt, D_A), lambda b, j, c0=c0: (b, j, c0 // D_A))

    vec = pl.BlockSpec((1, D_A), lambda b, j: (0, 0))
    state = pl.BlockSpec((1, HGRN_HEADS, HGRN_DK, HGRN_DV), lambda b, j: (b, 0, 0, 0))
    return pl.pallas_call(
        functools.partial(_hgrn_kernel, n_chunks=tt // CHUNK),
        grid=(bsz, t // tt),
        in_specs=[col(COL_QA), col(0), col(COL_IA), col(COL_GA), col(COL_ZA), vec, vec, state,
                  pl.BlockSpec((SUB * HGRN_DK, CHUNK), lambda b, j: (0, 0))],
        out_specs=[pl.BlockSpec((1, tt, D_A), lambda b, j: (b, j, 0)), state],
        out_shape=[jax.ShapeDtypeStruct((bsz, t, D_A), BF16),
                   jax.ShapeDtypeStruct((bsz, HGRN_HEADS, HGRN_DK, HGRN_DV), F32)],
        scratch_shapes=[pltpu.VMEM((HGRN_HEADS, HGRN_DV, HGRN_DK), F32),
                        pltpu.VMEM((HGRN_HEADS, CHUNK, HGRN_DK), F32)],
        compiler_params=_cparams(("arbitrary", "arbitrary")),
        name="hgrn2",
    )(pb, pf, pb, pb, pb, lb, ng, s0, _diag_indicator())


def _rope(x, cos, sin_lo, sin_hi):
    half = ROT_DIM // 2
    return x * cos + pltpu.roll(x, LANES - half, 1) * sin_lo + pltpu.roll(x, half, 1) * sin_hi


def _attn_kernel(sink_ref, q_ref, kv_ref, z_ref, cos_ref, slo_ref, shi_ref, ck_ref, cv_ref,
                 u_ref, ko_ref, vo_ref, kbuf, vbuf, *, tt, masked, outw):
    t = pl.program_id(1)
    last = t == pl.num_programs(1) - 1
    n_chunks = tt // CHUNK
    n_slabs = KV_W // LANES

    @pl.when(t == 0)
    def _():
        kbuf[0:WINDOW, :] = ck_ref[0].astype(BF16)
        vbuf[0:WINDOW, :] = cv_ref[0].astype(BF16)

    if tt >= WINDOW:
        @pl.when(t > 0)
        def _():
            kbuf[0:WINDOW, :] = kbuf[tt:tt + WINDOW, :]
            vbuf[0:WINDOW, :] = vbuf[tt:tt + WINDOW, :]

    for j in range(n_slabs):
        lanes = slice(j * LANES, (j + 1) * LANES)
        kr = _rope(kv_ref[0, :, lanes].astype(F32), cos_ref[...], slo_ref[...], shi_ref[...])
        kbuf[WINDOW:WINDOW + tt, lanes] = kr.astype(BF16)

        @pl.when(last)
        def _(kr=kr, lanes=lanes):
            ko_ref[0, :, lanes] = kr[tt - outw:, :]
    vbuf[WINDOW:WINDOW + tt, :] = kv_ref[0, :, KV_W:2 * KV_W]

    @pl.when(last)
    def _():
        vo_ref[0] = kv_ref[0, tt - outw:tt, KV_W:2 * KV_W].astype(F32)

    lane = lax.broadcasted_iota(jnp.int32, (CHUNK, LANES), 1)
    low = lane < HEAD_DIM
    sink_rows = [jnp.concatenate([jnp.full((1, CHUNK), sink_ref[HEAD_PERM[2 * (GROUP * j + p) + half]], F32)
                                  for p in range(GROUP) for half in range(2)], axis=1)
                 for j in range(n_slabs)]
    nt_dims = (((1,), (1,)), ((), ()))
    tn_dims = (((0,), (0,)), ((), ()))

    def chunk(c, carry):
        r0 = pl.multiple_of(c * CHUNK, CHUNK)
        rows = pl.ds(r0, CHUNK)
        band = pl.ds(r0, BAND)
        cq, sl, sh = cos_ref[rows, :], slo_ref[rows, :], shi_ref[rows, :]
        seq_chunk = t * n_chunks + c
        for j in range(n_slabs):
            lanes = slice(j * LANES, (j + 1) * LANES)
            kj = kbuf[band, lanes]
            vj = vbuf[band, lanes]
            qs = []
            for p in range(GROUP):
                slab = GROUP * j + p
                x = _rope(q_ref[0, rows, slab * LANES:(slab + 1) * LANES].astype(F32), cq, sl, sh) * ATTN_SCALE
                qs += [jnp.where(low, x, 0.0).astype(BF16), jnp.where(low, 0.0, x).astype(BF16)]
            qst = jnp.concatenate(qs, axis=0)
            s = lax.dot_general(kj, qst, nt_dims, preferred_element_type=F32)
            if masked:
                blocks = [jnp.where(seq_chunk >= N_LOOKBACK_CHUNKS - i, s[i * CHUNK:(i + 1) * CHUNK], NEG)
                          for i in range(N_LOOKBACK_CHUNKS)]
                s = jnp.concatenate(blocks + [s[WINDOW:]], axis=0)
            sink = sink_rows[j]
            m = jnp.maximum(jnp.max(s, axis=0, keepdims=True), sink)
            pr = jnp.exp(s - m)
            den = jnp.sum(pr, axis=0, keepdims=True) + jnp.exp(sink - m)
            o = lax.dot_general(vj, pr.astype(BF16), tn_dims, preferred_element_type=F32)
            o = o * (1.0 / den)
            for p in range(GROUP):
                slab = GROUP * j + p
                ot = o[:, p * LANES:(p + 1) * LANES].T
                ob = jnp.where(low, ot[:CHUNK], ot[CHUNK:])
                zb = z_ref[0, rows, slab * LANES:(slab + 1) * LANES].astype(F32)
                u_ref[0, rows, slab * LANES:(slab + 1) * LANES] = (
                    ob * (zb * jax.nn.sigmoid(zb))).astype(BF16)
        return carry

    lax.fori_loop(0, n_chunks, chunk, 0)


def _rope_tables(pos):
    half = ROT_DIM // 2
    inv = (ROPE_THETA ** (-np.arange(0, ROT_DIM, 2) / ROT_DIM)).astype(np.float32)
    ang = pos.astype(F32)[:, None] * inv[None, :]
    cos, sin = jnp.cos(ang), jnp.sin(ang)
    n = pos.shape[0]
    rest = HEAD_DIM - ROT_DIM
    cos_h = jnp.concatenate([cos, cos, jnp.ones((n, rest), F32)], axis=1)
    slo_h = jnp.concatenate([-sin, jnp.zeros((n, half + rest), F32)], axis=1)
    shi_h = jnp.concatenate([jnp.zeros((n, half), F32), sin, jnp.zeros((n, rest), F32)], axis=1)
    reps = LANES // HEAD_DIM
    return tuple(jnp.tile(a, (1, reps)) for a in (cos_h, slo_h, shi_h))


def _attention(pb, sinks, tables, cache_k, cache_v, *, tt, masked, outw):
    bsz, t, _ = pb.shape
    tab = pl.BlockSpec((tt, LANES), lambda b, j, s: (j, 0))
    cache = pl.BlockSpec((1, WINDOW, KV_W), lambda b, j, s: (b, 0, 0))
    win = pl.BlockSpec((1, outw, KV_W), lambda b, j, s: (b, 0, 0))
    grid_spec = pltpu.PrefetchScalarGridSpec(
        num_scalar_prefetch=1,
        grid=(bsz, t // tt),
        in_specs=[pl.BlockSpec((1, tt, D_B), lambda b, j, s: (b, j, COL_QB // D_B)),
                  pl.BlockSpec((1, tt, 2 * KV_W), lambda b, j, s: (b, j, COL_KV // (2 * KV_W))),
                  pl.BlockSpec((1, tt, D_B), lambda b, j, s: (b, j, COL_ZB // D_B)),
                  tab, tab, tab, cache, cache],
        out_specs=[pl.BlockSpec((1, tt, D_B), lambda b, j, s: (b, j, 0)), win, win],
        scratch_shapes=[pltpu.VMEM((WINDOW + tt, KV_W), BF16), pltpu.VMEM((WINDOW + tt, KV_W), BF16)],
    )
    return pl.pallas_call(
        functools.partial(_attn_kernel, tt=tt, masked=masked, outw=outw),
        grid_spec=grid_spec,
        out_shape=[jax.ShapeDtypeStruct((bsz, t, D_B), BF16),
                   jax.ShapeDtypeStruct((bsz, outw, KV_W), F32),
                   jax.ShapeDtypeStruct((bsz, outw, KV_W), F32)],
        compiler_params=_cparams(("arbitrary", "arbitrary")),
        name="swa_attention",
    )(sinks, pb, pb, pb, *tables, cache_k, cache_v)


def _out_kernel(ua_ref, ub_ref, ma_ref, mb_ref, x_ref, gate_ref, wpa_ref, wpb_ref, wout_ref, fg_ref,
                y_ref, *, nb, tt, final):
    m = nb * tt
    pa = jnp.dot(ua_ref[...].reshape(m, D_A), wpa_ref[...], preferred_element_type=F32)
    pb = jnp.dot(ub_ref[...].reshape(m, D_B), wpb_ref[...], preferred_element_type=F32)
    ma = ma_ref[...].reshape(m, D_MODEL).astype(F32)
    mb = mb_ref[...].reshape(m, D_MODEL).astype(F32)
    merged = jax.nn.sigmoid(ma) * pa + jax.nn.sigmoid(mb) * pb
    o = jnp.dot(merged.astype(BF16), wout_ref[...], preferred_element_type=F32)
    y = x_ref[...] + gate_ref[...] * o.reshape(nb, tt, D_MODEL)
    if final:
        y = (y * lax.rsqrt(jnp.mean(y * y, axis=-1, keepdims=True) + EPS)) * fg_ref[...]
    y_ref[...] = y


def _output(ua, ub, pb, x, gate, wpa, wpb, wout, final_g, *, nb, tt, final):
    bsz, t, _ = x.shape
    tok = lambda i, j: (i, j, 0)
    const = lambda i, j: (0, 0)
    single = pl.Buffered(1)
    return pl.pallas_call(
        functools.partial(_out_kernel, nb=nb, tt=tt, final=final),
        grid=(bsz // nb, t // tt),
        in_specs=[pl.BlockSpec((nb, tt, D_A), tok),
                  pl.BlockSpec((nb, tt, D_B), tok),
                  pl.BlockSpec((nb, tt, D_MODEL), lambda i, j: (i, j, COL_MA // D_MODEL)),
                  pl.BlockSpec((nb, tt, D_MODEL), lambda i, j: (i, j, COL_MB // D_MODEL)),
                  pl.BlockSpec((nb, tt, D_MODEL), tok),
                  pl.BlockSpec((nb, 1, D_MODEL), lambda i, j: (i, 0, 0)),
                  pl.BlockSpec((D_A, D_MODEL), const, pipeline_mode=single),
                  pl.BlockSpec((D_B, D_MODEL), const, pipeline_mode=single),
                  pl.BlockSpec((D_MODEL, D_MODEL), const, pipeline_mode=single),
                  pl.BlockSpec((1, D_MODEL), const)],
        out_specs=pl.BlockSpec((nb, tt, D_MODEL), tok),
        out_shape=jax.ShapeDtypeStruct((bsz, t, D_MODEL), F32),
        compiler_params=_cparams(("arbitrary", "arbitrary")),
        name="merge_output",
    )(ua, ub, pb, pb, x, gate, wpa, wpb, wout, final_g)


def _tiles(t):
    if t >= 1024:
        return dict(proj=(1, min(t, 2048)), out=(1, 256), hgrn_tt=512, attn_tt=512)
    return dict(proj=(16, t), out=(4, t), hgrn_tt=t, attn_tt=t)


def _layer(x, mod, w, lb, s0, cache_k, cache_v, tables, *, final, final_g):
    bsz, t, _ = x.shape
    tl = _tiles(t)
    shift, scale, gate = (mod[:, None, i * D_MODEL:(i + 1) * D_MODEL] for i in range(3))
    nb, tt = tl["proj"]
    pf, pb = _in_projection(x, shift, scale, w["norm_g"], w["w_in"], nb=min(nb, bsz), tt=tt)
    ua, s_new = _hgrn(pf, pb, lb, w["hgrn_g"], s0, tt=tl["hgrn_tt"])
    masked = cache_k is None
    if masked:
        cache_k = jnp.zeros((bsz, WINDOW, KV_W), F32)
        cache_v = cache_k
    outw = min(t, WINDOW)
    ub, k_new, v_new = _attention(pb, w["sinks"], tables, cache_k, cache_v,
                                  tt=tl["attn_tt"], masked=masked, outw=outw)
    nb, tt = tl["out"]
    y = _output(ua, ub, pb, x, gate, w["w_pa"], w["w_pb"], w["w_out"], final_g,
                nb=min(nb, bsz), tt=tt, final=final)
    shape = (bsz, outw, N_KV, HEAD_DIM)
    return y, s_new, k_new.reshape(shape), v_new.reshape(shape)


def kernel(x_prompt, x_sample, c_prompt, c_sample, state_hgrn, cache_win_k, cache_win_v, ada_w, ada_b,
           norm_g, w_in, lb_logits, hgrn_norm_g, sinks, w_branch_a, w_branch_b, w_out, final_norm_g):
    bp, tp = x_prompt.shape[0], x_prompt.shape[1]
    bs, ts = x_sample.shape[0], x_sample.shape[1]

    perm, head_cols = _in_col_perm()
    w_in_p = _take_cols(w_in.astype(BF16), perm, 2)
    w_pa = w_branch_a.astype(BF16)
    w_pb = _take_cols(w_branch_b.astype(BF16), head_cols, 1)
    w_o = w_out.astype(BF16)
    prob = jax.nn.softmax(lb_logits.astype(F32), axis=0)
    lb = jnp.cumsum(prob, axis=0) - prob[:1]
    final_g = final_norm_g.reshape(1, D_MODEL)

    mod = _modulation(jnp.concatenate([c_prompt, c_sample], axis=0), ada_w, ada_b)
    tab_p = _rope_tables(jnp.arange(tp))
    tab_s = _rope_tables(PAST_LEN + jnp.arange(ts))
    zero_state = jnp.zeros((bp, HGRN_HEADS, HGRN_DK, HGRN_DV), F32)

    hp, hs = x_prompt, x_sample
    outs = [[] for _ in range(6)]
    for l in range(DEPTH):
        w = dict(norm_g=norm_g[l].reshape(1, D_MODEL), w_in=w_in_p[l], hgrn_g=hgrn_norm_g[l].reshape(1, D_A),
                 sinks=sinks[l], w_pa=w_pa[l], w_pb=w_pb[l], w_out=w_o[l])
        lbl = lb[l].reshape(1, D_A)
        final = l == DEPTH - 1
        hp, s_p, k_p, v_p = _layer(hp, mod[l, :bp], w, lbl, zero_state, None, None, tab_p,
                                   final=final, final_g=final_g)
        hs, s_s, k_s, v_s = _layer(hs, mod[l, bp:], w, lbl, state_hgrn[l],
                                   cache_win_k[l].reshape(bs, WINDOW, KV_W),
                                   cache_win_v[l].reshape(bs, WINDOW, KV_W), tab_s,
                                   final=final, final_g=final_g)
        for acc, val in zip(outs, (s_p, k_p, v_p, s_s, k_s, v_s)):
            acc.append(val)
    return (hp, hs) + tuple(jnp.stack(o) for o in outs)
```

```python
import functools

import jax
import jax.numpy as jnp
import numpy as np
from jax import lax
from jax.experimental import pallas as pl
from jax.experimental.pallas import tpu as pltpu

F32 = jnp.float32
BF16 = jnp.bfloat16

D_MODEL = 2048
DEPTH = 4
PAST_LEN = 4096
CHUNK = 64
SUB = 16
NSUB = CHUNK // SUB
D_A = D_MODEL // 2
HGRN_DK = 128
HGRN_HEADS = D_A // HGRN_DK
HGRN_DV = D_A // HGRN_HEADS
D_B = D_MODEL // 2
HEAD_DIM = 64
N_Q = D_B // HEAD_DIM
N_KV = N_Q // 4
GROUP = N_Q // N_KV
WINDOW = 128
N_LOOKBACK_CHUNKS = WINDOW // CHUNK
BAND = WINDOW + CHUNK
ROT_DIM = HEAD_DIM // 4
ROPE_THETA = 500000.0
ATTN_SCALE = HEAD_DIM ** -0.5
EPS = 1e-6
NEG = -1e30
LOG2E = 1.4426950408889634
N_IN = 5 * D_A + N_Q * HEAD_DIM + 2 * N_KV * HEAD_DIM + D_B + 2 * D_MODEL

LANES = 128
SUBLANES = 8
KV_W = N_KV * HEAD_DIM
VMEM_LIMIT = 56 * 1024 * 1024

N_F32 = D_A
N_B16 = N_IN - N_F32
COL_MA, COL_MB = 0, D_MODEL
COL_QA, COL_IA, COL_GA, COL_ZA = (2 * D_MODEL + i * D_A for i in range(4))
COL_QB = COL_ZA + D_A
COL_ZB = COL_QB + D_B
COL_KV = COL_ZB + D_B
HEAD_PERM = [(2 * j) * GROUP + p if half == 0 else (2 * j + 1) * GROUP + p
             for j in range(N_KV // 2) for p in range(GROUP) for half in range(2)]


def _in_col_perm():
    sizes = (D_A, D_A, D_A, D_A, D_A, N_Q * HEAD_DIM, KV_W, KV_W, D_B, D_MODEL, D_MODEL)
    starts = np.concatenate([[0], np.cumsum(sizes)[:-1]])
    qa, fa, ia, ga, za, qb, kb, vb, zb, ma, mb = [np.arange(s, s + n) for s, n in zip(starts, sizes)]
    hp = np.concatenate([np.arange(h * HEAD_DIM, (h + 1) * HEAD_DIM) for h in HEAD_PERM])
    return np.concatenate([fa, ma, mb, qa, ia, ga, za, qb[hp], zb[hp], kb, vb]), hp


def _take_cols(w, idx, axis):
    breaks = np.flatnonzero(np.diff(idx) != 1) + 1
    runs = np.split(idx, breaks)
    return jnp.concatenate([lax.slice_in_dim(w, int(r[0]), int(r[-1]) + 1, axis=axis) for r in runs],
                           axis=axis)


def _cparams(sem):
    return pltpu.CompilerParams(dimension_semantics=sem, vmem_limit_bytes=VMEM_LIMIT)


def _mod_kernel(c_ref, w_ref, b_ref, o_ref):
    acc = jnp.dot(c_ref[...].astype(BF16), w_ref[0].astype(BF16), preferred_element_type=F32)
    o_ref[0] = acc + b_ref[0]


def _modulation(c_all, ada_w, ada_b):
    n, d3, tn = c_all.shape[0], 3 * D_MODEL, 768
    return pl.pallas_call(
        _mod_kernel,
        grid=(DEPTH, d3 // tn),
        in_specs=[pl.BlockSpec((n, D_MODEL), lambda l, j: (0, 0)),
                  pl.BlockSpec((1, D_MODEL, tn), lambda l, j: (l, 0, j)),
                  pl.BlockSpec((1, 1, tn), lambda l, j: (l, 0, j))],
        out_specs=pl.BlockSpec((1, n, tn), lambda l, j: (l, 0, j)),
        out_shape=jax.ShapeDtypeStruct((DEPTH, n, d3), F32),
        compiler_params=_cparams(("arbitrary", "arbitrary")),
        name="adaln_mod",
    )(c_all, ada_w, ada_b.reshape(DEPTH, 1, d3))


def _inproj_kernel(x_ref, shift_ref, scale_ref, g_ref, w_ref, of_ref, ob_ref, h_ref, *, nb, tt, rc, nf):
    n = pl.program_id(2)

    @pl.when(n == 0)
    def _():
        g = g_ref[...]
        for b in range(nb):
            mul = 1.0 + scale_ref[b]
            add = shift_ref[b]

            def rows(r, carry, b=b, mul=mul, add=add):
                r0 = pl.multiple_of(r * rc, rc)
                x = x_ref[b, pl.ds(r0, rc), :]
                y = x * lax.rsqrt(jnp.mean(x * x, axis=-1, keepdims=True) + EPS)
                h = (y * g) * mul + add
                h_ref[pl.ds(b * tt + r0, rc), :] = h.astype(BF16)
                return carry

            lax.fori_loop(0, tt // rc, rows, 0)

    acc = jnp.dot(h_ref[...], w_ref[...], preferred_element_type=F32)

    @pl.when(n < nf)
    def _():
        of_ref[...] = acc.reshape(of_ref.shape)

    @pl.when(n >= nf)
    def _():
        ob_ref[...] = acc.astype(BF16).reshape(ob_ref.shape)


def _in_projection(x, shift, scale, norm_g, w, *, nb, tt, tn=512):
    bsz, t, _ = x.shape
    rc = min(tt, 128)
    nf = N_F32 // tn
    kern = functools.partial(_inproj_kernel, nb=nb, tt=tt, rc=rc, nf=nf)
    return pl.pallas_call(
        kern,
        grid=(bsz // nb, t // tt, N_IN // tn),
        in_specs=[pl.BlockSpec((nb, tt, D_MODEL), lambda i, j, n: (i, j, 0), pipeline_mode=pl.Buffered(1)),
                  pl.BlockSpec((nb, 1, D_MODEL), lambda i, j, n: (i, 0, 0)),
                  pl.BlockSpec((nb, 1, D_MODEL), lambda i, j, n: (i, 0, 0)),
                  pl.BlockSpec((1, D_MODEL), lambda i, j, n: (0, 0)),
                  pl.BlockSpec((D_MODEL, tn), lambda i, j, n: (0, n))],
        out_specs=[pl.BlockSpec((nb, tt, tn), lambda i, j, n: (i, j, jnp.minimum(n, nf - 1))),
                   pl.BlockSpec((nb, tt, tn), lambda i, j, n: (i, j, jnp.maximum(n - nf, 0)))],
        out_shape=[jax.ShapeDtypeStruct((bsz, t, N_F32), F32),
                   jax.ShapeDtypeStruct((bsz, t, N_B16), BF16)],
        scratch_shapes=[pltpu.VMEM((nb * tt, D_MODEL), BF16)],
        compiler_params=_cparams(("arbitrary", "arbitrary", "arbitrary")),
        name="in_projection",
    )(x, shift, scale, norm_g, w)


HEADS_PER_LOOP = 8
LEVELS = CHUNK.bit_length() - 1
MXU_LEVELS = 2


def _hgrn_kernel(q_ref, f_ref, i_ref, g_ref, z_ref, lb_ref, ng_ref, s0_ref, mz_ref,
                 u_ref, sout_ref, st_ref, *, n_chunks):
    t = pl.program_id(1)

    @pl.when(t == 0)
    def _():
        for h in range(HGRN_HEADS):
            st_ref[h] = s0_ref[0, h].T

    row = lax.broadcasted_iota(jnp.int32, (CHUNK, CHUNK), 0)
    col = lax.broadcasted_iota(jnp.int32, (CHUNK, CHUNK), 1)
    code = jnp.where(row > col, row ^ col, jnp.where(row == col, 0, 2 * CHUNK))
    small_masks = [jnp.right_shift(code, l) == 1 for l in range(LEVELS - MXU_LEVELS)]
    diag_mask = code == 0
    sub_row = lax.broadcasted_iota(jnp.int32, (SUBLANES, HGRN_DK), 0)
    upper_half = [(jnp.right_shift(sub_row, l) & 1) == 1 for l in range(SUBLANES.bit_length() - 1)]
    nt_dims = (((1,), (1,)), ((), ()))
    tn_dims = (((0,), (0,)), ((), ()))

    def padded(x, r0):
        parts = []
        if r0:
            parts.append(jnp.zeros((r0, HGRN_DK), F32))
        parts.append(x)
        if CHUNK - r0 - x.shape[0]:
            parts.append(jnp.zeros((CHUNK - r0 - x.shape[0], HGRN_DK), F32))
        return jnp.concatenate(parts, axis=0).astype(BF16)

    def head_chunk(h, rows):
        cols = slice(h * HGRN_DK, (h + 1) * HGRN_DK)
        lb = lb_ref[:, cols]
        a = f_ref[0, rows, cols]
        qa = q_ref[0, rows, cols].astype(F32)
        v = i_ref[0, rows, cols]
        a2 = a * LOG2E
        e = jnp.exp2(-a2)
        l1e = jnp.log(1.0 + e) * LOG2E
        log_f = jnp.log(1.0 + lb * e) * LOG2E - l1e
        k = jnp.exp2((jnp.log(1.0 - lb) * LOG2E - a2) - l1e)
        q = qa * jax.nn.sigmoid(qa)

        hi = log_f.astype(BF16)
        r1 = log_f - hi.astype(F32)
        mid = r1.astype(BF16)
        lo = (r1 - mid.astype(F32)).astype(BF16)
        zall = yield jnp.concatenate([hi, mid, lo], axis=0)
        b = zall[:CHUNK]

        def level_sums(l):
            i = LEVELS - l
            return zall[i * CHUNK:(i + 1) * CHUNK]

        st = st_ref[h]
        q_dec = (q * jnp.exp2(b)).astype(BF16)
        st16 = st.astype(BF16)
        x_cols, y_cols = [], []
        for l in range(LEVELS - 1, LEVELS - 1 - MXU_LEVELS, -1):
            half = 1 << l
            w = jnp.exp2(level_sums(l))
            for base in range(0, CHUNK, 2 * half):
                lo_rows = slice(base, base + half)
                hi_rows = slice(base + half, base + 2 * half)
                x_cols.append(padded(q[hi_rows] * w[hi_rows], base + half))
                y_cols.append(padded(k[lo_rows] * w[lo_rows], base))
        x_big, y_big = jnp.concatenate(x_cols, axis=1), jnp.concatenate(y_cols, axis=1)
        pairs = [(q.astype(BF16), k.astype(BF16))]
        for l in range(LEVELS - MXU_LEVELS):
            w = jnp.exp2(level_sums(l))
            if (1 << l) < SUBLANES:
                side = jnp.concatenate([jnp.where(upper_half[l], q[r:r + SUBLANES], k[r:r + SUBLANES])
                                        for r in range(0, CHUNK, SUBLANES)], axis=0)
            else:
                side = jnp.concatenate([q[r:r + SUBLANES] if (r >> l) & 1 else k[r:r + SUBLANES]
                                        for r in range(0, CHUNK, SUBLANES)], axis=0)
            tl = (w * side).astype(BF16)
            pairs.append((tl, tl))
        b_end = b[CHUNK - 1:CHUNK]
        k_dec = (k * jnp.exp2(b_end - b)).astype(BF16)
        st_new = st * jnp.exp2(b_end)
        yield
        o_inter = lax.dot_general(q_dec, st16, nt_dims, preferred_element_type=F32)
        a_big = lax.dot_general(x_big, y_big, nt_dims, preferred_element_type=F32)
        a_small = [lax.dot_general(x, y, nt_dims, preferred_element_type=F32) for x, y in pairs]
        st_add = lax.dot_general(v, k_dec, tn_dims, preferred_element_type=F32)
        yield
        small = jnp.where(diag_mask, a_small[0], 0.0)
        for l in range(LEVELS - MXU_LEVELS):
            small = jnp.where(small_masks[l], a_small[1 + l], small)
        a_mat = (a_big + small).astype(BF16)
        st_ref[h] = st_new + st_add
        yield
        o_intra = jnp.dot(a_mat, v, preferred_element_type=F32)
        yield
        o = o_inter + o_intra
        y = o * lax.rsqrt(jnp.mean(o * o, axis=-1, keepdims=True) + EPS) * ng_ref[:, cols]
        ga = g_ref[0, rows, cols].astype(F32)
        za = z_ref[0, rows, cols].astype(F32)
        u = y * jax.nn.sigmoid(ga) * (za * jax.nn.sigmoid(za))
        u_ref[0, rows, cols] = u.astype(BF16)

    for h0 in range(0, HGRN_HEADS, HEADS_PER_LOOP):
        def chunk(c, carry, h0=h0):
            rows = pl.ds(pl.multiple_of(c * CHUNK, CHUNK), CHUNK)
            stages = [head_chunk(h, rows) for h in range(h0, h0 + HEADS_PER_LOOP)]
            splits = [next(s) for s in stages]
            sums = [jnp.dot(mz_ref[...], jnp.concatenate(splits[i:i + 2], axis=1), preferred_element_type=F32)
                    for i in range(0, HEADS_PER_LOOP, 2)]
            for i, s in enumerate(stages):
                s.send(sums[i // 2][:, (i % 2) * HGRN_DK:(i % 2 + 1) * HGRN_DK])
            while stages:
                stages = [s for s in stages if next(s, "done") != "done"]
            return carry

        lax.fori_loop(0, n_chunks, chunk, 0)

    @pl.when(t == pl.num_programs(1) - 1)
    def _():
        for h in range(HGRN_HEADS):
            sout_ref[0, h] = st_ref[h].T


def _decay_sum_matrix():
    m = np.zeros((1 + LEVELS, CHUNK, CHUNK), np.float32)
    idx = np.arange(CHUNK)
    m[0] = idx[:, None] >= idx[None, :]
    for i, l in enumerate(range(LEVELS - 1, -1, -1)):
        half = 1 << l
        for r in range(CHUNK):
            pos = r % (2 * half)
            first_upper = r - pos + half
            if pos >= half:
                m[1 + i, r, first_upper:r + 1] = 1.0
            else:
                m[1 + i, r, r + 1:first_upper] = 1.0
    m = m.reshape(-1, CHUNK)
    return jnp.asarray(np.concatenate([m, m, m], axis=1), dtype=BF16)


def _hgrn(pf, pb, lb, ng, s0, *, tt):
    bsz, t, _ = pb.shape

    def col(c0):
        return pl.BlockSpec((1, tt, D_A), lambda b, j, c0=c0: (b, j, c0 // D_A))

    vec = pl.BlockSpec((1, D_A), lambda b, j: (0, 0))
    state = pl.BlockSpec((1, HGRN_HEADS, HGRN_DK, HGRN_DV), lambda b, j: (b, 0, 0, 0))
    mz = _decay_sum_matrix()
    return pl.pallas_call(
        functools.partial(_hgrn_kernel, n_chunks=tt // CHUNK),
        grid=(bsz, t // tt),
        in_specs=[col(COL_QA), col(0), col(COL_IA), col(COL_GA), col(COL_ZA), vec, vec, state,
                  pl.BlockSpec(mz.shape, lambda b, j: (0, 0))],
        out_specs=[pl.BlockSpec((1, tt, D_A), lambda b, j: (b, j, 0)), state],
        out_shape=[jax.ShapeDtypeStruct((bsz, t, D_A), BF16),
                   jax.ShapeDtypeStruct((bsz, HGRN_HEADS, HGRN_DK, HGRN_DV), F32)],
        scratch_shapes=[pltpu.VMEM((HGRN_HEADS, HGRN_DV, HGRN_DK), F32)],
        compiler_params=_cparams(("arbitrary", "arbitrary")),
        name="hgrn2",
    )(pb, pf, pb, pb, pb, lb, ng, s0, mz)


def _rope(x, cos, sin_lo, sin_hi):
    half = ROT_DIM // 2
    return x * cos + pltpu.roll(x, LANES - half, 1) * sin_lo + pltpu.roll(x, half, 1) * sin_hi


def _attn_kernel(sink_ref, q_ref, kv_ref, z_ref, cos_ref, slo_ref, shi_ref, ck_ref, cv_ref,
                 u_ref, ko_ref, vo_ref, kbuf, vbuf, *, tt, masked, outw):
    t = pl.program_id(1)
    last = t == pl.num_programs(1) - 1
    n_chunks = tt // CHUNK
    n_slabs = KV_W // LANES

    @pl.when(t == 0)
    def _():
        kbuf[0:WINDOW, :] = ck_ref[0].astype(BF16)
        vbuf[0:WINDOW, :] = cv_ref[0].astype(BF16)

    if tt >= WINDOW:
        @pl.when(t > 0)
        def _():
            kbuf[0:WINDOW, :] = kbuf[tt:tt + WINDOW, :]
            vbuf[0:WINDOW, :] = vbuf[tt:tt + WINDOW, :]

    for j in range(n_slabs):
        lanes = slice(j * LANES, (j + 1) * LANES)
        kr = _rope(kv_ref[0, :, lanes].astype(F32), cos_ref[...], slo_ref[...], shi_ref[...])
        kbuf[WINDOW:WINDOW + tt, lanes] = kr.astype(BF16)

        @pl.when(last)
        def _(kr=kr, lanes=lanes):
            ko_ref[0, :, lanes] = kr[tt - outw:, :]
    vbuf[WINDOW:WINDOW + tt, :] = kv_ref[0, :, KV_W:2 * KV_W]

    @pl.when(last)
    def _():
        vo_ref[0] = kv_ref[0, tt - outw:tt, KV_W:2 * KV_W].astype(F32)

    lane = lax.broadcasted_iota(jnp.int32, (CHUNK, LANES), 1)
    low = lane < HEAD_DIM
    sink_rows = [jnp.concatenate([jnp.full((1, CHUNK), sink_ref[HEAD_PERM[2 * (GROUP * j + p) + half]], F32)
                                  for p in range(GROUP) for half in range(2)], axis=1)
                 for j in range(n_slabs)]
    nt_dims = (((1,), (1,)), ((), ()))
    tn_dims = (((0,), (0,)), ((), ()))

    def chunk(c, carry):
        r0 = pl.multiple_of(c * CHUNK, CHUNK)
        rows = pl.ds(r0, CHUNK)
        band = pl.ds(r0, BAND)
        cq, sl, sh = cos_ref[rows, :], slo_ref[rows, :], shi_ref[rows, :]
        seq_chunk = t * n_chunks + c

        def slab_chunk(j):
            lanes = slice(j * LANES, (j + 1) * LANES)
            kj = kbuf[band, lanes]
            vj = vbuf[band, lanes]
            qs = []
            for p in range(GROUP):
                slab = GROUP * j + p
                x = _rope(q_ref[0, rows, slab * LANES:(slab + 1) * LANES].astype(F32), cq, sl, sh) * ATTN_SCALE
                qs += [jnp.where(low, x, 0.0).astype(BF16), jnp.where(low, 0.0, x).astype(BF16)]
            qst = jnp.concatenate(qs, axis=0)
            yield
            s = lax.dot_general(kj, qst, nt_dims, preferred_element_type=F32)
            yield
            if masked:
                blocks = [jnp.where(seq_chunk >= N_LOOKBACK_CHUNKS - i, s[i * CHUNK:(i + 1) * CHUNK], NEG)
                          for i in range(N_LOOKBACK_CHUNKS)]
                s = jnp.concatenate(blocks + [s[WINDOW:]], axis=0)
            sink = sink_rows[j]
            m = jnp.maximum(jnp.max(s, axis=0, keepdims=True), sink)
            pr = jnp.exp(s - m)
            den = jnp.sum(pr, axis=0, keepdims=True) + jnp.exp(sink - m)
            pr16 = pr.astype(BF16)
            yield
            o = lax.dot_general(vj, pr16, tn_dims, preferred_element_type=F32)
            yield
            o = o * (1.0 / den)
            for p in range(GROUP):
                slab = GROUP * j + p
                ot = o[:, p * LANES:(p + 1) * LANES].T
                ob = jnp.where(low, ot[:CHUNK], ot[CHUNK:])
                zb = z_ref[0, rows, slab * LANES:(slab + 1) * LANES].astype(F32)
                u_ref[0, rows, slab * LANES:(slab + 1) * LANES] = (
                    ob * (zb * jax.nn.sigmoid(zb))).astype(BF16)

        stages = [slab_chunk(j) for j in range(n_slabs)]
        while stages:
            stages = [s for s in stages if next(s, "done") != "done"]
        return carry

    lax.fori_loop(0, n_chunks, chunk, 0)


def _rope_tables(pos):
    half = ROT_DIM // 2
    inv = (ROPE_THETA ** (-np.arange(0, ROT_DIM, 2) / ROT_DIM)).astype(np.float32)
    ang = pos.astype(F32)[:, None] * inv[None, :]
    cos, sin = jnp.cos(ang), jnp.sin(ang)
    n = pos.shape[0]
    rest = HEAD_DIM - ROT_DIM
    cos_h = jnp.concatenate([cos, cos, jnp.ones((n, rest), F32)], axis=1)
    slo_h = jnp.concatenate([-sin, jnp.zeros((n, half + rest), F32)], axis=1)
    shi_h = jnp.concatenate([jnp.zeros((n, half), F32), sin, jnp.zeros((n, rest), F32)], axis=1)
    reps = LANES // HEAD_DIM
    return tuple(jnp.tile(a, (1, reps)) for a in (cos_h, slo_h, shi_h))


def _attention(pb, sinks, tables, cache_k, cache_v, *, tt, masked, outw):
    bsz, t, _ = pb.shape
    tab = pl.BlockSpec((tt, LANES), lambda b, j, s: (j, 0))
    cache = pl.BlockSpec((1, WINDOW, KV_W), lambda b, j, s: (b, 0, 0))
    win = pl.BlockSpec((1, outw, KV_W), lambda b, j, s: (b, 0, 0))
    grid_spec = pltpu.PrefetchScalarGridSpec(
        num_scalar_prefetch=1,
        grid=(bsz, t // tt),
        in_specs=[pl.BlockSpec((1, tt, D_B), lambda b, j, s: (b, j, COL_QB // D_B)),
                  pl.BlockSpec((1, tt, 2 * KV_W), lambda b, j, s: (b, j, COL_KV // (2 * KV_W))),
                  pl.BlockSpec((1, tt, D_B), lambda b, j, s: (b, j, COL_ZB // D_B)),
                  tab, tab, tab, cache, cache],
        out_specs=[pl.BlockSpec((1, tt, D_B), lambda b, j, s: (b, j, 0)), win, win],
        scratch_shapes=[pltpu.VMEM((WINDOW + tt, KV_W), BF16), pltpu.VMEM((WINDOW + tt, KV_W), BF16)],
    )
    return pl.pallas_call(
        functools.partial(_attn_kernel, tt=tt, masked=masked, outw=outw),
        grid_spec=grid_spec,
        out_shape=[jax.ShapeDtypeStruct((bsz, t, D_B), BF16),
                   jax.ShapeDtypeStruct((bsz, outw, KV_W), F32),
                   jax.ShapeDtypeStruct((bsz, outw, KV_W), F32)],
        compiler_params=_cparams(("arbitrary", "arbitrary")),
        name="swa_attention",
    )(sinks, pb, pb, pb, *tables, cache_k, cache_v)


def _out_kernel(ua_ref, ub_ref, ma_ref, mb_ref, x_ref, gate_ref, wpa_ref, wpb_ref, wout_ref, fg_ref,
                y_ref, *, nb, tt, final):
    m = nb * tt
    pa = jnp.dot(ua_ref[...].reshape(m, D_A), wpa_ref[...], preferred_element_type=F32)
    pb = jnp.dot(ub_ref[...].reshape(m, D_B), wpb_ref[...], preferred_element_type=F32)
    ma = ma_ref[...].reshape(m, D_MODEL).astype(F32)
    mb = mb_ref[...].reshape(m, D_MODEL).astype(F32)
    merged = jax.nn.sigmoid(ma) * pa + jax.nn.sigmoid(mb) * pb
    o = jnp.dot(merged.astype(BF16), wout_ref[...], preferred_element_type=F32)
    y = x_ref[...] + gate_ref[...] * o.reshape(nb, tt, D_MODEL)
    if final:
        y = (y * lax.rsqrt(jnp.mean(y * y, axis=-1, keepdims=True) + EPS)) * fg_ref[...]
    y_ref[...] = y


def _output(ua, ub, pb, x, gate, wpa, wpb, wout, final_g, *, nb, tt, final):
    bsz, t, _ = x.shape
    tok = lambda i, j: (i, j, 0)
    const = lambda i, j: (0, 0)
    single = pl.Buffered(1)
    return pl.pallas_call(
        functools.partial(_out_kernel, nb=nb, tt=tt, final=final),
        grid=(bsz // nb, t // tt),
        in_specs=[pl.BlockSpec((nb, tt, D_A), tok),
                  pl.BlockSpec((nb, tt, D_B), tok),
                  pl.BlockSpec((nb, tt, D_MODEL), lambda i, j: (i, j, COL_MA // D_MODEL)),
                  pl.BlockSpec((nb, tt, D_MODEL), lambda i, j: (i, j, COL_MB // D_MODEL)),
                  pl.BlockSpec((nb, tt, D_MODEL), tok),
                  pl.BlockSpec((nb, 1, D_MODEL), lambda i, j: (i, 0, 0)),
                  pl.BlockSpec((D_A, D_MODEL), const, pipeline_mode=single),
                  pl.BlockSpec((D_B, D_MODEL), const, pipeline_mode=single),
                  pl.BlockSpec((D_MODEL, D_MODEL), const, pipeline_mode=single),
                  pl.BlockSpec((1, D_MODEL), const)],
        out_specs=pl.BlockSpec((nb, tt, D_MODEL), tok),
        out_shape=jax.ShapeDtypeStruct((bsz, t, D_MODEL), F32),
        compiler_params=_cparams(("arbitrary", "arbitrary")),
        name="merge_output",
    )(ua, ub, pb, pb, x, gate, wpa, wpb, wout, final_g)


def _tiles(t):
    if t >= 1024:
        return dict(proj=(1, min(t, 2048)), out=(1, 256), hgrn_tt=512, attn_tt=512)
    return dict(proj=(16, t), out=(4, t), hgrn_tt=t, attn_tt=t)


def _layer(x, mod, w, lb, s0, cache_k, cache_v, tables, *, final, final_g):
    bsz, t, _ = x.shape
    tl = _tiles(t)
    shift, scale, gate = (mod[:, None, i * D_MODEL:(i + 1) * D_MODEL] for i in range(3))
    nb, tt = tl["proj"]
    pf, pb = _in_projection(x, shift, scale, w["norm_g"], w["w_in"], nb=min(nb, bsz), tt=tt)
    ua, s_new = _hgrn(pf, pb, lb, w["hgrn_g"], s0, tt=tl["hgrn_tt"])
    masked = cache_k is None
    if masked:
        cache_k = jnp.zeros((bsz, WINDOW, KV_W), F32)
        cache_v = cache_k
    outw = min(t, WINDOW)
    ub, k_new, v_new = _attention(pb, w["sinks"], tables, cache_k, cache_v,
                                  tt=tl["attn_tt"], masked=masked, outw=outw)
    nb, tt = tl["out"]
    y = _output(ua, ub, pb, x, gate, w["w_pa"], w["w_pb"], w["w_out"], final_g,
                nb=min(nb, bsz), tt=tt, final=final)
    shape = (bsz, outw, N_KV, HEAD_DIM)
    return y, s_new, k_new.reshape(shape), v_new.reshape(shape)


def kernel(x_prompt, x_sample, c_prompt, c_sample, state_hgrn, cache_win_k, cache_win_v, ada_w, ada_b,
           norm_g, w_in, lb_logits, hgrn_norm_g, sinks, w_branch_a, w_branch_b, w_out, final_norm_g):
    bp, tp = x_prompt.shape[0], x_prompt.shape[1]
    bs, ts = x_sample.shape[0], x_sample.shape[1]

    perm, head_cols = _in_col_perm()
    w_in_p = _take_cols(w_in.astype(BF16), perm, 2)
    w_pa = w_branch_a.astype(BF16)
    w_pb = _take_cols(w_branch_b.astype(BF16), head_cols, 1)
    w_o = w_out.astype(BF16)
    prob = jax.nn.softmax(lb_logits.astype(F32), axis=0)
    lb = jnp.cumsum(prob, axis=0) - prob[:1]
    final_g = final_norm_g.reshape(1, D_MODEL)

    mod = _modulation(jnp.concatenate([c_prompt, c_sample], axis=0), ada_w, ada_b)
    tab_p = _rope_tables(jnp.arange(tp))
    tab_s = _rope_tables(PAST_LEN + jnp.arange(ts))
    zero_state = jnp.zeros((bp, HGRN_HEADS, HGRN_DK, HGRN_DV), F32)

    hp, hs = x_prompt, x_sample
    outs = [[] for _ in range(6)]
    for l in range(DEPTH):
        w = dict(norm_g=norm_g[l].reshape(1, D_MODEL), w_in=w_in_p[l], hgrn_g=hgrn_norm_g[l].reshape(1, D_A),
                 sinks=sinks[l], w_pa=w_pa[l], w_pb=w_pb[l], w_out=w_o[l])
        lbl = lb[l].reshape(1, D_A)
        final = l == DEPTH - 1
        hp, s_p, k_p, v_p = _layer(hp, mod[l, :bp], w, lbl, zero_state, None, None, tab_p,
                                   final=final, final_g=final_g)
        hs, s_s, k_s, v_s = _layer(hs, mod[l, bp:], w, lbl, state_hgrn[l],
                                   cache_win_k[l].reshape(bs, WINDOW, KV_W),
                                   cache_win_v[l].reshape(bs, WINDOW, KV_W), tab_s,
                                   final=final, final_g=final_g)
        for acc, val in zip(outs, (s_p, k_p, v_p, s_s, k_s, v_s)):
            acc.append(val)
    return (hp, hs) + tuple(jnp.stack(o) for o in outs)
```

```python
import functools

import jax
import jax.numpy as jnp
import numpy as np
from jax import lax
from jax.experimental import pallas as pl
from jax.experimental.pallas import tpu as pltpu

F32 = jnp.float32
BF16 = jnp.bfloat16

D_MODEL = 2048
DEPTH = 4
PAST_LEN = 4096
CHUNK = 64
SUB = 16
NSUB = CHUNK // SUB
D_A = D_MODEL // 2
HGRN_DK = 128
HGRN_HEADS = D_A // HGRN_DK
HGRN_DV = D_A // HGRN_HEADS
D_B = D_MODEL // 2
HEAD_DIM = 64
N_Q = D_B // HEAD_DIM
N_KV = N_Q // 4
GROUP = N_Q // N_KV
WINDOW = 128
N_LOOKBACK_CHUNKS = WINDOW // CHUNK
BAND = WINDOW + CHUNK
ROT_DIM = HEAD_DIM // 4
ROPE_THETA = 500000.0
ATTN_SCALE = HEAD_DIM ** -0.5
EPS = 1e-6
NEG = -1e30
LOG2E = 1.4426950408889634
N_IN = 5 * D_A + N_Q * HEAD_DIM + 2 * N_KV * HEAD_DIM + D_B + 2 * D_MODEL

LANES = 128
SUBLANES = 8
KV_W = N_KV * HEAD_DIM
VMEM_LIMIT = 56 * 1024 * 1024

N_F32 = D_A
N_B16 = N_IN - N_F32
COL_MA, COL_MB = 0, D_MODEL
COL_QA, COL_IA, COL_GA, COL_ZA = (2 * D_MODEL + i * D_A for i in range(4))
COL_QB = COL_ZA + D_A
COL_ZB = COL_QB + D_B
COL_KV = COL_ZB + D_B
PROJ_TN = 512


def _proj_tile_order():
    sizes = dict(qa=D_A, fa=D_A, ia=D_A, ga=D_A, za=D_A, qb=D_B, kv=2 * KV_W, zb=D_B, ma=D_MODEL, mb=D_MODEL)
    start, first = 0, {}
    for name, n in sizes.items():
        first[name] = start // PROJ_TN
        start += n
    order = []
    for name in ("fa", "ma", "mb", "qa", "ia", "ga", "za", "qb", "zb", "kv"):
        order += range(first[name], first[name] + sizes[name] // PROJ_TN)
    return np.asarray(order, np.int32)


def _cparams(sem):
    return pltpu.CompilerParams(dimension_semantics=sem, vmem_limit_bytes=VMEM_LIMIT)


def _mod_kernel(c_ref, w_ref, b_ref, o_ref):
    acc = jnp.dot(c_ref[...].astype(BF16), w_ref[0].astype(BF16), preferred_element_type=F32)
    o_ref[0] = acc + b_ref[0]


def _modulation(c_all, ada_w, ada_b):
    n, d3, tn = c_all.shape[0], 3 * D_MODEL, 768
    return pl.pallas_call(
        _mod_kernel,
        grid=(DEPTH, d3 // tn),
        in_specs=[pl.BlockSpec((n, D_MODEL), lambda l, j: (0, 0)),
                  pl.BlockSpec((1, D_MODEL, tn), lambda l, j: (l, 0, j)),
                  pl.BlockSpec((1, 1, tn), lambda l, j: (l, 0, j))],
        out_specs=pl.BlockSpec((1, n, tn), lambda l, j: (l, 0, j)),
        out_shape=jax.ShapeDtypeStruct((DEPTH, n, d3), F32),
        compiler_params=_cparams(("arbitrary", "arbitrary")),
        name="adaln_mod",
    )(c_all, ada_w, ada_b.reshape(DEPTH, 1, d3))


def _inproj_kernel(src_ref, x_ref, shift_ref, scale_ref, g_ref, w_ref, of_ref, ob_ref, h_ref, *,
                   nb, tt, rc, nf):
    del src_ref
    n = pl.program_id(2)

    @pl.when(n == 0)
    def _():
        g = g_ref[...]
        for b in range(nb):
            mul = 1.0 + scale_ref[b]
            add = shift_ref[b]

            def rows(r, carry, b=b, mul=mul, add=add):
                r0 = pl.multiple_of(r * rc, rc)
                x = x_ref[b, pl.ds(r0, rc), :]
                y = x * lax.rsqrt(jnp.mean(x * x, axis=-1, keepdims=True) + EPS)
                h = (y * g) * mul + add
                h_ref[pl.ds(b * tt + r0, rc), :] = h.astype(BF16)
                return carry

            lax.fori_loop(0, tt // rc, rows, 0)

    acc = jnp.dot(h_ref[...], w_ref[...], preferred_element_type=F32)

    @pl.when(n < nf)
    def _():
        of_ref[...] = acc.reshape(of_ref.shape)

    @pl.when(n >= nf)
    def _():
        ob_ref[...] = acc.astype(BF16).reshape(ob_ref.shape)


def _in_projection(x, shift, scale, norm_g, w, *, nb, tt):
    bsz, t, _ = x.shape
    tn = PROJ_TN
    rc = min(tt, 128)
    nf = N_F32 // tn
    kern = functools.partial(_inproj_kernel, nb=nb, tt=tt, rc=rc, nf=nf)
    grid_spec = pltpu.PrefetchScalarGridSpec(
        num_scalar_prefetch=1,
        grid=(bsz // nb, t // tt, N_IN // tn),
        in_specs=[pl.BlockSpec((nb, tt, D_MODEL), lambda i, j, n, src: (i, j, 0),
                               pipeline_mode=pl.Buffered(1)),
                  pl.BlockSpec((nb, 1, D_MODEL), lambda i, j, n, src: (i, 0, 0)),
                  pl.BlockSpec((nb, 1, D_MODEL), lambda i, j, n, src: (i, 0, 0)),
                  pl.BlockSpec((1, D_MODEL), lambda i, j, n, src: (0, 0)),
                  pl.BlockSpec((D_MODEL, tn), lambda i, j, n, src: (0, src[n]))],
        out_specs=[pl.BlockSpec((nb, tt, tn), lambda i, j, n, src: (i, j, jnp.minimum(n, nf - 1))),
                   pl.BlockSpec((nb, tt, tn), lambda i, j, n, src: (i, j, jnp.maximum(n - nf, 0)))],
        scratch_shapes=[pltpu.VMEM((nb * tt, D_MODEL), BF16)],
    )
    return pl.pallas_call(
        kern,
        grid_spec=grid_spec,
        out_shape=[jax.ShapeDtypeStruct((bsz, t, N_F32), F32),
                   jax.ShapeDtypeStruct((bsz, t, N_B16), BF16)],
        compiler_params=_cparams(("arbitrary", "arbitrary", "arbitrary")),
        name="in_projection",
    )(jnp.asarray(_proj_tile_order()), x, shift, scale, norm_g, w)


HEADS_PER_LOOP = 8
LEVELS = CHUNK.bit_length() - 1
MXU_LEVELS = 2


def _hgrn_kernel(q_ref, f_ref, i_ref, g_ref, z_ref, lb_ref, ng_ref, s0_ref, mz_ref,
                 u_ref, sout_ref, st_ref, *, n_chunks):
    t = pl.program_id(1)

    @pl.when(t == 0)
    def _():
        for h in range(HGRN_HEADS):
            st_ref[h] = s0_ref[0, h].T

    row = lax.broadcasted_iota(jnp.int32, (CHUNK, CHUNK), 0)
    col = lax.broadcasted_iota(jnp.int32, (CHUNK, CHUNK), 1)
    code = jnp.where(row > col, row ^ col, jnp.where(row == col, 0, 2 * CHUNK))
    small_masks = [jnp.right_shift(code, l) == 1 for l in range(LEVELS - MXU_LEVELS)]
    diag_mask = code == 0
    sub_row = lax.broadcasted_iota(jnp.int32, (SUBLANES, HGRN_DK), 0)
    upper_half = [(jnp.right_shift(sub_row, l) & 1) == 1 for l in range(SUBLANES.bit_length() - 1)]
    nt_dims = (((1,), (1,)), ((), ()))
    tn_dims = (((0,), (0,)), ((), ()))

    def padded(x, r0):
        parts = []
        if r0:
            parts.append(jnp.zeros((r0, HGRN_DK), F32))
        parts.append(x)
        if CHUNK - r0 - x.shape[0]:
            parts.append(jnp.zeros((CHUNK - r0 - x.shape[0], HGRN_DK), F32))
        return jnp.concatenate(parts, axis=0).astype(BF16)

    def head_chunk(h, rows):
        cols = slice(h * HGRN_DK, (h + 1) * HGRN_DK)
        lb = lb_ref[:, cols]
        a = f_ref[0, rows, cols]
        qa = q_ref[0, rows, cols].astype(F32)
        v = i_ref[0, rows, cols]
        a2 = a * LOG2E
        e = jnp.exp2(-a2)
        l1e = jnp.log(1.0 + e) * LOG2E
        log_f = jnp.log(1.0 + lb * e) * LOG2E - l1e
        k = jnp.exp2((jnp.log(1.0 - lb) * LOG2E - a2) - l1e)
        q = qa * jax.nn.sigmoid(qa)

        hi = log_f.astype(BF16)
        r1 = log_f - hi.astype(F32)
        mid = r1.astype(BF16)
        lo = (r1 - mid.astype(F32)).astype(BF16)
        zall = yield jnp.concatenate([hi, mid, lo], axis=0)
        b = zall[:CHUNK]

        def level_sums(l):
            i = LEVELS - l
            return zall[i * CHUNK:(i + 1) * CHUNK]

        st = st_ref[h]
        q_dec = (q * jnp.exp2(b)).astype(BF16)
        st16 = st.astype(BF16)
        x_cols, y_cols = [], []
        for l in range(LEVELS - 1, LEVELS - 1 - MXU_LEVELS, -1):
            half = 1 << l
            w = jnp.exp2(level_sums(l))
            for base in range(0, CHUNK, 2 * half):
                lo_rows = slice(base, base + half)
                hi_rows = slice(base + half, base + 2 * half)
                x_cols.append(padded(q[hi_rows] * w[hi_rows], base + half))
                y_cols.append(padded(k[lo_rows] * w[lo_rows], base))
        x_big, y_big = jnp.concatenate(x_cols, axis=1), jnp.concatenate(y_cols, axis=1)
        pairs = [(q.astype(BF16), k.astype(BF16))]
        for l in range(LEVELS - MXU_LEVELS):
            w = jnp.exp2(level_sums(l))
            if (1 << l) < SUBLANES:
                side = jnp.concatenate([jnp.where(upper_half[l], q[r:r + SUBLANES], k[r:r + SUBLANES])
                                        for r in range(0, CHUNK, SUBLANES)], axis=0)
            else:
                side = jnp.concatenate([q[r:r + SUBLANES] if (r >> l) & 1 else k[r:r + SUBLANES]
                                        for r in range(0, CHUNK, SUBLANES)], axis=0)
            tl = (w * side).astype(BF16)
            pairs.append((tl, tl))
        b_end = b[CHUNK - 1:CHUNK]
        k_dec = (k * jnp.exp2(b_end - b)).astype(BF16)
        st_new = st * jnp.exp2(b_end)
        yield
        o_inter = lax.dot_general(q_dec, st16, nt_dims, preferred_element_type=F32)
        a_big = lax.dot_general(x_big, y_big, nt_dims, preferred_element_type=F32)
        a_small = [lax.dot_general(x, y, nt_dims, preferred_element_type=F32) for x, y in pairs]
        st_add = lax.dot_general(v, k_dec, tn_dims, preferred_element_type=F32)
        yield
        small = jnp.where(diag_mask, a_small[0], 0.0)
        for l in range(LEVELS - MXU_LEVELS):
            small = jnp.where(small_masks[l], a_small[1 + l], small)
        a_mat = (a_big + small).astype(BF16)
        st_ref[h] = st_new + st_add
        yield
        o_intra = jnp.dot(a_mat, v, preferred_element_type=F32)
        yield
        o = o_inter + o_intra
        y = o * lax.rsqrt(jnp.mean(o * o, axis=-1, keepdims=True) + EPS) * ng_ref[:, cols]
        ga = g_ref[0, rows, cols].astype(F32)
        za = z_ref[0, rows, cols].astype(F32)
        u = y * jax.nn.sigmoid(ga) * (za * jax.nn.sigmoid(za))
        u_ref[0, rows, cols] = u.astype(BF16)

    for h0 in range(0, HGRN_HEADS, HEADS_PER_LOOP):
        def chunk(c, carry, h0=h0):
            rows = pl.ds(pl.multiple_of(c * CHUNK, CHUNK), CHUNK)
            stages = [head_chunk(h, rows) for h in range(h0, h0 + HEADS_PER_LOOP)]
            splits = [next(s) for s in stages]
            sums = [jnp.dot(mz_ref[...], jnp.concatenate(splits[i:i + 2], axis=1), preferred_element_type=F32)
                    for i in range(0, HEADS_PER_LOOP, 2)]
            for i, s in enumerate(stages):
                s.send(sums[i // 2][:, (i % 2) * HGRN_DK:(i % 2 + 1) * HGRN_DK])
            while stages:
                stages = [s for s in stages if next(s, "done") != "done"]
            return carry

        lax.fori_loop(0, n_chunks, chunk, 0)

    @pl.when(t == pl.num_programs(1) - 1)
    def _():
        for h in range(HGRN_HEADS):
            sout_ref[0, h] = st_ref[h].T


def _decay_sum_matrix():
    m = np.zeros((1 + LEVELS, CHUNK, CHUNK), np.float32)
    idx = np.arange(CHUNK)
    m[0] = idx[:, None] >= idx[None, :]
    for i, l in enumerate(range(LEVELS - 1, -1, -1)):
        half = 1 << l
        for r in range(CHUNK):
            pos = r % (2 * half)
            first_upper = r - pos + half
            if pos >= half:
                m[1 + i, r, first_upper:r + 1] = 1.0
            else:
                m[1 + i, r, r + 1:first_upper] = 1.0
    m = m.reshape(-1, CHUNK)
    return jnp.asarray(np.concatenate([m, m, m], axis=1), dtype=BF16)


def _hgrn(pf, pb, lb, ng, s0, *, tt):
    bsz, t, _ = pb.shape

    def col(c0):
        return pl.BlockSpec((1, tt, D_A), lambda b, j, c0=c0: (b, j, c0 // D_A))

    vec = pl.BlockSpec((1, D_A), lambda b, j: (0, 0))
    state = pl.BlockSpec((1, HGRN_HEADS, HGRN_DK, HGRN_DV), lambda b, j: (b, 0, 0, 0))
    mz = _decay_sum_matrix()
    return pl.pallas_call(
        functools.partial(_hgrn_kernel, n_chunks=tt // CHUNK),
        grid=(bsz, t // tt),
        in_specs=[col(COL_QA), col(0), col(COL_IA), col(COL_GA), col(COL_ZA), vec, vec, state,
                  pl.BlockSpec(mz.shape, lambda b, j: (0, 0))],
        out_specs=[pl.BlockSpec((1, tt, D_A), lambda b, j: (b, j, 0)), state],
        out_shape=[jax.ShapeDtypeStruct((bsz, t, D_A), BF16),
                   jax.ShapeDtypeStruct((bsz, HGRN_HEADS, HGRN_DK, HGRN_DV), F32)],
        scratch_shapes=[pltpu.VMEM((HGRN_HEADS, HGRN_DV, HGRN_DK), F32)],
        compiler_params=_cparams(("arbitrary", "arbitrary")),
        name="hgrn2",
    )(pb, pf, pb, pb, pb, lb, ng, s0, mz)


def _rope(x, cos, sin_lo, sin_hi):
    half = ROT_DIM // 2
    return x * cos + pltpu.roll(x, LANES - half, 1) * sin_lo + pltpu.roll(x, half, 1) * sin_hi


def _attn_kernel(sink_ref, q_ref, kv_ref, z_ref, cos_ref, slo_ref, shi_ref, ck_ref, cv_ref,
                 u_ref, ko_ref, vo_ref, kbuf, vbuf, *, tt, masked, outw):
    t = pl.program_id(1)
    last = t == pl.num_programs(1) - 1
    n_chunks = tt // CHUNK
    n_slabs = KV_W // LANES

    @pl.when(t == 0)
    def _():
        kbuf[0:WINDOW, :] = ck_ref[0].astype(BF16)
        vbuf[0:WINDOW, :] = cv_ref[0].astype(BF16)

    if tt >= WINDOW:
        @pl.when(t > 0)
        def _():
            kbuf[0:WINDOW, :] = kbuf[tt:tt + WINDOW, :]
            vbuf[0:WINDOW, :] = vbuf[tt:tt + WINDOW, :]

    for j in range(n_slabs):
        lanes = slice(j * LANES, (j + 1) * LANES)
        kr = _rope(kv_ref[0, :, lanes].astype(F32), cos_ref[...], slo_ref[...], shi_ref[...])
        kbuf[WINDOW:WINDOW + tt, lanes] = kr.astype(BF16)

        @pl.when(last)
        def _(kr=kr, lanes=lanes):
            ko_ref[0, :, lanes] = kr[tt - outw:, :]
    vbuf[WINDOW:WINDOW + tt, :] = kv_ref[0, :, KV_W:2 * KV_W]

    @pl.when(last)
    def _():
        vo_ref[0] = kv_ref[0, tt - outw:tt, KV_W:2 * KV_W].astype(F32)

    lane = lax.broadcasted_iota(jnp.int32, (CHUNK, LANES), 1)
    low = lane < HEAD_DIM
    heads_per_slab = LANES // HEAD_DIM * GROUP
    sink_rows = [jnp.concatenate([jnp.full((1, CHUNK), sink_ref[heads_per_slab * j + i], F32)
                                  for i in range(heads_per_slab)], axis=1)
                 for j in range(n_slabs)]
    nt_dims = (((1,), (1,)), ((), ()))
    tn_dims = (((0,), (0,)), ((), ()))

    def kv_in_low_half(p):
        return (2 * p) // GROUP == 0

    def chunk(c, carry):
        r0 = pl.multiple_of(c * CHUNK, CHUNK)
        rows = pl.ds(r0, CHUNK)
        band = pl.ds(r0, BAND)
        cq, sl, sh = cos_ref[rows, :], slo_ref[rows, :], shi_ref[rows, :]
        seq_chunk = t * n_chunks + c

        def slab_chunk(j):
            lanes = slice(j * LANES, (j + 1) * LANES)
            kj = kbuf[band, lanes]
            vj = vbuf[band, lanes]
            qs = []
            for p in range(GROUP):
                slab = GROUP * j + p
                x = _rope(q_ref[0, rows, slab * LANES:(slab + 1) * LANES].astype(F32), cq, sl, sh) * ATTN_SCALE
                xr = pltpu.roll(x, HEAD_DIM, 1)
                if kv_in_low_half(p):
                    qs += [jnp.where(low, x, 0.0).astype(BF16), jnp.where(low, xr, 0.0).astype(BF16)]
                else:
                    qs += [jnp.where(low, 0.0, xr).astype(BF16), jnp.where(low, 0.0, x).astype(BF16)]
            qst = jnp.concatenate(qs, axis=0)
            yield
            s = lax.dot_general(kj, qst, nt_dims, preferred_element_type=F32)
            yield
            if masked:
                blocks = [jnp.where(seq_chunk >= N_LOOKBACK_CHUNKS - i, s[i * CHUNK:(i + 1) * CHUNK], NEG)
                          for i in range(N_LOOKBACK_CHUNKS)]
                s = jnp.concatenate(blocks + [s[WINDOW:]], axis=0)
            sink = sink_rows[j]
            m = jnp.maximum(jnp.max(s, axis=0, keepdims=True), sink)
            pr = jnp.exp(s - m)
            den = jnp.sum(pr, axis=0, keepdims=True) + jnp.exp(sink - m)
            pr16 = pr.astype(BF16)
            yield
            o = lax.dot_general(vj, pr16, tn_dims, preferred_element_type=F32)
            yield
            o = o * (1.0 / den)
            for p in range(GROUP):
                slab = GROUP * j + p
                ot = o[:, p * LANES:(p + 1) * LANES].T
                if kv_in_low_half(p):
                    ob = jnp.where(low, ot[:CHUNK], pltpu.roll(ot[CHUNK:], HEAD_DIM, 1))
                else:
                    ob = jnp.where(low, pltpu.roll(ot[:CHUNK], HEAD_DIM, 1), ot[CHUNK:])
                zb = z_ref[0, rows, slab * LANES:(slab + 1) * LANES].astype(F32)
                u_ref[0, rows, slab * LANES:(slab + 1) * LANES] = (
                    ob * (zb * jax.nn.sigmoid(zb))).astype(BF16)

        stages = [slab_chunk(j) for j in range(n_slabs)]
        while stages:
            stages = [s for s in stages if next(s, "done") != "done"]
        return carry

    lax.fori_loop(0, n_chunks, chunk, 0)


def _rope_tables(pos):
    half = ROT_DIM // 2
    inv = (ROPE_THETA ** (-np.arange(0, ROT_DIM, 2) / ROT_DIM)).astype(np.float32)
    ang = pos.astype(F32)[:, None] * inv[None, :]
    cos, sin = jnp.cos(ang), jnp.sin(ang)
    n = pos.shape[0]
    rest = HEAD_DIM - ROT_DIM
    cos_h = jnp.concatenate([cos, cos, jnp.ones((n, rest), F32)], axis=1)
    slo_h = jnp.concatenate([-sin, jnp.zeros((n, half + rest), F32)], axis=1)
    shi_h = jnp.concatenate([jnp.zeros((n, half), F32), sin, jnp.zeros((n, rest), F32)], axis=1)
    reps = LANES // HEAD_DIM
    return tuple(jnp.tile(a, (1, reps)) for a in (cos_h, slo_h, shi_h))


def _attention(pb, sinks, tables, cache_k, cache_v, *, tt, masked, outw):
    bsz, t, _ = pb.shape
    tab = pl.BlockSpec((tt, LANES), lambda b, j, s: (j, 0))
    cache = pl.BlockSpec((1, WINDOW, KV_W), lambda b, j, s: (b, 0, 0))
    win = pl.BlockSpec((1, outw, KV_W), lambda b, j, s: (b, 0, 0))
    grid_spec = pltpu.PrefetchScalarGridSpec(
        num_scalar_prefetch=1,
        grid=(bsz, t // tt),
        in_specs=[pl.BlockSpec((1, tt, D_B), lambda b, j, s: (b, j, COL_QB // D_B)),
                  pl.BlockSpec((1, tt, 2 * KV_W), lambda b, j, s: (b, j, COL_KV // (2 * KV_W))),
                  pl.BlockSpec((1, tt, D_B), lambda b, j, s: (b, j, COL_ZB // D_B)),
                  tab, tab, tab, cache, cache],
        out_specs=[pl.BlockSpec((1, tt, D_B), lambda b, j, s: (b, j, 0)), win, win],
        scratch_shapes=[pltpu.VMEM((WINDOW + tt, KV_W), BF16), pltpu.VMEM((WINDOW + tt, KV_W), BF16)],
    )
    return pl.pallas_call(
        functools.partial(_attn_kernel, tt=tt, masked=masked, outw=outw),
        grid_spec=grid_spec,
        out_shape=[jax.ShapeDtypeStruct((bsz, t, D_B), BF16),
                   jax.ShapeDtypeStruct((bsz, outw, KV_W), F32),
                   jax.ShapeDtypeStruct((bsz, outw, KV_W), F32)],
        compiler_params=_cparams(("arbitrary", "arbitrary")),
        name="swa_attention",
    )(sinks, pb, pb, pb, *tables, cache_k, cache_v)


def _out_kernel(ua_ref, ub_ref, ma_ref, mb_ref, x_ref, gate_ref, wpa_ref, wpb_ref, wout_ref, fg_ref,
                y_ref, *, nb, tt, final):
    m = nb * tt
    pa = jnp.dot(ua_ref[...].reshape(m, D_A), wpa_ref[...], preferred_element_type=F32)
    pb = jnp.dot(ub_ref[...].reshape(m, D_B), wpb_ref[...], preferred_element_type=F32)
    ma = ma_ref[...].reshape(m, D_MODEL).astype(F32)
    mb = mb_ref[...].reshape(m, D_MODEL).astype(F32)
    merged = jax.nn.sigmoid(ma) * pa + jax.nn.sigmoid(mb) * pb
    o = jnp.dot(merged.astype(BF16), wout_ref[...], preferred_element_type=F32)
    y = x_ref[...] + gate_ref[...] * o.reshape(nb, tt, D_MODEL)
    if final:
        y = (y * lax.rsqrt(jnp.mean(y * y, axis=-1, keepdims=True) + EPS)) * fg_ref[...]
    y_ref[...] = y


def _output(ua, ub, pb, x, gate, wpa, wpb, wout, final_g, *, nb, tt, final):
    bsz, t, _ = x.shape
    tok = lambda i, j: (i, j, 0)
    const = lambda i, j: (0, 0)
    single = pl.Buffered(1)
    return pl.pallas_call(
        functools.partial(_out_kernel, nb=nb, tt=tt, final=final),
        grid=(bsz // nb, t // tt),
        in_specs=[pl.BlockSpec((nb, tt, D_A), tok),
                  pl.BlockSpec((nb, tt, D_B), tok),
                  pl.BlockSpec((nb, tt, D_MODEL), lambda i, j: (i, j, COL_MA // D_MODEL)),
                  pl.BlockSpec((nb, tt, D_MODEL), lambda i, j: (i, j, COL_MB // D_MODEL)),
                  pl.BlockSpec((nb, tt, D_MODEL), tok),
                  pl.BlockSpec((nb, 1, D_MODEL), lambda i, j: (i, 0, 0)),
                  pl.BlockSpec((D_A, D_MODEL), const, pipeline_mode=single),
                  pl.BlockSpec((D_B, D_MODEL), const, pipeline_mode=single),
                  pl.BlockSpec((D_MODEL, D_MODEL), const, pipeline_mode=single),
                  pl.BlockSpec((1, D_MODEL), const)],
        out_specs=pl.BlockSpec((nb, tt, D_MODEL), tok),
        out_shape=jax.ShapeDtypeStruct((bsz, t, D_MODEL), F32),
        compiler_params=_cparams(("arbitrary", "arbitrary")),
        name="merge_output",
    )(ua, ub, pb, pb, x, gate, wpa, wpb, wout, final_g)


def _tiles(t):
    if t >= 1024:
        return dict(proj=(1, min(t, 2048)), out=(1, 256), hgrn_tt=512, attn_tt=512)
    return dict(proj=(16, t), out=(4, t), hgrn_tt=t, attn_tt=t)


def _layer(x, mod, w, lb, s0, cache_k, cache_v, tables, *, final, final_g):
    bsz, t, _ = x.shape
    tl = _tiles(t)
    shift, scale, gate = (mod[:, None, i * D_MODEL:(i + 1) * D_MODEL] for i in range(3))
    nb, tt = tl["proj"]
    pf, pb = _in_projection(x, shift, scale, w["norm_g"], w["w_in"], nb=min(nb, bsz), tt=tt)
    ua, s_new = _hgrn(pf, pb, lb, w["hgrn_g"], s0, tt=tl["hgrn_tt"])
    masked = cache_k is None
    if masked:
        cache_k = jnp.zeros((bsz, WINDOW, KV_W), F32)
        cache_v = cache_k
    outw = min(t, WINDOW)
    ub, k_new, v_new = _attention(pb, w["sinks"], tables, cache_k, cache_v,
                                  tt=tl["attn_tt"], masked=masked, outw=outw)
    nb, tt = tl["out"]
    y = _output(ua, ub, pb, x, gate, w["w_pa"], w["w_pb"], w["w_out"], final_g,
                nb=min(nb, bsz), tt=tt, final=final)
    shape = (bsz, outw, N_KV, HEAD_DIM)
    return y, s_new, k_new.reshape(shape), v_new.reshape(shape)


def kernel(x_prompt, x_sample, c_prompt, c_sample, state_hgrn, cache_win_k, cache_win_v, ada_w, ada_b,
           norm_g, w_in, lb_logits, hgrn_norm_g, sinks, w_branch_a, w_branch_b, w_out, final_norm_g):
    bp, tp = x_prompt.shape[0], x_prompt.shape[1]
    bs, ts = x_sample.shape[0], x_sample.shape[1]

    w_in_p = w_in.astype(BF16)
    w_pa = w_branch_a.astype(BF16)
    w_pb = w_branch_b.astype(BF16)
    w_o = w_out.astype(BF16)
    prob = jax.nn.softmax(lb_logits.astype(F32), axis=0)
    lb = jnp.cumsum(prob, axis=0) - prob[:1]
    final_g = final_norm_g.reshape(1, D_MODEL)

    mod = _modulation(jnp.concatenate([c_prompt, c_sample], axis=0), ada_w, ada_b)
    tab_p = _rope_tables(jnp.arange(tp))
    tab_s = _rope_tables(PAST_LEN + jnp.arange(ts))
    zero_state = jnp.zeros((bp, HGRN_HEADS, HGRN_DK, HGRN_DV), F32)

    hp, hs = x_prompt, x_sample
    outs = [[] for _ in range(6)]
    for l in range(DEPTH):
        w = dict(norm_g=norm_g[l].reshape(1, D_MODEL), w_in=w_in_p[l], hgrn_g=hgrn_norm_g[l].reshape(1, D_A),
                 sinks=sinks[l], w_pa=w_pa[l], w_pb=w_pb[l], w_out=w_o[l])
        lbl = lb[l].reshape(1, D_A)
        final = l == DEPTH - 1
        hp, s_p, k_p, v_p = _layer(hp, mod[l, :bp], w, lbl, zero_state, None, None, tab_p,
                                   final=final, final_g=final_g)
        hs, s_s, k_s, v_s = _layer(hs, mod[l, bp:], w, lbl, state_hgrn[l],
                                   cache_win_k[l].reshape(bs, WINDOW, KV_W),
                                   cache_win_v[l].reshape(bs, WINDOW, KV_W), tab_s,
                                   final=final, final_g=final_g)
        for acc, val in zip(outs, (s_p, k_p, v_p, s_s, k_s, v_s)):
            acc.append(val)
    return (hp, hs) + tuple(jnp.stack(o) for o in outs)
```

```python
import functools

import jax
import jax.numpy as jnp
import numpy as np
from jax import lax
from jax.experimental import pallas as pl
from jax.experimental.pallas import tpu as pltpu

F32 = jnp.float32
BF16 = jnp.bfloat16

D_MODEL = 2048
DEPTH = 4
PAST_LEN = 4096
CHUNK = 64
SUB = 16
NSUB = CHUNK // SUB
D_A = D_MODEL // 2
HGRN_DK = 128
HGRN_HEADS = D_A // HGRN_DK
HGRN_DV = D_A // HGRN_HEADS
D_B = D_MODEL // 2
HEAD_DIM = 64
N_Q = D_B // HEAD_DIM
N_KV = N_Q // 4
GROUP = N_Q // N_KV
WINDOW = 128
N_LOOKBACK_CHUNKS = WINDOW // CHUNK
BAND = WINDOW + CHUNK
ROT_DIM = HEAD_DIM // 4
ROPE_THETA = 500000.0
ATTN_SCALE = HEAD_DIM ** -0.5
EPS = 1e-6
NEG = -1e30
LOG2E = 1.4426950408889634
N_IN = 5 * D_A + N_Q * HEAD_DIM + 2 * N_KV * HEAD_DIM + D_B + 2 * D_MODEL

LANES = 128
SUBLANES = 8
KV_W = N_KV * HEAD_DIM
VMEM_LIMIT = 56 * 1024 * 1024

N_F32 = D_A
N_B16 = N_IN - N_F32
COL_MA, COL_MB = 0, D_MODEL
COL_QA, COL_IA, COL_GA, COL_ZA = (2 * D_MODEL + i * D_A for i in range(4))
COL_QB = COL_ZA + D_A
COL_ZB = COL_QB + D_B
COL_KV = COL_ZB + D_B
PROJ_TN = 512


def _proj_tile_order():
    sizes = dict(qa=D_A, fa=D_A, ia=D_A, ga=D_A, za=D_A, qb=D_B, kv=2 * KV_W, zb=D_B, ma=D_MODEL, mb=D_MODEL)
    start, first = 0, {}
    for name, n in sizes.items():
        first[name] = start // PROJ_TN
        start += n
    order = []
    for name in ("fa", "ma", "mb", "qa", "ia", "ga", "za", "qb", "zb", "kv"):
        order += range(first[name], first[name] + sizes[name] // PROJ_TN)
    return np.asarray(order, np.int32)


def _cparams(sem):
    return pltpu.CompilerParams(dimension_semantics=sem, vmem_limit_bytes=VMEM_LIMIT)


def _mod_kernel(c_ref, w_ref, b_ref, o_ref):
    acc = jnp.dot(c_ref[...].astype(BF16), w_ref[0].astype(BF16), preferred_element_type=F32)
    o_ref[0] = acc + b_ref[0]


def _modulation(c_all, ada_w, ada_b):
    n, d3, tn = c_all.shape[0], 3 * D_MODEL, 768
    return pl.pallas_call(
        _mod_kernel,
        grid=(DEPTH, d3 // tn),
        in_specs=[pl.BlockSpec((n, D_MODEL), lambda l, j: (0, 0)),
                  pl.BlockSpec((1, D_MODEL, tn), lambda l, j: (l, 0, j)),
                  pl.BlockSpec((1, 1, tn), lambda l, j: (l, 0, j))],
        out_specs=pl.BlockSpec((1, n, tn), lambda l, j: (l, 0, j)),
        out_shape=jax.ShapeDtypeStruct((DEPTH, n, d3), F32),
        compiler_params=_cparams(("arbitrary", "arbitrary")),
        name="adaln_mod",
    )(c_all, ada_w, ada_b.reshape(DEPTH, 1, d3))


def _inproj_kernel(src_ref, x_ref, shift_ref, scale_ref, g_ref, w_ref, of_ref, ob_ref, h_ref, *,
                   nb, tt, rc, nf):
    del src_ref
    n = pl.program_id(2)

    @pl.when(n == 0)
    def _():
        g = g_ref[...]
        for b in range(nb):
            mul = 1.0 + scale_ref[b]
            add = shift_ref[b]

            def rows(r, carry, b=b, mul=mul, add=add):
                r0 = pl.multiple_of(r * rc, rc)
                x = x_ref[b, pl.ds(r0, rc), :]
                y = x * lax.rsqrt(jnp.mean(x * x, axis=-1, keepdims=True) + EPS)
                h = (y * g) * mul + add
                h_ref[pl.ds(b * tt + r0, rc), :] = h.astype(BF16)
                return carry

            lax.fori_loop(0, tt // rc, rows, 0)

    acc = jnp.dot(h_ref[...], w_ref[...], preferred_element_type=F32)

    @pl.when(n < nf)
    def _():
        of_ref[...] = acc.reshape(of_ref.shape)

    @pl.when(n >= nf)
    def _():
        ob_ref[...] = acc.astype(BF16).reshape(ob_ref.shape)


def _in_projection(x, shift, scale, norm_g, w, *, layer, nb, tt):
    bsz, t, _ = x.shape
    tn = PROJ_TN
    rc = min(tt, 128)
    nf = N_F32 // tn
    kern = functools.partial(_inproj_kernel, nb=nb, tt=tt, rc=rc, nf=nf)
    grid_spec = pltpu.PrefetchScalarGridSpec(
        num_scalar_prefetch=1,
        grid=(bsz // nb, t // tt, N_IN // tn),
        in_specs=[pl.BlockSpec((nb, tt, D_MODEL), lambda i, j, n, src: (i, j, 0),
                               pipeline_mode=pl.Buffered(1)),
                  pl.BlockSpec((nb, 1, D_MODEL), lambda i, j, n, src: (i, 0, 0)),
                  pl.BlockSpec((nb, 1, D_MODEL), lambda i, j, n, src: (i, 0, 0)),
                  pl.BlockSpec((1, D_MODEL), lambda i, j, n, src: (0, 0)),
                  pl.BlockSpec((None, D_MODEL, tn), lambda i, j, n, src: (layer, 0, src[n]))],
        out_specs=[pl.BlockSpec((nb, tt, tn), lambda i, j, n, src: (i, j, jnp.minimum(n, nf - 1))),
                   pl.BlockSpec((nb, tt, tn), lambda i, j, n, src: (i, j, jnp.maximum(n - nf, 0)))],
        scratch_shapes=[pltpu.VMEM((nb * tt, D_MODEL), BF16)],
    )
    return pl.pallas_call(
        kern,
        grid_spec=grid_spec,
        out_shape=[jax.ShapeDtypeStruct((bsz, t, N_F32), F32),
                   jax.ShapeDtypeStruct((bsz, t, N_B16), BF16)],
        compiler_params=_cparams(("arbitrary", "arbitrary", "arbitrary")),
        name="in_projection",
    )(jnp.asarray(_proj_tile_order()), x, shift, scale, norm_g, w)


HEADS_PER_LOOP = 8
LEVELS = CHUNK.bit_length() - 1
MXU_LEVELS = 2


def _hgrn_kernel(q_ref, f_ref, i_ref, g_ref, z_ref, lb_ref, ng_ref, s0_ref, mz_ref,
                 u_ref, sout_ref, st_ref, *, n_chunks):
    t = pl.program_id(1)

    @pl.when(t == 0)
    def _():
        for h in range(HGRN_HEADS):
            st_ref[h] = s0_ref[0, h].T

    row = lax.broadcasted_iota(jnp.int32, (CHUNK, CHUNK), 0)
    col = lax.broadcasted_iota(jnp.int32, (CHUNK, CHUNK), 1)
    code = jnp.where(row > col, row ^ col, jnp.where(row == col, 0, 2 * CHUNK))
    small_masks = [jnp.right_shift(code, l) == 1 for l in range(LEVELS - MXU_LEVELS)]
    diag_mask = code == 0
    sub_row = lax.broadcasted_iota(jnp.int32, (SUBLANES, HGRN_DK), 0)
    upper_half = [(jnp.right_shift(sub_row, l) & 1) == 1 for l in range(SUBLANES.bit_length() - 1)]
    nt_dims = (((1,), (1,)), ((), ()))
    tn_dims = (((0,), (0,)), ((), ()))

    def padded(x, r0):
        parts = []
        if r0:
            parts.append(jnp.zeros((r0, HGRN_DK), F32))
        parts.append(x)
        if CHUNK - r0 - x.shape[0]:
            parts.append(jnp.zeros((CHUNK - r0 - x.shape[0], HGRN_DK), F32))
        return jnp.concatenate(parts, axis=0).astype(BF16)

    def head_chunk(h, rows):
        cols = slice(h * HGRN_DK, (h + 1) * HGRN_DK)
        lb = lb_ref[:, cols]
        a = f_ref[0, rows, cols]
        qa = q_ref[0, rows, cols].astype(F32)
        v = i_ref[0, rows, cols]
        a2 = a * LOG2E
        e = jnp.exp2(-a2)
        l1e = jnp.log(1.0 + e) * LOG2E
        log_f = jnp.log(1.0 + lb * e) * LOG2E - l1e
        k = jnp.exp2((jnp.log(1.0 - lb) * LOG2E - a2) - l1e)
        q = qa * jax.nn.sigmoid(qa)

        hi = log_f.astype(BF16)
        r1 = log_f - hi.astype(F32)
        mid = r1.astype(BF16)
        lo = (r1 - mid.astype(F32)).astype(BF16)
        zall = yield jnp.concatenate([hi, mid, lo], axis=0)
        b = zall[:CHUNK]

        def level_sums(l):
            i = LEVELS - l
            return zall[i * CHUNK:(i + 1) * CHUNK]

        st = st_ref[h]
        q_dec = (q * jnp.exp2(b)).astype(BF16)
        st16 = st.astype(BF16)
        x_cols, y_cols = [], []
        for l in range(LEVELS - 1, LEVELS - 1 - MXU_LEVELS, -1):
            half = 1 << l
            w = jnp.exp2(level_sums(l))
            for base in range(0, CHUNK, 2 * half):
                lo_rows = slice(base, base + half)
                hi_rows = slice(base + half, base + 2 * half)
                x_cols.append(padded(q[hi_rows] * w[hi_rows], base + half))
                y_cols.append(padded(k[lo_rows] * w[lo_rows], base))
        x_big, y_big = jnp.concatenate(x_cols, axis=1), jnp.concatenate(y_cols, axis=1)
        pairs = [(q.astype(BF16), k.astype(BF16))]
        for l in range(LEVELS - MXU_LEVELS):
            w = jnp.exp2(level_sums(l))
            if (1 << l) < SUBLANES:
                side = jnp.concatenate([jnp.where(upper_half[l], q[r:r + SUBLANES], k[r:r + SUBLANES])
                                        for r in range(0, CHUNK, SUBLANES)], axis=0)
            else:
                side = jnp.concatenate([q[r:r + SUBLANES] if (r >> l) & 1 else k[r:r + SUBLANES]
                                        for r in range(0, CHUNK, SUBLANES)], axis=0)
            tl = (w * side).astype(BF16)
            pairs.append((tl, tl))
        b_end = b[CHUNK - 1:CHUNK]
        k_dec = (k * jnp.exp2(b_end - b)).astype(BF16)
        st_new = st * jnp.exp2(b_end)
        yield
        o_inter = lax.dot_general(q_dec, st16, nt_dims, preferred_element_type=F32)
        a_big = lax.dot_general(x_big, y_big, nt_dims, preferred_element_type=F32)
        a_small = [lax.dot_general(x, y, nt_dims, preferred_element_type=F32) for x, y in pairs]
        st_add = lax.dot_general(v, k_dec, tn_dims, preferred_element_type=F32)
        yield
        small = jnp.where(diag_mask, a_small[0], 0.0)
        for l in range(LEVELS - MXU_LEVELS):
            small = jnp.where(small_masks[l], a_small[1 + l], small)
        a_mat = (a_big + small).astype(BF16)
        st_ref[h] = st_new + st_add
        yield
        o_intra = jnp.dot(a_mat, v, preferred_element_type=F32)
        yield
        o = o_inter + o_intra
        y = o * lax.rsqrt(jnp.mean(o * o, axis=-1, keepdims=True) + EPS) * ng_ref[:, cols]
        ga = g_ref[0, rows, cols].astype(F32)
        za = z_ref[0, rows, cols].astype(F32)
        u = y * jax.nn.sigmoid(ga) * (za * jax.nn.sigmoid(za))
        u_ref[0, rows, cols] = u.astype(BF16)

    for h0 in range(0, HGRN_HEADS, HEADS_PER_LOOP):
        def chunk(c, carry, h0=h0):
            rows = pl.ds(pl.multiple_of(c * CHUNK, CHUNK), CHUNK)
            stages = [head_chunk(h, rows) for h in range(h0, h0 + HEADS_PER_LOOP)]
            splits = [next(s) for s in stages]
            sums = [jnp.dot(mz_ref[...], jnp.concatenate(splits[i:i + 2], axis=1), preferred_element_type=F32)
                    for i in range(0, HEADS_PER_LOOP, 2)]
            for i, s in enumerate(stages):
                s.send(sums[i // 2][:, (i % 2) * HGRN_DK:(i % 2 + 1) * HGRN_DK])
            while stages:
                stages = [s for s in stages if next(s, "done") != "done"]
            return carry

        lax.fori_loop(0, n_chunks, chunk, 0)

    @pl.when(t == pl.num_programs(1) - 1)
    def _():
        for h in range(HGRN_HEADS):
            sout_ref[0, h] = st_ref[h].T


def _decay_sum_matrix():
    m = np.zeros((1 + LEVELS, CHUNK, CHUNK), np.float32)
    idx = np.arange(CHUNK)
    m[0] = idx[:, None] >= idx[None, :]
    for i, l in enumerate(range(LEVELS - 1, -1, -1)):
        half = 1 << l
        for r in range(CHUNK):
            pos = r % (2 * half)
            first_upper = r - pos + half
            if pos >= half:
                m[1 + i, r, first_upper:r + 1] = 1.0
            else:
                m[1 + i, r, r + 1:first_upper] = 1.0
    m = m.reshape(-1, CHUNK)
    return jnp.asarray(np.concatenate([m, m, m], axis=1), dtype=BF16)


def _hgrn(pf, pb, lb, ng, s0, *, tt):
    bsz, t, _ = pb.shape

    def col(c0):
        return pl.BlockSpec((1, tt, D_A), lambda b, j, c0=c0: (b, j, c0 // D_A))

    vec = pl.BlockSpec((1, D_A), lambda b, j: (0, 0))
    state = pl.BlockSpec((1, HGRN_HEADS, HGRN_DK, HGRN_DV), lambda b, j: (b, 0, 0, 0))
    mz = _decay_sum_matrix()
    return pl.pallas_call(
        functools.partial(_hgrn_kernel, n_chunks=tt // CHUNK),
        grid=(bsz, t // tt),
        in_specs=[col(COL_QA), col(0), col(COL_IA), col(COL_GA), col(COL_ZA), vec, vec, state,
                  pl.BlockSpec(mz.shape, lambda b, j: (0, 0))],
        out_specs=[pl.BlockSpec((1, tt, D_A), lambda b, j: (b, j, 0)), state],
        out_shape=[jax.ShapeDtypeStruct((bsz, t, D_A), BF16),
                   jax.ShapeDtypeStruct((bsz, HGRN_HEADS, HGRN_DK, HGRN_DV), F32)],
        scratch_shapes=[pltpu.VMEM((HGRN_HEADS, HGRN_DV, HGRN_DK), F32)],
        compiler_params=_cparams(("arbitrary", "arbitrary")),
        name="hgrn2",
    )(pb, pf, pb, pb, pb, lb, ng, s0, mz)


def _rope(x, cos, sin_lo, sin_hi):
    half = ROT_DIM // 2
    return x * cos + pltpu.roll(x, LANES - half, 1) * sin_lo + pltpu.roll(x, half, 1) * sin_hi


def _attn_kernel(sink_ref, q_ref, kv_ref, z_ref, cos_ref, slo_ref, shi_ref, ck_ref, cv_ref,
                 u_ref, ko_ref, vo_ref, kbuf, vbuf, *, tt, masked, outw):
    t = pl.program_id(1)
    last = t == pl.num_programs(1) - 1
    n_chunks = tt // CHUNK
    n_slabs = KV_W // LANES

    def store_band(buf, r0, x, lanes):
        buf[r0:r0 + x.shape[0], lanes] = x.astype(BF16)
        buf[r0:r0 + x.shape[0], slice(KV_W + lanes.start, KV_W + lanes.stop)] = (
            pltpu.roll(x, HEAD_DIM, 1).astype(BF16))

    @pl.when(t == 0)
    def _():
        for j in range(n_slabs):
            lanes = slice(j * LANES, (j + 1) * LANES)
            store_band(kbuf, 0, ck_ref[0, :, lanes], lanes)
            store_band(vbuf, 0, cv_ref[0, :, lanes], lanes)

    if tt >= WINDOW:
        @pl.when(t > 0)
        def _():
            kbuf[0:WINDOW, :] = kbuf[tt:tt + WINDOW, :]
            vbuf[0:WINDOW, :] = vbuf[tt:tt + WINDOW, :]

    for j in range(n_slabs):
        lanes = slice(j * LANES, (j + 1) * LANES)
        kr = _rope(kv_ref[0, :, lanes].astype(F32), cos_ref[...], slo_ref[...], shi_ref[...])
        store_band(kbuf, WINDOW, kr, lanes)
        store_band(vbuf, WINDOW, kv_ref[0, :, slice(KV_W + lanes.start, KV_W + lanes.stop)].astype(F32), lanes)

        @pl.when(last)
        def _(kr=kr, lanes=lanes):
            ko_ref[0, :, lanes] = kr[tt - outw:, :]

    @pl.when(last)
    def _():
        vo_ref[0] = kv_ref[0, tt - outw:tt, KV_W:2 * KV_W].astype(F32)

    lane = lax.broadcasted_iota(jnp.int32, (CHUNK, LANES), 1)
    low = lane < HEAD_DIM
    heads_per_slab = LANES // HEAD_DIM * GROUP

    def kv_in_low_half(p):
        return (2 * p) // GROUP == 0

    def matched_head(p):
        return 0 if kv_in_low_half(p) else 1

    sink_rows = []
    for j in range(n_slabs):
        order = ([heads_per_slab * j + 2 * p + matched_head(p) for p in range(GROUP)]
                 + [heads_per_slab * j + 2 * p + 1 - matched_head(p) for p in range(GROUP)])
        sink_rows.append(jnp.concatenate([jnp.full((1, CHUNK), sink_ref[h], F32) for h in order], axis=1))
    nt_dims = (((1,), (1,)), ((), ()))
    tn_dims = (((0,), (0,)), ((), ()))
    half_cols = GROUP * CHUNK

    def chunk(c, carry):
        r0 = pl.multiple_of(c * CHUNK, CHUNK)
        rows = pl.ds(r0, CHUNK)
        band = pl.ds(r0, BAND)
        cq, sl, sh = cos_ref[rows, :], slo_ref[rows, :], shi_ref[rows, :]
        seq_chunk = t * n_chunks + c

        def slab_chunk(j):
            plain = slice(j * LANES, (j + 1) * LANES)
            swapped = slice(KV_W + j * LANES, KV_W + (j + 1) * LANES)
            k_plain, k_swap = kbuf[band, plain], kbuf[band, swapped]
            v_plain, v_swap = vbuf[band, plain], vbuf[band, swapped]
            q_plain, q_swap = [], []
            for p in range(GROUP):
                slab = GROUP * j + p
                x = _rope(q_ref[0, rows, slab * LANES:(slab + 1) * LANES].astype(F32), cq, sl, sh) * ATTN_SCALE
                halves = [jnp.where(low, x, 0.0).astype(BF16), jnp.where(low, 0.0, x).astype(BF16)]
                q_plain.append(halves[matched_head(p)])
                q_swap.append(halves[1 - matched_head(p)])
            q_plain = jnp.concatenate(q_plain, axis=0)
            q_swap = jnp.concatenate(q_swap, axis=0)
            yield
            s = jnp.concatenate([lax.dot_general(k_plain, q_plain, nt_dims, preferred_element_type=F32),
                                 lax.dot_general(k_swap, q_swap, nt_dims, preferred_element_type=F32)],
                                axis=1)
            yield
            if masked:
                blocks = [jnp.where(seq_chunk >= N_LOOKBACK_CHUNKS - i, s[i * CHUNK:(i + 1) * CHUNK], NEG)
                          for i in range(N_LOOKBACK_CHUNKS)]
                s = jnp.concatenate(blocks + [s[WINDOW:]], axis=0)
            sink = sink_rows[j]
            m = jnp.maximum(jnp.max(s, axis=0, keepdims=True), sink)
            pr = jnp.exp(s - m)
            inv = 1.0 / (jnp.sum(pr, axis=0, keepdims=True) + jnp.exp(sink - m))
            pr16 = pr.astype(BF16)
            yield
            o_plain = lax.dot_general(v_plain, pr16[:, :half_cols], tn_dims, preferred_element_type=F32)
            o_swap = lax.dot_general(v_swap, pr16[:, half_cols:], tn_dims, preferred_element_type=F32)
            yield
            o_plain = o_plain * inv[:, :half_cols]
            o_swap = o_swap * inv[:, half_cols:]
            for pair in range(GROUP // 2):
                t_plain = o_plain[:, pair * LANES:(pair + 1) * LANES].T
                t_swap = o_swap[:, pair * LANES:(pair + 1) * LANES].T
                for i in range(2):
                    p = 2 * pair + i
                    slab = GROUP * j + p
                    qrows = slice(i * CHUNK, (i + 1) * CHUNK)
                    if matched_head(p) == 0:
                        ob = jnp.where(low, t_plain[qrows], t_swap[qrows])
                    else:
                        ob = jnp.where(low, t_swap[qrows], t_plain[qrows])
                    zb = z_ref[0, rows, slab * LANES:(slab + 1) * LANES].astype(F32)
                    u_ref[0, rows, slab * LANES:(slab + 1) * LANES] = (
                        ob * (zb * jax.nn.sigmoid(zb))).astype(BF16)

        stages = [slab_chunk(j) for j in range(n_slabs)]
        while stages:
            stages = [s for s in stages if next(s, "done") != "done"]
        return carry

    lax.fori_loop(0, n_chunks, chunk, 0)


def _rope_tables(pos):
    half = ROT_DIM // 2
    inv = (ROPE_THETA ** (-np.arange(0, ROT_DIM, 2) / ROT_DIM)).astype(np.float32)
    ang = pos.astype(F32)[:, None] * inv[None, :]
    cos, sin = jnp.cos(ang), jnp.sin(ang)
    n = pos.shape[0]
    rest = HEAD_DIM - ROT_DIM
    cos_h = jnp.concatenate([cos, cos, jnp.ones((n, rest), F32)], axis=1)
    slo_h = jnp.concatenate([-sin, jnp.zeros((n, half + rest), F32)], axis=1)
    shi_h = jnp.concatenate([jnp.zeros((n, half), F32), sin, jnp.zeros((n, rest), F32)], axis=1)
    reps = LANES // HEAD_DIM
    return tuple(jnp.tile(a, (1, reps)) for a in (cos_h, slo_h, shi_h))


def _attention(pb, sinks, tables, cache_k, cache_v, *, tt, masked, outw):
    bsz, t, _ = pb.shape
    tab = pl.BlockSpec((tt, LANES), lambda b, j, s: (j, 0))
    cache = pl.BlockSpec((1, WINDOW, KV_W), lambda b, j, s: (b, 0, 0))
    win = pl.BlockSpec((1, outw, KV_W), lambda b, j, s: (b, 0, 0))
    grid_spec = pltpu.PrefetchScalarGridSpec(
        num_scalar_prefetch=1,
        grid=(bsz, t // tt),
        in_specs=[pl.BlockSpec((1, tt, D_B), lambda b, j, s: (b, j, COL_QB // D_B)),
                  pl.BlockSpec((1, tt, 2 * KV_W), lambda b, j, s: (b, j, COL_KV // (2 * KV_W))),
                  pl.BlockSpec((1, tt, D_B), lambda b, j, s: (b, j, COL_ZB // D_B)),
                  tab, tab, tab, cache, cache],
        out_specs=[pl.BlockSpec((1, tt, D_B), lambda b, j, s: (b, j, 0)), win, win],
        scratch_shapes=[pltpu.VMEM((WINDOW + tt, 2 * KV_W), BF16), pltpu.VMEM((WINDOW + tt, 2 * KV_W), BF16)],
    )
    return pl.pallas_call(
        functools.partial(_attn_kernel, tt=tt, masked=masked, outw=outw),
        grid_spec=grid_spec,
        out_shape=[jax.ShapeDtypeStruct((bsz, t, D_B), BF16),
                   jax.ShapeDtypeStruct((bsz, outw, KV_W), F32),
                   jax.ShapeDtypeStruct((bsz, outw, KV_W), F32)],
        compiler_params=_cparams(("arbitrary", "arbitrary")),
        name="swa_attention",
    )(sinks, pb, pb, pb, *tables, cache_k, cache_v)


def _out_kernel(ua_ref, ub_ref, ma_ref, mb_ref, x_ref, gate_ref, wpa_ref, wpb_ref, wout_ref, fg_ref,
                y_ref, *, nb, tt, final):
    m = nb * tt
    pa = jnp.dot(ua_ref[...].reshape(m, D_A), wpa_ref[...], preferred_element_type=F32)
    pb = jnp.dot(ub_ref[...].reshape(m, D_B), wpb_ref[...], preferred_element_type=F32)
    ma = ma_ref[...].reshape(m, D_MODEL).astype(F32)
    mb = mb_ref[...].reshape(m, D_MODEL).astype(F32)
    merged = jax.nn.sigmoid(ma) * pa + jax.nn.sigmoid(mb) * pb
    o = jnp.dot(merged.astype(BF16), wout_ref[...], preferred_element_type=F32)
    y = x_ref[...] + gate_ref[...] * o.reshape(nb, tt, D_MODEL)
    if final:
        y = (y * lax.rsqrt(jnp.mean(y * y, axis=-1, keepdims=True) + EPS)) * fg_ref[...]
    y_ref[...] = y


def _output(ua, ub, pb, x, gate, wpa, wpb, wout, final_g, *, layer, nb, tt, final):
    bsz, t, _ = x.shape
    tok = lambda i, j: (i, j, 0)
    const = lambda i, j: (0, 0)
    of_layer = lambda i, j: (layer, 0, 0)
    single = pl.Buffered(1)
    return pl.pallas_call(
        functools.partial(_out_kernel, nb=nb, tt=tt, final=final),
        grid=(bsz // nb, t // tt),
        in_specs=[pl.BlockSpec((nb, tt, D_A), tok),
                  pl.BlockSpec((nb, tt, D_B), tok),
                  pl.BlockSpec((nb, tt, D_MODEL), lambda i, j: (i, j, COL_MA // D_MODEL)),
                  pl.BlockSpec((nb, tt, D_MODEL), lambda i, j: (i, j, COL_MB // D_MODEL)),
                  pl.BlockSpec((nb, tt, D_MODEL), tok),
                  pl.BlockSpec((nb, 1, D_MODEL), lambda i, j: (i, 0, 0)),
                  pl.BlockSpec((None, D_A, D_MODEL), of_layer, pipeline_mode=single),
                  pl.BlockSpec((None, D_B, D_MODEL), of_layer, pipeline_mode=single),
                  pl.BlockSpec((None, D_MODEL, D_MODEL), of_layer, pipeline_mode=single),
                  pl.BlockSpec((1, D_MODEL), const)],
        out_specs=pl.BlockSpec((nb, tt, D_MODEL), tok),
        out_shape=jax.ShapeDtypeStruct((bsz, t, D_MODEL), F32),
        compiler_params=_cparams(("arbitrary", "arbitrary")),
        name="merge_output",
    )(ua, ub, pb, pb, x, gate, wpa, wpb, wout, final_g)


def _tiles(t):
    if t >= 1024:
        return dict(proj=(1, min(t, 2048)), out=(1, 256), hgrn_tt=512, attn_tt=512)
    return dict(proj=(16, t), out=(4, t), hgrn_tt=t, attn_tt=t)


def _layer(x, mod, w, lb, s0, cache_k, cache_v, tables, *, final, final_g):
    bsz, t, _ = x.shape
    tl = _tiles(t)
    shift, scale, gate = (mod[:, None, i * D_MODEL:(i + 1) * D_MODEL] for i in range(3))
    nb, tt = tl["proj"]
    pf, pb = _in_projection(x, shift, scale, w["norm_g"], w["w_in"], layer=w["layer"], nb=min(nb, bsz), tt=tt)
    ua, s_new = _hgrn(pf, pb, lb, w["hgrn_g"], s0, tt=tl["hgrn_tt"])
    masked = cache_k is None
    if masked:
        cache_k = jnp.zeros((bsz, WINDOW, KV_W), F32)
        cache_v = cache_k
    outw = min(t, WINDOW)
    ub, k_new, v_new = _attention(pb, w["sinks"], tables, cache_k, cache_v,
                                  tt=tl["attn_tt"], masked=masked, outw=outw)
    nb, tt = tl["out"]
    y = _output(ua, ub, pb, x, gate, w["w_pa"], w["w_pb"], w["w_out"], final_g,
                layer=w["layer"], nb=min(nb, bsz), tt=tt, final=final)
    shape = (bsz, outw, N_KV, HEAD_DIM)
    return y, s_new, k_new.reshape(shape), v_new.reshape(shape)


def kernel(x_prompt, x_sample, c_prompt, c_sample, state_hgrn, cache_win_k, cache_win_v, ada_w, ada_b,
           norm_g, w_in, lb_logits, hgrn_norm_g, sinks, w_branch_a, w_branch_b, w_out, final_norm_g):
    bp, tp = x_prompt.shape[0], x_prompt.shape[1]
    bs, ts = x_sample.shape[0], x_sample.shape[1]

    w_in_p = w_in.astype(BF16)
    w_pa = w_branch_a.astype(BF16)
    w_pb = w_branch_b.astype(BF16)
    w_o = w_out.astype(BF16)
    prob = jax.nn.softmax(lb_logits.astype(F32), axis=0)
    lb = jnp.cumsum(prob, axis=0) - prob[:1]
    final_g = final_norm_g.reshape(1, D_MODEL)

    mod = _modulation(jnp.concatenate([c_prompt, c_sample], axis=0), ada_w, ada_b)
    tab_p = _rope_tables(jnp.arange(tp))
    tab_s = _rope_tables(PAST_LEN + jnp.arange(ts))
    zero_state = jnp.zeros((bp, HGRN_HEADS, HGRN_DK, HGRN_DV), F32)

    hp, hs = x_prompt, x_sample
    outs = [[] for _ in range(6)]
    for l in range(DEPTH):
        w = dict(norm_g=norm_g[l].reshape(1, D_MODEL), w_in=w_in_p, hgrn_g=hgrn_norm_g[l].reshape(1, D_A),
                 sinks=sinks[l], w_pa=w_pa, w_pb=w_pb, w_out=w_o, layer=l)
        lbl = lb[l].reshape(1, D_A)
        final = l == DEPTH - 1
        hp, s_p, k_p, v_p = _layer(hp, mod[l, :bp], w, lbl, zero_state, None, None, tab_p,
                                   final=final, final_g=final_g)
        hs, s_s, k_s, v_s = _layer(hs, mod[l, bp:], w, lbl, state_hgrn[l],
                                   cache_win_k[l].reshape(bs, WINDOW, KV_W),
                                   cache_win_v[l].reshape(bs, WINDOW, KV_W), tab_s,
                                   final=final, final_g=final_g)
        for acc, val in zip(outs, (s_p, k_p, v_p, s_s, k_s, v_s)):
            acc.append(val)
    return (hp, hs) + tuple(jnp.stack(o) for o in outs)
```

```python
import functools

import jax
import jax.numpy as jnp
import numpy as np
from jax import lax
from jax.experimental import pallas as pl
from jax.experimental.pallas import tpu as pltpu

F32 = jnp.float32
BF16 = jnp.bfloat16

D_MODEL = 2048
DEPTH = 4
PAST_LEN = 4096
CHUNK = 64
SUB = 16
NSUB = CHUNK // SUB
D_A = D_MODEL // 2
HGRN_DK = 128
HGRN_HEADS = D_A // HGRN_DK
HGRN_DV = D_A // HGRN_HEADS
D_B = D_MODEL // 2
HEAD_DIM = 64
N_Q = D_B // HEAD_DIM
N_KV = N_Q // 4
GROUP = N_Q // N_KV
WINDOW = 128
N_LOOKBACK_CHUNKS = WINDOW // CHUNK
BAND = WINDOW + CHUNK
ROT_DIM = HEAD_DIM // 4
ROPE_THETA = 500000.0
ATTN_SCALE = HEAD_DIM ** -0.5
EPS = 1e-6
NEG = -1e30
LOG2E = 1.4426950408889634
N_IN = 5 * D_A + N_Q * HEAD_DIM + 2 * N_KV * HEAD_DIM + D_B + 2 * D_MODEL

LANES = 128
SUBLANES = 8
KV_W = N_KV * HEAD_DIM
VMEM_LIMIT = 56 * 1024 * 1024

N_F32 = D_A
N_B16 = N_IN - N_F32
COL_MA, COL_MB = 0, D_MODEL
COL_QA, COL_IA, COL_GA, COL_ZA = (2 * D_MODEL + i * D_A for i in range(4))
COL_QB = COL_ZA + D_A
COL_ZB = COL_QB + D_B
COL_KV = COL_ZB + D_B
PROJ_TN = 512


def _proj_tile_order():
    sizes = dict(qa=D_A, fa=D_A, ia=D_A, ga=D_A, za=D_A, qb=D_B, kv=2 * KV_W, zb=D_B, ma=D_MODEL, mb=D_MODEL)
    start, first = 0, {}
    for name, n in sizes.items():
        first[name] = start // PROJ_TN
        start += n
    order = []
    for name in ("fa", "ma", "mb", "qa", "ia", "ga", "za", "qb", "zb", "kv"):
        order += range(first[name], first[name] + sizes[name] // PROJ_TN)
    return np.asarray(order, np.int32)


def _cparams(sem):
    return pltpu.CompilerParams(dimension_semantics=sem, vmem_limit_bytes=VMEM_LIMIT)


def _mod_kernel(c_ref, w_ref, b_ref, o_ref):
    acc = jnp.dot(c_ref[...].astype(BF16), w_ref[0].astype(BF16), preferred_element_type=F32)
    o_ref[0] = acc + b_ref[0]


def _modulation(c_all, ada_w, ada_b):
    n, d3, tn = c_all.shape[0], 3 * D_MODEL, 768
    return pl.pallas_call(
        _mod_kernel,
        grid=(DEPTH, d3 // tn),
        in_specs=[pl.BlockSpec((n, D_MODEL), lambda l, j: (0, 0)),
                  pl.BlockSpec((1, D_MODEL, tn), lambda l, j: (l, 0, j)),
                  pl.BlockSpec((1, 1, tn), lambda l, j: (l, 0, j))],
        out_specs=pl.BlockSpec((1, n, tn), lambda l, j: (l, 0, j)),
        out_shape=jax.ShapeDtypeStruct((DEPTH, n, d3), F32),
        compiler_params=_cparams(("arbitrary", "arbitrary")),
        name="adaln_mod",
    )(c_all, ada_w, ada_b.reshape(DEPTH, 1, d3))


def _inproj_kernel(src_ref, x_ref, shift_ref, scale_ref, g_ref, w_ref, of_ref, ob_ref, h0_ref, h1_ref, *,
                   nb, tt, rc, nf):
    del src_ref
    i, j, n = pl.program_id(0), pl.program_id(1), pl.program_id(2)
    tile = i * pl.num_programs(1) + j
    chunks_per_seq = tt // rc
    n_chunks = nb * chunks_per_seq

    def norm_chunk(k, dst_ref):
        b = k // chunks_per_seq if nb > 1 else 0
        r0 = pl.multiple_of((k - b * chunks_per_seq) * rc, rc)
        x = x_ref[b, pl.ds(r0, rc), :]
        y = x * lax.rsqrt(jnp.mean(x * x, axis=-1, keepdims=True) + EPS)
        h = (y * g_ref[...]) * (1.0 + scale_ref[b]) + shift_ref[b]
        dst_ref[pl.ds(pl.multiple_of(k * rc, rc), rc), :] = h.astype(BF16)

    @pl.when(jnp.logical_and(tile == 0, n == 0))
    def _():
        def body(k, carry):
            norm_chunk(k, h0_ref)
            return carry

        lax.fori_loop(0, n_chunks, body, 0)

    def step(cur_ref, nxt_ref, with_norm):
        if with_norm:
            norm_chunk(n - 1, nxt_ref)
        acc = jnp.dot(cur_ref[...], w_ref[...], preferred_element_type=F32)

        @pl.when(n < nf)
        def _():
            of_ref[...] = acc.reshape(of_ref.shape)

        @pl.when(n >= nf)
        def _():
            ob_ref[...] = acc.astype(BF16).reshape(ob_ref.shape)

    even = lax.rem(tile, 2) == 0
    normalising = jnp.logical_and(n >= 1, n <= n_chunks)
    for cur_ref, nxt_ref, is_cur in ((h0_ref, h1_ref, even), (h1_ref, h0_ref, jnp.logical_not(even))):
        pl.when(jnp.logical_and(is_cur, normalising))(
            functools.partial(step, cur_ref, nxt_ref, True))
        pl.when(jnp.logical_and(is_cur, jnp.logical_not(normalising)))(
            functools.partial(step, cur_ref, nxt_ref, False))


def _in_projection(x, shift, scale, norm_g, w, *, layer, nb, tt):
    bsz, t, _ = x.shape
    tn = PROJ_TN
    rc = min(tt, 128)
    nf = N_F32 // tn
    tiles_per_row = t // tt
    n_tiles = (bsz // nb) * tiles_per_row
    assert nb * (tt // rc) < N_IN // tn
    kern = functools.partial(_inproj_kernel, nb=nb, tt=tt, rc=rc, nf=nf)

    def ahead(i, j, n):
        tile = i * tiles_per_row + j
        nxt = jnp.where(jnp.logical_and(tile == 0, n == 0), 0, jnp.minimum(tile + 1, n_tiles - 1))
        return nxt // tiles_per_row, lax.rem(nxt, tiles_per_row)

    def x_map(i, j, n, src):
        ia, ja = ahead(i, j, n)
        return ia, ja, 0

    def mod_map(i, j, n, src):
        return ahead(i, j, n)[0], 0, 0

    grid_spec = pltpu.PrefetchScalarGridSpec(
        num_scalar_prefetch=1,
        grid=(bsz // nb, tiles_per_row, N_IN // tn),
        in_specs=[pl.BlockSpec((nb, tt, D_MODEL), x_map),
                  pl.BlockSpec((nb, 1, D_MODEL), mod_map),
                  pl.BlockSpec((nb, 1, D_MODEL), mod_map),
                  pl.BlockSpec((1, D_MODEL), lambda i, j, n, src: (0, 0)),
                  pl.BlockSpec((None, D_MODEL, tn), lambda i, j, n, src: (layer, 0, src[n]))],
        out_specs=[pl.BlockSpec((nb, tt, tn), lambda i, j, n, src: (i, j, jnp.minimum(n, nf - 1))),
                   pl.BlockSpec((nb, tt, tn), lambda i, j, n, src: (i, j, jnp.maximum(n - nf, 0)))],
        scratch_shapes=[pltpu.VMEM((nb * tt, D_MODEL), BF16), pltpu.VMEM((nb * tt, D_MODEL), BF16)],
    )
    return pl.pallas_call(
        kern,
        grid_spec=grid_spec,
        out_shape=[jax.ShapeDtypeStruct((bsz, t, N_F32), F32),
                   jax.ShapeDtypeStruct((bsz, t, N_B16), BF16)],
        compiler_params=_cparams(("arbitrary", "arbitrary", "arbitrary")),
        name="in_projection",
    )(jnp.asarray(_proj_tile_order()), x, shift, scale, norm_g, w)


HEADS_PER_LOOP = 8
LEVELS = CHUNK.bit_length() - 1
MXU_LEVELS = 2


def _hgrn_kernel(q_ref, f_ref, i_ref, g_ref, z_ref, lb_ref, ng_ref, s0_ref, mz_ref,
                 u_ref, sout_ref, st_ref, *, n_chunks):
    t = pl.program_id(1)

    @pl.when(t == 0)
    def _():
        for h in range(HGRN_HEADS):
            st_ref[h] = s0_ref[0, h].T

    row = lax.broadcasted_iota(jnp.int32, (CHUNK, CHUNK), 0)
    col = lax.broadcasted_iota(jnp.int32, (CHUNK, CHUNK), 1)
    code = jnp.where(row > col, row ^ col, jnp.where(row == col, 0, 2 * CHUNK))
    small_masks = [jnp.right_shift(code, l) == 1 for l in range(LEVELS - MXU_LEVELS)]
    diag_mask = code == 0
    sub_row = lax.broadcasted_iota(jnp.int32, (SUBLANES, HGRN_DK), 0)
    upper_half = [(jnp.right_shift(sub_row, l) & 1) == 1 for l in range(SUBLANES.bit_length() - 1)]
    nt_dims = (((1,), (1,)), ((), ()))
    tn_dims = (((0,), (0,)), ((), ()))

    def padded(x, r0):
        parts = []
        if r0:
            parts.append(jnp.zeros((r0, HGRN_DK), F32))
        parts.append(x)
        if CHUNK - r0 - x.shape[0]:
            parts.append(jnp.zeros((CHUNK - r0 - x.shape[0], HGRN_DK), F32))
        return jnp.concatenate(parts, axis=0).astype(BF16)

    def head_chunk(h, rows):
        cols = slice(h * HGRN_DK, (h + 1) * HGRN_DK)
        lb = lb_ref[:, cols]
        a = f_ref[0, rows, cols]
        qa = q_ref[0, rows, cols].astype(F32)
        v = i_ref[0, rows, cols]
        a2 = a * LOG2E
        e = jnp.exp2(-a2)
        l1e = jnp.log(1.0 + e) * LOG2E
        log_f = jnp.log(1.0 + lb * e) * LOG2E - l1e
        k = jnp.exp2((jnp.log(1.0 - lb) * LOG2E - a2) - l1e)
        q = qa * jax.nn.sigmoid(qa)

        hi = log_f.astype(BF16)
        r1 = log_f - hi.astype(F32)
        mid = r1.astype(BF16)
        lo = (r1 - mid.astype(F32)).astype(BF16)
        zall = yield jnp.concatenate([hi, mid, lo], axis=0)
        b = zall[:CHUNK]

        def level_sums(l):
            i = LEVELS - l
            return zall[i * CHUNK:(i + 1) * CHUNK]

        st = st_ref[h]
        q_dec = (q * jnp.exp2(b)).astype(BF16)
        st16 = st.astype(BF16)
        x_cols, y_cols = [], []
        for l in range(LEVELS - 1, LEVELS - 1 - MXU_LEVELS, -1):
            half = 1 << l
            w = jnp.exp2(level_sums(l))
            for base in range(0, CHUNK, 2 * half):
                lo_rows = slice(base, base + half)
                hi_rows = slice(base + half, base + 2 * half)
                x_cols.append(padded(q[hi_rows] * w[hi_rows], base + half))
                y_cols.append(padded(k[lo_rows] * w[lo_rows], base))
        x_big, y_big = jnp.concatenate(x_cols, axis=1), jnp.concatenate(y_cols, axis=1)
        pairs = [(q.astype(BF16), k.astype(BF16))]
        for l in range(LEVELS - MXU_LEVELS):
            w = jnp.exp2(level_sums(l))
            if (1 << l) < SUBLANES:
                side = jnp.concatenate([jnp.where(upper_half[l], q[r:r + SUBLANES], k[r:r + SUBLANES])
                                        for r in range(0, CHUNK, SUBLANES)], axis=0)
            else:
                side = jnp.concatenate([q[r:r + SUBLANES] if (r >> l) & 1 else k[r:r + SUBLANES]
                                        for r in range(0, CHUNK, SUBLANES)], axis=0)
            tl = (w * side).astype(BF16)
            pairs.append((tl, tl))
        b_end = b[CHUNK - 1:CHUNK]
        k_dec = (k * jnp.exp2(b_end - b)).astype(BF16)
        st_new = st * jnp.exp2(b_end)
        yield
        o_inter = lax.dot_general(q_dec, st16, nt_dims, preferred_element_type=F32)
        a_big = lax.dot_general(x_big, y_big, nt_dims, preferred_element_type=F32)
        a_small = [lax.dot_general(x, y, nt_dims, preferred_element_type=F32) for x, y in pairs]
        st_add = lax.dot_general(v, k_dec, tn_dims, preferred_element_type=F32)
        yield
        small = jnp.where(diag_mask, a_small[0], 0.0)
        for l in range(LEVELS - MXU_LEVELS):
            small = jnp.where(small_masks[l], a_small[1 + l], small)
        a_mat = (a_big + small).astype(BF16)
        st_ref[h] = st_new + st_add
        yield
        o_intra = jnp.dot(a_mat, v, preferred_element_type=F32)
        yield
        o = o_inter + o_intra
        y = o * lax.rsqrt(jnp.mean(o * o, axis=-1, keepdims=True) + EPS) * ng_ref[:, cols]
        ga = g_ref[0, rows, cols].astype(F32)
        za = z_ref[0, rows, cols].astype(F32)
        u = y * jax.nn.sigmoid(ga) * (za * jax.nn.sigmoid(za))
        u_ref[0, rows, cols] = u.astype(BF16)

    for h0 in range(0, HGRN_HEADS, HEADS_PER_LOOP):
        def chunk(c, carry, h0=h0):
            rows = pl.ds(pl.multiple_of(c * CHUNK, CHUNK), CHUNK)
            stages = [head_chunk(h, rows) for h in range(h0, h0 + HEADS_PER_LOOP)]
            splits = [next(s) for s in stages]
            sums = [jnp.dot(mz_ref[...], jnp.concatenate(splits[i:i + 2], axis=1), preferred_element_type=F32)
                    for i in range(0, HEADS_PER_LOOP, 2)]
            for i, s in enumerate(stages):
                s.send(sums[i // 2][:, (i % 2) * HGRN_DK:(i % 2 + 1) * HGRN_DK])
            while stages:
                stages = [s for s in stages if next(s, "done") != "done"]
            return carry

        lax.fori_loop(0, n_chunks, chunk, 0)

    @pl.when(t == pl.num_programs(1) - 1)
    def _():
        for h in range(HGRN_HEADS):
            sout_ref[0, h] = st_ref[h].T


def _decay_sum_matrix():
    m = np.zeros((1 + LEVELS, CHUNK, CHUNK), np.float32)
    idx = np.arange(CHUNK)
    m[0] = idx[:, None] >= idx[None, :]
    for i, l in enumerate(range(LEVELS - 1, -1, -1)):
        half = 1 << l
        for r in range(CHUNK):
            pos = r % (2 * half)
            first_upper = r - pos + half
            if pos >= half:
                m[1 + i, r, first_upper:r + 1] = 1.0
            else:
                m[1 + i, r, r + 1:first_upper] = 1.0
    m = m.reshape(-1, CHUNK)
    return jnp.asarray(np.concatenate([m, m, m], axis=1), dtype=BF16)


def _hgrn(pf, pb, lb, ng, s0, *, tt):
    bsz, t, _ = pb.shape

    def col(c0):
        return pl.BlockSpec((1, tt, D_A), lambda b, j, c0=c0: (b, j, c0 // D_A))

    vec = pl.BlockSpec((1, D_A), lambda b, j: (0, 0))
    state = pl.BlockSpec((1, HGRN_HEADS, HGRN_DK, HGRN_DV), lambda b, j: (b, 0, 0, 0))
    mz = _decay_sum_matrix()
    return pl.pallas_call(
        functools.partial(_hgrn_kernel, n_chunks=tt // CHUNK),
        grid=(bsz, t // tt),
        in_specs=[col(COL_QA), col(0), col(COL_IA), col(COL_GA), col(COL_ZA), vec, vec, state,
                  pl.BlockSpec(mz.shape, lambda b, j: (0, 0))],
        out_specs=[pl.BlockSpec((1, tt, D_A), lambda b, j: (b, j, 0)), state],
        out_shape=[jax.ShapeDtypeStruct((bsz, t, D_A), BF16),
                   jax.ShapeDtypeStruct((bsz, HGRN_HEADS, HGRN_DK, HGRN_DV), F32)],
        scratch_shapes=[pltpu.VMEM((HGRN_HEADS, HGRN_DV, HGRN_DK), F32)],
        compiler_params=_cparams(("arbitrary", "arbitrary")),
        name="hgrn2",
    )(pb, pf, pb, pb, pb, lb, ng, s0, mz)


def _rope(x, cos, sin_lo, sin_hi):
    half = ROT_DIM // 2
    return x * cos + pltpu.roll(x, LANES - half, 1) * sin_lo + pltpu.roll(x, half, 1) * sin_hi


def _attn_kernel(sink_ref, q_ref, kv_ref, z_ref, cos_ref, slo_ref, shi_ref, ck_ref, cv_ref,
                 u_ref, ko_ref, vo_ref, kbuf, vbuf, *, tt, masked, outw):
    t = pl.program_id(1)
    last = t == pl.num_programs(1) - 1
    n_chunks = tt // CHUNK
    n_slabs = KV_W // LANES

    def store_band(buf, r0, x, lanes):
        buf[r0:r0 + x.shape[0], lanes] = x.astype(BF16)
        buf[r0:r0 + x.shape[0], slice(KV_W + lanes.start, KV_W + lanes.stop)] = (
            pltpu.roll(x, HEAD_DIM, 1).astype(BF16))

    @pl.when(t == 0)
    def _():
        for j in range(n_slabs):
            lanes = slice(j * LANES, (j + 1) * LANES)
            store_band(kbuf, 0, ck_ref[0, :, lanes], lanes)
            store_band(vbuf, 0, cv_ref[0, :, lanes], lanes)

    if tt >= WINDOW:
        @pl.when(t > 0)
        def _():
            kbuf[0:WINDOW, :] = kbuf[tt:tt + WINDOW, :]
            vbuf[0:WINDOW, :] = vbuf[tt:tt + WINDOW, :]

    for j in range(n_slabs):
        lanes = slice(j * LANES, (j + 1) * LANES)
        kr = _rope(kv_ref[0, :, lanes].astype(F32), cos_ref[...], slo_ref[...], shi_ref[...])
        store_band(kbuf, WINDOW, kr, lanes)
        store_band(vbuf, WINDOW, kv_ref[0, :, slice(KV_W + lanes.start, KV_W + lanes.stop)].astype(F32), lanes)

        @pl.when(last)
        def _(kr=kr, lanes=lanes):
            ko_ref[0, :, lanes] = kr[tt - outw:, :]

    @pl.when(last)
    def _():
        vo_ref[0] = kv_ref[0, tt - outw:tt, KV_W:2 * KV_W].astype(F32)

    lane = lax.broadcasted_iota(jnp.int32, (CHUNK, LANES), 1)
    low = lane < HEAD_DIM
    heads_per_slab = LANES // HEAD_DIM * GROUP

    def kv_in_low_half(p):
        return (2 * p) // GROUP == 0

    def matched_head(p):
        return 0 if kv_in_low_half(p) else 1

    sink_rows = []
    for j in range(n_slabs):
        order = ([heads_per_slab * j + 2 * p + matched_head(p) for p in range(GROUP)]
                 + [heads_per_slab * j + 2 * p + 1 - matched_head(p) for p in range(GROUP)])
        sink_rows.append(jnp.concatenate([jnp.full((1, CHUNK), sink_ref[h], F32) for h in order], axis=1))
    nt_dims = (((1,), (1,)), ((), ()))
    tn_dims = (((0,), (0,)), ((), ()))
    half_cols = GROUP * CHUNK

    def chunk(c, carry):
        r0 = pl.multiple_of(c * CHUNK, CHUNK)
        rows = pl.ds(r0, CHUNK)
        band = pl.ds(r0, BAND)
        cq, sl, sh = cos_ref[rows, :], slo_ref[rows, :], shi_ref[rows, :]
        seq_chunk = t * n_chunks + c

        def slab_chunk(j):
            plain = slice(j * LANES, (j + 1) * LANES)
            swapped = slice(KV_W + j * LANES, KV_W + (j + 1) * LANES)
            k_plain, k_swap = kbuf[band, plain], kbuf[band, swapped]
            v_plain, v_swap = vbuf[band, plain], vbuf[band, swapped]
            q_plain, q_swap = [], []
            for p in range(GROUP):
                slab = GROUP * j + p
                x = _rope(q_ref[0, rows, slab * LANES:(slab + 1) * LANES].astype(F32), cq, sl, sh) * ATTN_SCALE
                halves = [jnp.where(low, x, 0.0).astype(BF16), jnp.where(low, 0.0, x).astype(BF16)]
                q_plain.append(halves[matched_head(p)])
                q_swap.append(halves[1 - matched_head(p)])
            q_plain = jnp.concatenate(q_plain, axis=0)
            q_swap = jnp.concatenate(q_swap, axis=0)
            yield
            s = jnp.concatenate([lax.dot_general(k_plain, q_plain, nt_dims, preferred_element_type=F32),
                                 lax.dot_general(k_swap, q_swap, nt_dims, preferred_element_type=F32)],
                                axis=1)
            yield
            if masked:
                blocks = [jnp.where(seq_chunk >= N_LOOKBACK_CHUNKS - i, s[i * CHUNK:(i + 1) * CHUNK], NEG)
                          for i in range(N_LOOKBACK_CHUNKS)]
                s = jnp.concatenate(blocks + [s[WINDOW:]], axis=0)
            sink = sink_rows[j]
            m = jnp.maximum(jnp.max(s, axis=0, keepdims=True), sink)
            pr = jnp.exp(s - m)
            inv = 1.0 / (jnp.sum(pr, axis=0, keepdims=True) + jnp.exp(sink - m))
            pr16 = pr.astype(BF16)
            yield
            o_plain = lax.dot_general(v_plain, pr16[:, :half_cols], tn_dims, preferred_element_type=F32)
            o_swap = lax.dot_general(v_swap, pr16[:, half_cols:], tn_dims, preferred_element_type=F32)
            yield
            o_plain = o_plain * inv[:, :half_cols]
            o_swap = o_swap * inv[:, half_cols:]
            for pair in range(GROUP // 2):
                t_plain = o_plain[:, pair * LANES:(pair + 1) * LANES].T
                t_swap = o_swap[:, pair * LANES:(pair + 1) * LANES].T
                for i in range(2):
                    p = 2 * pair + i
                    slab = GROUP * j + p
                    qrows = slice(i * CHUNK, (i + 1) * CHUNK)
                    if matched_head(p) == 0:
                        ob = jnp.where(low, t_plain[qrows], t_swap[qrows])
                    else:
                        ob = jnp.where(low, t_swap[qrows], t_plain[qrows])
                    zb = z_ref[0, rows, slab * LANES:(slab + 1) * LANES].astype(F32)
                    u_ref[0, rows, slab * LANES:(slab + 1) * LANES] = (
                        ob * (zb * jax.nn.sigmoid(zb))).astype(BF16)

        stages = [slab_chunk(j) for j in range(n_slabs)]
        while stages:
            stages = [s for s in stages if next(s, "done") != "done"]
        return carry

    lax.fori_loop(0, n_chunks, chunk, 0)


def _rope_tables(pos):
    half = ROT_DIM // 2
    inv = (ROPE_THETA ** (-np.arange(0, ROT_DIM, 2) / ROT_DIM)).astype(np.float32)
    ang = pos.astype(F32)[:, None] * inv[None, :]
    cos, sin = jnp.cos(ang), jnp.sin(ang)
    n = pos.shape[0]
    rest = HEAD_DIM - ROT_DIM
    cos_h = jnp.concatenate([cos, cos, jnp.ones((n, rest), F32)], axis=1)
    slo_h = jnp.concatenate([-sin, jnp.zeros((n, half + rest), F32)], axis=1)
    shi_h = jnp.concatenate([jnp.zeros((n, half), F32), sin, jnp.zeros((n, rest), F32)], axis=1)
    reps = LANES // HEAD_DIM
    return tuple(jnp.tile(a, (1, reps)) for a in (cos_h, slo_h, shi_h))


def _attention(pb, sinks, tables, cache_k, cache_v, *, tt, masked, outw):
    bsz, t, _ = pb.shape
    tab = pl.BlockSpec((tt, LANES), lambda b, j, s: (j, 0))
    cache = pl.BlockSpec((1, WINDOW, KV_W), lambda b, j, s: (b, 0, 0))
    win = pl.BlockSpec((1, outw, KV_W), lambda b, j, s: (b, 0, 0))
    grid_spec = pltpu.PrefetchScalarGridSpec(
        num_scalar_prefetch=1,
        grid=(bsz, t // tt),
        in_specs=[pl.BlockSpec((1, tt, D_B), lambda b, j, s: (b, j, COL_QB // D_B)),
                  pl.BlockSpec((1, tt, 2 * KV_W), lambda b, j, s: (b, j, COL_KV // (2 * KV_W))),
                  pl.BlockSpec((1, tt, D_B), lambda b, j, s: (b, j, COL_ZB // D_B)),
                  tab, tab, tab, cache, cache],
        out_specs=[pl.BlockSpec((1, tt, D_B), lambda b, j, s: (b, j, 0)), win, win],
        scratch_shapes=[pltpu.VMEM((WINDOW + tt, 2 * KV_W), BF16), pltpu.VMEM((WINDOW + tt, 2 * KV_W), BF16)],
    )
    return pl.pallas_call(
        functools.partial(_attn_kernel, tt=tt, masked=masked, outw=outw),
        grid_spec=grid_spec,
        out_shape=[jax.ShapeDtypeStruct((bsz, t, D_B), BF16),
                   jax.ShapeDtypeStruct((bsz, outw, KV_W), F32),
                   jax.ShapeDtypeStruct((bsz, outw, KV_W), F32)],
        compiler_params=_cparams(("arbitrary", "arbitrary")),
        name="swa_attention",
    )(sinks, pb, pb, pb, *tables, cache_k, cache_v)


def _out_kernel(ua_ref, ub_ref, ma_ref, mb_ref, x_ref, gate_ref, wpa_ref, wpb_ref, wout_ref, fg_ref,
                y_ref, *, nb, tt, final):
    m = nb * tt
    pa = jnp.dot(ua_ref[...].reshape(m, D_A), wpa_ref[...], preferred_element_type=F32)
    pb = jnp.dot(ub_ref[...].reshape(m, D_B), wpb_ref[...], preferred_element_type=F32)
    ma = ma_ref[...].reshape(m, D_MODEL).astype(F32)
    mb = mb_ref[...].reshape(m, D_MODEL).astype(F32)
    merged = jax.nn.sigmoid(ma) * pa + jax.nn.sigmoid(mb) * pb
    o = jnp.dot(merged.astype(BF16), wout_ref[...], preferred_element_type=F32)
    y = x_ref[...] + gate_ref[...] * o.reshape(nb, tt, D_MODEL)
    if final:
        y = (y * lax.rsqrt(jnp.mean(y * y, axis=-1, keepdims=True) + EPS)) * fg_ref[...]
    y_ref[...] = y


def _output(ua, ub, pb, x, gate, wpa, wpb, wout, final_g, *, layer, nb, tt, final):
    bsz, t, _ = x.shape
    tok = lambda i, j: (i, j, 0)
    const = lambda i, j: (0, 0)
    of_layer = lambda i, j: (layer, 0, 0)
    single = pl.Buffered(1)
    return pl.pallas_call(
        functools.partial(_out_kernel, nb=nb, tt=tt, final=final),
        grid=(bsz // nb, t // tt),
        in_specs=[pl.BlockSpec((nb, tt, D_A), tok),
                  pl.BlockSpec((nb, tt, D_B), tok),
                  pl.BlockSpec((nb, tt, D_MODEL), lambda i, j: (i, j, COL_MA // D_MODEL)),
                  pl.BlockSpec((nb, tt, D_MODEL), lambda i, j: (i, j, COL_MB // D_MODEL)),
                  pl.BlockSpec((nb, tt, D_MODEL), tok),
                  pl.BlockSpec((nb, 1, D_MODEL), lambda i, j: (i, 0, 0)),
                  pl.BlockSpec((None, D_A, D_MODEL), of_layer, pipeline_mode=single),
                  pl.BlockSpec((None, D_B, D_MODEL), of_layer, pipeline_mode=single),
                  pl.BlockSpec((None, D_MODEL, D_MODEL), of_layer, pipeline_mode=single),
                  pl.BlockSpec((1, D_MODEL), const)],
        out_specs=pl.BlockSpec((nb, tt, D_MODEL), tok),
        out_shape=jax.ShapeDtypeStruct((bsz, t, D_MODEL), F32),
        compiler_params=_cparams(("arbitrary", "arbitrary")),
        name="merge_output",
    )(ua, ub, pb, pb, x, gate, wpa, wpb, wout, final_g)


def _tiles(t):
    if t >= 1024:
        return dict(proj=(1, 1024), out=(1, 256), hgrn_tt=512, attn_tt=512)
    return dict(proj=(16, t), out=(4, t), hgrn_tt=t, attn_tt=t)


def _layer(x, mod, w, lb, s0, cache_k, cache_v, tables, *, final, final_g):
    bsz, t, _ = x.shape
    tl = _tiles(t)
    shift, scale, gate = (mod[:, None, i * D_MODEL:(i + 1) * D_MODEL] for i in range(3))
    nb, tt = tl["proj"]
    pf, pb = _in_projection(x, shift, scale, w["norm_g"], w["w_in"], layer=w["layer"], nb=min(nb, bsz), tt=tt)
    ua, s_new = _hgrn(pf, pb, lb, w["hgrn_g"], s0, tt=tl["hgrn_tt"])
    masked = cache_k is None
    if masked:
        cache_k = jnp.zeros((bsz, WINDOW, KV_W), F32)
        cache_v = cache_k
    outw = min(t, WINDOW)
    ub, k_new, v_new = _attention(pb, w["sinks"], tables, cache_k, cache_v,
                                  tt=tl["attn_tt"], masked=masked, outw=outw)
    nb, tt = tl["out"]
    y = _output(ua, ub, pb, x, gate, w["w_pa"], w["w_pb"], w["w_out"], final_g,
                layer=w["layer"], nb=min(nb, bsz), tt=tt, final=final)
    shape = (bsz, outw, N_KV, HEAD_DIM)
    return y, s_new, k_new.reshape(shape), v_new.reshape(shape)


def kernel(x_prompt, x_sample, c_prompt, c_sample, state_hgrn, cache_win_k, cache_win_v, ada_w, ada_b,
           norm_g, w_in, lb_logits, hgrn_norm_g, sinks, w_branch_a, w_branch_b, w_out, final_norm_g):
    bp, tp = x_prompt.shape[0], x_prompt.shape[1]
    bs, ts = x_sample.shape[0], x_sample.shape[1]

    w_in_p = w_in.astype(BF16)
    w_pa = w_branch_a.astype(BF16)
    w_pb = w_branch_b.astype(BF16)
    w_o = w_out.astype(BF16)
    prob = jax.nn.softmax(lb_logits.astype(F32), axis=0)
    lb = jnp.cumsum(prob, axis=0) - prob[:1]
    final_g = final_norm_g.reshape(1, D_MODEL)

    mod = _modulation(jnp.concatenate([c_prompt, c_sample], axis=0), ada_w, ada_b)
    tab_p = _rope_tables(jnp.arange(tp))
    tab_s = _rope_tables(PAST_LEN + jnp.arange(ts))
    zero_state = jnp.zeros((bp, HGRN_HEADS, HGRN_DK, HGRN_DV), F32)

    hp, hs = x_prompt, x_sample
    outs = [[] for _ in range(6)]
    for l in range(DEPTH):
        w = dict(norm_g=norm_g[l].reshape(1, D_MODEL), w_in=w_in_p, hgrn_g=hgrn_norm_g[l].reshape(1, D_A),
                 sinks=sinks[l], w_pa=w_pa, w_pb=w_pb, w_out=w_o, layer=l)
        lbl = lb[l].reshape(1, D_A)
        final = l == DEPTH - 1
        hp, s_p, k_p, v_p = _layer(hp, mod[l, :bp], w, lbl, zero_state, None, None, tab_p,
                                   final=final, final_g=final_g)
        hs, s_s, k_s, v_s = _layer(hs, mod[l, bp:], w, lbl, state_hgrn[l],
                                   cache_win_k[l].reshape(bs, WINDOW, KV_W),
                                   cache_win_v[l].reshape(bs, WINDOW, KV_W), tab_s,
                                   final=final, final_g=final_g)
        for acc, val in zip(outs, (s_p, k_p, v_p, s_s, k_s, v_s)):
            acc.append(val)
    return (hp, hs) + tuple(jnp.stack(o) for o in outs)
```

```python
import functools

import jax
import jax.numpy as jnp
import numpy as np
from jax import lax
from jax.experimental import pallas as pl
from jax.experimental.pallas import tpu as pltpu

F32 = jnp.float32
BF16 = jnp.bfloat16

D_MODEL = 2048
DEPTH = 4
PAST_LEN = 4096
CHUNK = 64
SUB = 16
NSUB = CHUNK // SUB
D_A = D_MODEL // 2
HGRN_DK = 128
HGRN_HEADS = D_A // HGRN_DK
HGRN_DV = D_A // HGRN_HEADS
D_B = D_MODEL // 2
HEAD_DIM = 64
N_Q = D_B // HEAD_DIM
N_KV = N_Q // 4
GROUP = N_Q // N_KV
WINDOW = 128
N_LOOKBACK_CHUNKS = WINDOW // CHUNK
BAND = WINDOW + CHUNK
ROT_DIM = HEAD_DIM // 4
ROPE_THETA = 500000.0
ATTN_SCALE = HEAD_DIM ** -0.5
EPS = 1e-6
NEG = -1e30
LOG2E = 1.4426950408889634
N_IN = 5 * D_A + N_Q * HEAD_DIM + 2 * N_KV * HEAD_DIM + D_B + 2 * D_MODEL

LANES = 128
SUBLANES = 8
KV_W = N_KV * HEAD_DIM
VMEM_LIMIT = 56 * 1024 * 1024

N_F32 = D_A
N_B16 = N_IN - N_F32
COL_MA, COL_MB = 0, D_MODEL
COL_QA, COL_IA, COL_GA, COL_ZA = (2 * D_MODEL + i * D_A for i in range(4))
COL_QB = COL_ZA + D_A
COL_ZB = COL_QB + D_B
COL_KV = COL_ZB + D_B
PROJ_TN = 512


def _proj_tile_order():
    sizes = dict(qa=D_A, fa=D_A, ia=D_A, ga=D_A, za=D_A, qb=D_B, kv=2 * KV_W, zb=D_B, ma=D_MODEL, mb=D_MODEL)
    start, first = 0, {}
    for name, n in sizes.items():
        first[name] = start // PROJ_TN
        start += n
    order = []
    for name in ("fa", "ma", "mb", "qa", "ia", "ga", "za", "qb", "zb", "kv"):
        order += range(first[name], first[name] + sizes[name] // PROJ_TN)
    return np.asarray(order, np.int32)


def _cparams(sem):
    return pltpu.CompilerParams(dimension_semantics=sem, vmem_limit_bytes=VMEM_LIMIT)


def _mod_kernel(c_ref, w_ref, b_ref, o_ref):
    acc = jnp.dot(c_ref[...].astype(BF16), w_ref[0].astype(BF16), preferred_element_type=F32)
    o_ref[0] = acc + b_ref[0]


def _modulation(c_all, ada_w, ada_b):
    n, d3, tn = c_all.shape[0], 3 * D_MODEL, 768
    return pl.pallas_call(
        _mod_kernel,
        grid=(DEPTH, d3 // tn),
        in_specs=[pl.BlockSpec((n, D_MODEL), lambda l, j: (0, 0)),
                  pl.BlockSpec((1, D_MODEL, tn), lambda l, j: (l, 0, j)),
                  pl.BlockSpec((1, 1, tn), lambda l, j: (l, 0, j))],
        out_specs=pl.BlockSpec((1, n, tn), lambda l, j: (l, 0, j)),
        out_shape=jax.ShapeDtypeStruct((DEPTH, n, d3), F32),
        compiler_params=_cparams(("arbitrary", "arbitrary")),
        name="adaln_mod",
    )(c_all, ada_w, ada_b.reshape(DEPTH, 1, d3))


def _inproj_kernel(src_ref, x_ref, shift_ref, scale_ref, g_ref, w_ref, of_ref, ob_ref, h_ref, *,
                   nb, tt, rc, nf):
    del src_ref
    n = pl.program_id(2)

    @pl.when(n == 0)
    def _():
        g = g_ref[...]
        for b in range(nb):
            mul = 1.0 + scale_ref[b]
            add = shift_ref[b]

            def rows(r, carry, b=b, mul=mul, add=add):
                r0 = pl.multiple_of(r * rc, rc)
                x = x_ref[b, pl.ds(r0, rc), :]
                y = x * lax.rsqrt(jnp.mean(x * x, axis=-1, keepdims=True) + EPS)
                h = (y * g) * mul + add
                h_ref[pl.ds(b * tt + r0, rc), :] = h.astype(BF16)
                return carry

            lax.fori_loop(0, tt // rc, rows, 0)

    acc = jnp.dot(h_ref[...], w_ref[...], preferred_element_type=F32)

    @pl.when(n < nf)
    def _():
        of_ref[...] = acc.reshape(of_ref.shape)

    @pl.when(n >= nf)
    def _():
        ob_ref[...] = acc.astype(BF16).reshape(ob_ref.shape)


def _in_projection(x, shift, scale, norm_g, w, *, layer, nb, tt):
    bsz, t, _ = x.shape
    tn = PROJ_TN
    rc = min(tt, 128)
    nf = N_F32 // tn
    kern = functools.partial(_inproj_kernel, nb=nb, tt=tt, rc=rc, nf=nf)
    grid_spec = pltpu.PrefetchScalarGridSpec(
        num_scalar_prefetch=1,
        grid=(bsz // nb, t // tt, N_IN // tn),
        in_specs=[pl.BlockSpec((nb, tt, D_MODEL), lambda i, j, n, src: (i, j, 0),
                               pipeline_mode=pl.Buffered(1)),
                  pl.BlockSpec((nb, 1, D_MODEL), lambda i, j, n, src: (i, 0, 0)),
                  pl.BlockSpec((nb, 1, D_MODEL), lambda i, j, n, src: (i, 0, 0)),
                  pl.BlockSpec((1, D_MODEL), lambda i, j, n, src: (0, 0)),
                  pl.BlockSpec((None, D_MODEL, tn), lambda i, j, n, src: (layer, 0, src[n]))],
        out_specs=[pl.BlockSpec((nb, tt, tn), lambda i, j, n, src: (i, j, jnp.minimum(n, nf - 1))),
                   pl.BlockSpec((nb, tt, tn), lambda i, j, n, src: (i, j, jnp.maximum(n - nf, 0)))],
        scratch_shapes=[pltpu.VMEM((nb * tt, D_MODEL), BF16)],
    )
    return pl.pallas_call(
        kern,
        grid_spec=grid_spec,
        out_shape=[jax.ShapeDtypeStruct((bsz, t, N_F32), F32),
                   jax.ShapeDtypeStruct((bsz, t, N_B16), BF16)],
        compiler_params=_cparams(("arbitrary", "arbitrary", "arbitrary")),
        name="in_projection",
    )(jnp.asarray(_proj_tile_order()), x, shift, scale, norm_g, w)


HEADS_PER_LOOP = 8
LEVELS = CHUNK.bit_length() - 1
MXU_LEVELS = 2


def _hgrn_kernel(q_ref, f_ref, i_ref, g_ref, z_ref, lb_ref, ng_ref, s0_ref, mz_ref,
                 u_ref, sout_ref, st_ref, *, n_chunks):
    t = pl.program_id(1)

    @pl.when(t == 0)
    def _():
        for h in range(HGRN_HEADS):
            st_ref[h] = s0_ref[0, h].T

    row = lax.broadcasted_iota(jnp.int32, (CHUNK, CHUNK), 0)
    col = lax.broadcasted_iota(jnp.int32, (CHUNK, CHUNK), 1)
    code = jnp.where(row > col, row ^ col, jnp.where(row == col, 0, 2 * CHUNK))
    small_masks = [jnp.right_shift(code, l) == 1 for l in range(LEVELS - MXU_LEVELS)]
    diag_mask = code == 0
    sub_row = lax.broadcasted_iota(jnp.int32, (SUBLANES, HGRN_DK), 0)
    upper_half = [(jnp.right_shift(sub_row, l) & 1) == 1 for l in range(SUBLANES.bit_length() - 1)]
    nt_dims = (((1,), (1,)), ((), ()))
    tn_dims = (((0,), (0,)), ((), ()))

    def padded(x, r0):
        parts = []
        if r0:
            parts.append(jnp.zeros((r0, HGRN_DK), F32))
        parts.append(x)
        if CHUNK - r0 - x.shape[0]:
            parts.append(jnp.zeros((CHUNK - r0 - x.shape[0], HGRN_DK), F32))
        return jnp.concatenate(parts, axis=0).astype(BF16)

    def head_chunk(h, rows):
        cols = slice(h * HGRN_DK, (h + 1) * HGRN_DK)
        lb = lb_ref[:, cols]
        a = f_ref[0, rows, cols]
        qa = q_ref[0, rows, cols].astype(F32)
        v = i_ref[0, rows, cols]
        a2 = a * LOG2E
        e = jnp.exp2(-a2)
        l1e = jnp.log(1.0 + e) * LOG2E
        log_f = jnp.log(1.0 + lb * e) * LOG2E - l1e
        k = jnp.exp2((jnp.log(1.0 - lb) * LOG2E - a2) - l1e)
        q = qa * jax.nn.sigmoid(qa)

        hi = log_f.astype(BF16)
        r1 = log_f - hi.astype(F32)
        mid = r1.astype(BF16)
        lo = (r1 - mid.astype(F32)).astype(BF16)
        zall = yield jnp.concatenate([hi, mid, lo], axis=0)
        b = zall[:CHUNK]

        def level_sums(l):
            i = LEVELS - l
            return zall[i * CHUNK:(i + 1) * CHUNK]

        st = st_ref[h]
        q_dec = (q * jnp.exp2(b)).astype(BF16)
        st16 = st.astype(BF16)
        x_cols, y_cols = [], []
        for l in range(LEVELS - 1, LEVELS - 1 - MXU_LEVELS, -1):
            half = 1 << l
            w = jnp.exp2(level_sums(l))
            for base in range(0, CHUNK, 2 * half):
                lo_rows = slice(base, base + half)
                hi_rows = slice(base + half, base + 2 * half)
                x_cols.append(padded(q[hi_rows] * w[hi_rows], base + half))
                y_cols.append(padded(k[lo_rows] * w[lo_rows], base))
        x_big, y_big = jnp.concatenate(x_cols, axis=1), jnp.concatenate(y_cols, axis=1)
        pairs = [(q.astype(BF16), k.astype(BF16))]
        for l in range(LEVELS - MXU_LEVELS):
            w = jnp.exp2(level_sums(l))
            if (1 << l) < SUBLANES:
                side = jnp.concatenate([jnp.where(upper_half[l], q[r:r + SUBLANES], k[r:r + SUBLANES])
                                        for r in range(0, CHUNK, SUBLANES)], axis=0)
            else:
                side = jnp.concatenate([q[r:r + SUBLANES] if (r >> l) & 1 else k[r:r + SUBLANES]
                                        for r in range(0, CHUNK, SUBLANES)], axis=0)
            tl = (w * side).astype(BF16)
            pairs.append((tl, tl))
        b_end = b[CHUNK - 1:CHUNK]
        k_dec = (k * jnp.exp2(b_end - b)).astype(BF16)
        st_new = st * jnp.exp2(b_end)
        yield
        o_inter = lax.dot_general(q_dec, st16, nt_dims, preferred_element_type=F32)
        a_big = lax.dot_general(x_big, y_big, nt_dims, preferred_element_type=F32)
        a_small = [lax.dot_general(x, y, nt_dims, preferred_element_type=F32) for x, y in pairs]
        st_add = lax.dot_general(v, k_dec, tn_dims, preferred_element_type=F32)
        yield
        small = jnp.where(diag_mask, a_small[0], 0.0)
        for l in range(LEVELS - MXU_LEVELS):
            small = jnp.where(small_masks[l], a_small[1 + l], small)
        a_mat = (a_big + small).astype(BF16)
        st_ref[h] = st_new + st_add
        yield
        o_intra = jnp.dot(a_mat, v, preferred_element_type=F32)
        yield
        o = o_inter + o_intra
        y = o * lax.rsqrt(jnp.mean(o * o, axis=-1, keepdims=True) + EPS) * ng_ref[:, cols]
        ga = g_ref[0, rows, cols].astype(F32)
        za = z_ref[0, rows, cols].astype(F32)
        u = y * jax.nn.sigmoid(ga) * (za * jax.nn.sigmoid(za))
        u_ref[0, rows, cols] = u.astype(BF16)

    for h0 in range(0, HGRN_HEADS, HEADS_PER_LOOP):
        def chunk(c, carry, h0=h0):
            rows = pl.ds(pl.multiple_of(c * CHUNK, CHUNK), CHUNK)
            stages = [head_chunk(h, rows) for h in range(h0, h0 + HEADS_PER_LOOP)]
            splits = [next(s) for s in stages]
            sums = [jnp.dot(mz_ref[...], jnp.concatenate(splits[i:i + 2], axis=1), preferred_element_type=F32)
                    for i in range(0, HEADS_PER_LOOP, 2)]
            for i, s in enumerate(stages):
                s.send(sums[i // 2][:, (i % 2) * HGRN_DK:(i % 2 + 1) * HGRN_DK])
            while stages:
                stages = [s for s in stages if next(s, "done") != "done"]
            return carry

        lax.fori_loop(0, n_chunks, chunk, 0)

    @pl.when(t == pl.num_programs(1) - 1)
    def _():
        for h in range(HGRN_HEADS):
            sout_ref[0, h] = st_ref[h].T


def _decay_sum_matrix():
    m = np.zeros((1 + LEVELS, CHUNK, CHUNK), np.float32)
    idx = np.arange(CHUNK)
    m[0] = idx[:, None] >= idx[None, :]
    for i, l in enumerate(range(LEVELS - 1, -1, -1)):
        half = 1 << l
        for r in range(CHUNK):
            pos = r % (2 * half)
            first_upper = r - pos + half
            if pos >= half:
                m[1 + i, r, first_upper:r + 1] = 1.0
            else:
                m[1 + i, r, r + 1:first_upper] = 1.0
    m = m.reshape(-1, CHUNK)
    return jnp.asarray(np.concatenate([m, m, m], axis=1), dtype=BF16)


def _hgrn(pf, pb, lb, ng, s0, *, tt):
    bsz, t, _ = pb.shape

    def col(c0):
        return pl.BlockSpec((1, tt, D_A), lambda b, j, c0=c0: (b, j, c0 // D_A))

    vec = pl.BlockSpec((1, D_A), lambda b, j: (0, 0))
    state = pl.BlockSpec((1, HGRN_HEADS, HGRN_DK, HGRN_DV), lambda b, j: (b, 0, 0, 0))
    mz = _decay_sum_matrix()
    return pl.pallas_call(
        functools.partial(_hgrn_kernel, n_chunks=tt // CHUNK),
        grid=(bsz, t // tt),
        in_specs=[col(COL_QA), col(0), col(COL_IA), col(COL_GA), col(COL_ZA), vec, vec, state,
                  pl.BlockSpec(mz.shape, lambda b, j: (0, 0))],
        out_specs=[pl.BlockSpec((1, tt, D_A), lambda b, j: (b, j, 0)), state],
        out_shape=[jax.ShapeDtypeStruct((bsz, t, D_A), BF16),
                   jax.ShapeDtypeStruct((bsz, HGRN_HEADS, HGRN_DK, HGRN_DV), F32)],
        scratch_shapes=[pltpu.VMEM((HGRN_HEADS, HGRN_DV, HGRN_DK), F32)],
        compiler_params=_cparams(("arbitrary", "arbitrary")),
        name="hgrn2",
    )(pb, pf, pb, pb, pb, lb, ng, s0, mz)


def _rope(x, cos, sin_lo, sin_hi):
    half = ROT_DIM // 2
    return x * cos + pltpu.roll(x, LANES - half, 1) * sin_lo + pltpu.roll(x, half, 1) * sin_hi


def _attn_kernel(sink_ref, q_ref, kv_ref, z_ref, cos_ref, slo_ref, shi_ref, ck_ref, cv_ref,
                 u_ref, ko_ref, vo_ref, kbuf, vbuf, *, tt, masked, outw):
    t = pl.program_id(1)
    last = t == pl.num_programs(1) - 1
    n_chunks = tt // CHUNK
    n_slabs = KV_W // LANES

    def store_band(buf, r0, x, lanes):
        buf[r0:r0 + x.shape[0], lanes] = x.astype(BF16)
        buf[r0:r0 + x.shape[0], slice(KV_W + lanes.start, KV_W + lanes.stop)] = (
            pltpu.roll(x, HEAD_DIM, 1).astype(BF16))

    @pl.when(t == 0)
    def _():
        for j in range(n_slabs):
            lanes = slice(j * LANES, (j + 1) * LANES)
            store_band(kbuf, 0, ck_ref[0, :, lanes], lanes)
            store_band(vbuf, 0, cv_ref[0, :, lanes], lanes)

    if tt >= WINDOW:
        @pl.when(t > 0)
        def _():
            kbuf[0:WINDOW, :] = kbuf[tt:tt + WINDOW, :]
            vbuf[0:WINDOW, :] = vbuf[tt:tt + WINDOW, :]

    for j in range(n_slabs):
        lanes = slice(j * LANES, (j + 1) * LANES)
        kr = _rope(kv_ref[0, :, lanes].astype(F32), cos_ref[...], slo_ref[...], shi_ref[...])
        store_band(kbuf, WINDOW, kr, lanes)
        store_band(vbuf, WINDOW, kv_ref[0, :, slice(KV_W + lanes.start, KV_W + lanes.stop)].astype(F32), lanes)

        @pl.when(last)
        def _(kr=kr, lanes=lanes):
            ko_ref[0, :, lanes] = kr[tt - outw:, :]

    @pl.when(last)
    def _():
        vo_ref[0] = kv_ref[0, tt - outw:tt, KV_W:2 * KV_W].astype(F32)

    lane = lax.broadcasted_iota(jnp.int32, (CHUNK, LANES), 1)
    low = lane < HEAD_DIM
    heads_per_slab = LANES // HEAD_DIM * GROUP

    def kv_in_low_half(p):
        return (2 * p) // GROUP == 0

    def matched_head(p):
        return 0 if kv_in_low_half(p) else 1

    sink_rows = []
    for j in range(n_slabs):
        order = ([heads_per_slab * j + 2 * p + matched_head(p) for p in range(GROUP)]
                 + [heads_per_slab * j + 2 * p + 1 - matched_head(p) for p in range(GROUP)])
        sink_rows.append(jnp.concatenate([jnp.full((1, CHUNK), sink_ref[h], F32) for h in order], axis=1))
    nt_dims = (((1,), (1,)), ((), ()))
    tn_dims = (((0,), (0,)), ((), ()))
    half_cols = GROUP * CHUNK

    def chunk(c, carry):
        r0 = pl.multiple_of(c * CHUNK, CHUNK)
        rows = pl.ds(r0, CHUNK)
        band = pl.ds(r0, BAND)
        cq, sl, sh = cos_ref[rows, :], slo_ref[rows, :], shi_ref[rows, :]
        seq_chunk = t * n_chunks + c

        def slab_chunk(j):
            plain = slice(j * LANES, (j + 1) * LANES)
            swapped = slice(KV_W + j * LANES, KV_W + (j + 1) * LANES)
            k_plain, k_swap = kbuf[band, plain], kbuf[band, swapped]
            v_plain, v_swap = vbuf[band, plain], vbuf[band, swapped]
            q_plain, q_swap = [], []
            for p in range(GROUP):
                slab = GROUP * j + p
                x = _rope(q_ref[0, rows, slab * LANES:(slab + 1) * LANES].astype(F32), cq, sl, sh) * ATTN_SCALE
                halves = [jnp.where(low, x, 0.0).astype(BF16), jnp.where(low, 0.0, x).astype(BF16)]
                q_plain.append(halves[matched_head(p)])
                q_swap.append(halves[1 - matched_head(p)])
            q_plain = jnp.concatenate(q_plain, axis=0)
            q_swap = jnp.concatenate(q_swap, axis=0)
            yield
            s = jnp.concatenate([lax.dot_general(k_plain, q_plain, nt_dims, preferred_element_type=F32),
                                 lax.dot_general(k_swap, q_swap, nt_dims, preferred_element_type=F32)],
                                axis=1)
            yield
            if masked:
                blocks = [jnp.where(seq_chunk >= N_LOOKBACK_CHUNKS - i, s[i * CHUNK:(i + 1) * CHUNK], NEG)
                          for i in range(N_LOOKBACK_CHUNKS)]
                s = jnp.concatenate(blocks + [s[WINDOW:]], axis=0)
            sink = sink_rows[j]
            m = jnp.maximum(jnp.max(s, axis=0, keepdims=True), sink)
            pr = jnp.exp(s - m)
            inv = 1.0 / (jnp.sum(pr, axis=0, keepdims=True) + jnp.exp(sink - m))
            pr16 = pr.astype(BF16)
            yield
            o_plain = lax.dot_general(v_plain, pr16[:, :half_cols], tn_dims, preferred_element_type=F32)
            o_swap = lax.dot_general(v_swap, pr16[:, half_cols:], tn_dims, preferred_element_type=F32)
            yield
            o_plain = o_plain * inv[:, :half_cols]
            o_swap = o_swap * inv[:, half_cols:]
            for pair in range(GROUP // 2):
                t_plain = o_plain[:, pair * LANES:(pair + 1) * LANES].T
                t_swap = o_swap[:, pair * LANES:(pair + 1) * LANES].T
                for i in range(2):
                    p = 2 * pair + i
                    slab = GROUP * j + p
                    qrows = slice(i * CHUNK, (i + 1) * CHUNK)
                    if matched_head(p) == 0:
                        ob = jnp.where(low, t_plain[qrows], t_swap[qrows])
                    else:
                        ob = jnp.where(low, t_swap[qrows], t_plain[qrows])
                    zb = z_ref[0, rows, slab * LANES:(slab + 1) * LANES].astype(F32)
                    u_ref[0, rows, slab * LANES:(slab + 1) * LANES] = (
                        ob * (zb * jax.nn.sigmoid(zb))).astype(BF16)

        stages = [slab_chunk(j) for j in range(n_slabs)]
        while stages:
            stages = [s for s in stages if next(s, "done") != "done"]
        return carry

    lax.fori_loop(0, n_chunks, chunk, 0)


def _rope_tables(pos):
    half = ROT_DIM // 2
    inv = (ROPE_THETA ** (-np.arange(0, ROT_DIM, 2) / ROT_DIM)).astype(np.float32)
    ang = pos.astype(F32)[:, None] * inv[None, :]
    cos, sin = jnp.cos(ang), jnp.sin(ang)
    n = pos.shape[0]
    rest = HEAD_DIM - ROT_DIM
    cos_h = jnp.concatenate([cos, cos, jnp.ones((n, rest), F32)], axis=1)
    slo_h = jnp.concatenate([-sin, jnp.zeros((n, half + rest), F32)], axis=1)
    shi_h = jnp.concatenate([jnp.zeros((n, half), F32), sin, jnp.zeros((n, rest), F32)], axis=1)
    reps = LANES // HEAD_DIM
    return tuple(jnp.tile(a, (1, reps)) for a in (cos_h, slo_h, shi_h))


def _attention(pb, sinks, tables, cache_k, cache_v, *, tt, masked, outw):
    bsz, t, _ = pb.shape
    tab = pl.BlockSpec((tt, LANES), lambda b, j, s: (j, 0))
    cache = pl.BlockSpec((1, WINDOW, KV_W), lambda b, j, s: (b, 0, 0))
    win = pl.BlockSpec((1, outw, KV_W), lambda b, j, s: (b, 0, 0))
    grid_spec = pltpu.PrefetchScalarGridSpec(
        num_scalar_prefetch=1,
        grid=(bsz, t // tt),
        in_specs=[pl.BlockSpec((1, tt, D_B), lambda b, j, s: (b, j, COL_QB // D_B)),
                  pl.BlockSpec((1, tt, 2 * KV_W), lambda b, j, s: (b, j, COL_KV // (2 * KV_W))),
                  pl.BlockSpec((1, tt, D_B), lambda b, j, s: (b, j, COL_ZB // D_B)),
                  tab, tab, tab, cache, cache],
        out_specs=[pl.BlockSpec((1, tt, D_B), lambda b, j, s: (b, j, 0)), win, win],
        scratch_shapes=[pltpu.VMEM((WINDOW + tt, 2 * KV_W), BF16), pltpu.VMEM((WINDOW + tt, 2 * KV_W), BF16)],
    )
    return pl.pallas_call(
        functools.partial(_attn_kernel, tt=tt, masked=masked, outw=outw),
        grid_spec=grid_spec,
        out_shape=[jax.ShapeDtypeStruct((bsz, t, D_B), BF16),
                   jax.ShapeDtypeStruct((bsz, outw, KV_W), F32),
                   jax.ShapeDtypeStruct((bsz, outw, KV_W), F32)],
        compiler_params=_cparams(("arbitrary", "arbitrary")),
        name="swa_attention",
    )(sinks, pb, pb, pb, *tables, cache_k, cache_v)


def _out_kernel(ua_ref, ub_ref, ma_ref, mb_ref, x_ref, gate_ref, wpa_ref, wpb_ref, wout_ref, fg_ref,
                y_ref, *, nb, tt, final):
    m = nb * tt
    pa = jnp.dot(ua_ref[...].reshape(m, D_A), wpa_ref[...], preferred_element_type=F32)
    pb = jnp.dot(ub_ref[...].reshape(m, D_B), wpb_ref[...], preferred_element_type=F32)
    ma = ma_ref[...].reshape(m, D_MODEL).astype(F32)
    mb = mb_ref[...].reshape(m, D_MODEL).astype(F32)
    merged = jax.nn.sigmoid(ma) * pa + jax.nn.sigmoid(mb) * pb
    o = jnp.dot(merged.astype(BF16), wout_ref[...], preferred_element_type=F32)
    y = x_ref[...] + gate_ref[...] * o.reshape(nb, tt, D_MODEL)
    if final:
        y = (y * lax.rsqrt(jnp.mean(y * y, axis=-1, keepdims=True) + EPS)) * fg_ref[...]
    y_ref[...] = y


def _output(ua, ub, pb, x, gate, wpa, wpb, wout, final_g, *, layer, nb, tt, final):
    bsz, t, _ = x.shape
    tok = lambda i, j: (i, j, 0)
    const = lambda i, j: (0, 0)
    of_layer = lambda i, j: (layer, 0, 0)
    single = pl.Buffered(1)
    return pl.pallas_call(
        functools.partial(_out_kernel, nb=nb, tt=tt, final=final),
        grid=(bsz // nb, t // tt),
        in_specs=[pl.BlockSpec((nb, tt, D_A), tok),
                  pl.BlockSpec((nb, tt, D_B), tok),
                  pl.BlockSpec((nb, tt, D_MODEL), lambda i, j: (i, j, COL_MA // D_MODEL)),
                  pl.BlockSpec((nb, tt, D_MODEL), lambda i, j: (i, j, COL_MB // D_MODEL)),
                  pl.BlockSpec((nb, tt, D_MODEL), tok),
                  pl.BlockSpec((nb, 1, D_MODEL), lambda i, j: (i, 0, 0)),
                  pl.BlockSpec((None, D_A, D_MODEL), of_layer, pipeline_mode=single),
                  pl.BlockSpec((None, D_B, D_MODEL), of_layer, pipeline_mode=single),
                  pl.BlockSpec((None, D_MODEL, D_MODEL), of_layer, pipeline_mode=single),
                  pl.BlockSpec((1, D_MODEL), const)],
        out_specs=pl.BlockSpec((nb, tt, D_MODEL), tok),
        out_shape=jax.ShapeDtypeStruct((bsz, t, D_MODEL), F32),
        compiler_params=_cparams(("arbitrary", "arbitrary")),
        name="merge_output",
    )(ua, ub, pb, pb, x, gate, wpa, wpb, wout, final_g)


def _tiles(t):
    if t >= 1024:
        return dict(proj=(1, min(t, 2048)), out=(1, 512), hgrn_tt=min(t, 1024), attn_tt=min(t, 1024))
    return dict(proj=(16, t), out=(4, t), hgrn_tt=t, attn_tt=t)


def _layer(x, mod, w, lb, s0, cache_k, cache_v, tables, *, final, final_g):
    bsz, t, _ = x.shape
    tl = _tiles(t)
    shift, scale, gate = (mod[:, None, i * D_MODEL:(i + 1) * D_MODEL] for i in range(3))
    nb, tt = tl["proj"]
    pf, pb = _in_projection(x, shift, scale, w["norm_g"], w["w_in"], layer=w["layer"], nb=min(nb, bsz), tt=tt)
    ua, s_new = _hgrn(pf, pb, lb, w["hgrn_g"], s0, tt=tl["hgrn_tt"])
    masked = cache_k is None
    if masked:
        cache_k = jnp.zeros((bsz, WINDOW, KV_W), F32)
        cache_v = cache_k
    outw = min(t, WINDOW)
    ub, k_new, v_new = _attention(pb, w["sinks"], tables, cache_k, cache_v,
                                  tt=tl["attn_tt"], masked=masked, outw=outw)
    nb, tt = tl["out"]
    y = _output(ua, ub, pb, x, gate, w["w_pa"], w["w_pb"], w["w_out"], final_g,
                layer=w["layer"], nb=min(nb, bsz), tt=tt, final=final)
    shape = (bsz, outw, N_KV, HEAD_DIM)
    return y, s_new, k_new.reshape(shape), v_new.reshape(shape)


def kernel(x_prompt, x_sample, c_prompt, c_sample, state_hgrn, cache_win_k, cache_win_v, ada_w, ada_b,
           norm_g, w_in, lb_logits, hgrn_norm_g, sinks, w_branch_a, w_branch_b, w_out, final_norm_g):
    bp, tp = x_prompt.shape[0], x_prompt.shape[1]
    bs, ts = x_sample.shape[0], x_sample.shape[1]

    w_in_p = w_in.astype(BF16)
    w_pa = w_branch_a.astype(BF16)
    w_pb = w_branch_b.astype(BF16)
    w_o = w_out.astype(BF16)
    prob = jax.nn.softmax(lb_logits.astype(F32), axis=0)
    lb = jnp.cumsum(prob, axis=0) - prob[:1]
    final_g = final_norm_g.reshape(1, D_MODEL)

    mod = _modulation(jnp.concatenate([c_prompt, c_sample], axis=0), ada_w, ada_b)
    tab_p = _rope_tables(jnp.arange(tp))
    tab_s = _rope_tables(PAST_LEN + jnp.arange(ts))
    zero_state = jnp.zeros((bp, HGRN_HEADS, HGRN_DK, HGRN_DV), F32)

    hp, hs = x_prompt, x_sample
    outs = [[] for _ in range(6)]
    for l in range(DEPTH):
        w = dict(norm_g=norm_g[l].reshape(1, D_MODEL), w_in=w_in_p, hgrn_g=hgrn_norm_g[l].reshape(1, D_A),
                 sinks=sinks[l], w_pa=w_pa, w_pb=w_pb, w_out=w_o, layer=l)
        lbl = lb[l].reshape(1, D_A)
        final = l == DEPTH - 1
        hp, s_p, k_p, v_p = _layer(hp, mod[l, :bp], w, lbl, zero_state, None, None, tab_p,
                                   final=final, final_g=final_g)
        hs, s_s, k_s, v_s = _layer(hs, mod[l, bp:], w, lbl, state_hgrn[l],
                                   cache_win_k[l].reshape(bs, WINDOW, KV_W),
                                   cache_win_v[l].reshape(bs, WINDOW, KV_W), tab_s,
                                   final=final, final_g=final_g)
        for acc, val in zip(outs, (s_p, k_p, v_p, s_s, k_s, v_s)):
            acc.append(val)
    return (hp, hs) + tuple(jnp.stack(o) for o in outs)
```

```python
import functools

import jax
import jax.numpy as jnp
import numpy as np
from jax import lax
from jax.experimental import pallas as pl
from jax.experimental.pallas import tpu as pltpu

F32 = jnp.float32
BF16 = jnp.bfloat16

D_MODEL = 2048
DEPTH = 4
PAST_LEN = 4096
CHUNK = 64
SUB = 16
NSUB = CHUNK // SUB
D_A = D_MODEL // 2
HGRN_DK = 128
HGRN_HEADS = D_A // HGRN_DK
HGRN_DV = D_A // HGRN_HEADS
D_B = D_MODEL // 2
HEAD_DIM = 64
N_Q = D_B // HEAD_DIM
N_KV = N_Q // 4
GROUP = N_Q // N_KV
WINDOW = 128
N_LOOKBACK_CHUNKS = WINDOW // CHUNK
BAND = WINDOW + CHUNK
ROT_DIM = HEAD_DIM // 4
ROPE_THETA = 500000.0
ATTN_SCALE = HEAD_DIM ** -0.5
EPS = 1e-6
NEG = -1e30
LOG2E = 1.4426950408889634
N_IN = 5 * D_A + N_Q * HEAD_DIM + 2 * N_KV * HEAD_DIM + D_B + 2 * D_MODEL

LANES = 128
SUBLANES = 8
KV_W = N_KV * HEAD_DIM
VMEM_LIMIT = 58 * 1024 * 1024

N_F32 = D_A
N_B16 = N_IN - N_F32
COL_MA, COL_MB = 0, D_MODEL
COL_QA, COL_IA, COL_GA, COL_ZA = (2 * D_MODEL + i * D_A for i in range(4))
COL_QB = COL_ZA + D_A
COL_ZB = COL_QB + D_B
COL_KV = COL_ZB + D_B
PROJ_TN = 512


def _proj_tile_order():
    sizes = dict(qa=D_A, fa=D_A, ia=D_A, ga=D_A, za=D_A, qb=D_B, kv=2 * KV_W, zb=D_B, ma=D_MODEL, mb=D_MODEL)
    start, first = 0, {}
    for name, n in sizes.items():
        first[name] = start // PROJ_TN
        start += n
    order = []
    for name in ("fa", "ma", "mb", "qa", "ia", "ga", "za", "qb", "zb", "kv"):
        order += range(first[name], first[name] + sizes[name] // PROJ_TN)
    return np.asarray(order, np.int32)


def _cparams(sem):
    return pltpu.CompilerParams(dimension_semantics=sem, vmem_limit_bytes=VMEM_LIMIT)


def _mod_kernel(c_ref, w_ref, b_ref, o_ref):
    acc = jnp.dot(c_ref[...].astype(BF16), w_ref[0].astype(BF16), preferred_element_type=F32)
    o_ref[0] = acc + b_ref[0]


def _modulation(c_all, ada_w, ada_b):
    n, d3, tn = c_all.shape[0], 3 * D_MODEL, 768
    return pl.pallas_call(
        _mod_kernel,
        grid=(DEPTH, d3 // tn),
        in_specs=[pl.BlockSpec((n, D_MODEL), lambda l, j: (0, 0)),
                  pl.BlockSpec((1, D_MODEL, tn), lambda l, j: (l, 0, j)),
                  pl.BlockSpec((1, 1, tn), lambda l, j: (l, 0, j))],
        out_specs=pl.BlockSpec((1, n, tn), lambda l, j: (l, 0, j)),
        out_shape=jax.ShapeDtypeStruct((DEPTH, n, d3), F32),
        compiler_params=_cparams(("arbitrary", "arbitrary")),
        name="adaln_mod",
    )(c_all, ada_w, ada_b.reshape(DEPTH, 1, d3))


def _inproj_kernel(src_ref, x_ref, shift_ref, scale_ref, g_ref, w_ref, of_ref, ob_ref, h_ref, *,
                   nb, tt, rc, nf):
    del src_ref
    n = pl.program_id(2)

    @pl.when(n == 0)
    def _():
        g = g_ref[...]
        for b in range(nb):
            mul = 1.0 + scale_ref[b]
            add = shift_ref[b]

            def rows(r, carry, b=b, mul=mul, add=add):
                r0 = pl.multiple_of(r * rc, rc)
                x = x_ref[b, pl.ds(r0, rc), :]
                y = x * lax.rsqrt(jnp.mean(x * x, axis=-1, keepdims=True) + EPS)
                h = (y * g) * mul + add
                h_ref[pl.ds(b * tt + r0, rc), :] = h.astype(BF16)
                return carry

            lax.fori_loop(0, tt // rc, rows, 0)

    acc = jnp.dot(h_ref[...], w_ref[...], preferred_element_type=F32)

    @pl.when(n < nf)
    def _():
        of_ref[...] = acc.reshape(of_ref.shape)

    @pl.when(n >= nf)
    def _():
        ob_ref[...] = acc.astype(BF16).reshape(ob_ref.shape)


def _in_projection(x, shift, scale, norm_g, w, *, layer, nb, tt):
    bsz, t, _ = x.shape
    tn = PROJ_TN
    rc = min(tt, 128)
    nf = N_F32 // tn
    kern = functools.partial(_inproj_kernel, nb=nb, tt=tt, rc=rc, nf=nf)
    grid_spec = pltpu.PrefetchScalarGridSpec(
        num_scalar_prefetch=1,
        grid=(bsz // nb, t // tt, N_IN // tn),
        in_specs=[pl.BlockSpec((nb, tt, D_MODEL), lambda i, j, n, src: (i, j, 0)),
                  pl.BlockSpec((nb, 1, D_MODEL), lambda i, j, n, src: (i, 0, 0)),
                  pl.BlockSpec((nb, 1, D_MODEL), lambda i, j, n, src: (i, 0, 0)),
                  pl.BlockSpec((1, D_MODEL), lambda i, j, n, src: (0, 0)),
                  pl.BlockSpec((None, D_MODEL, tn), lambda i, j, n, src: (layer, 0, src[n]))],
        out_specs=[pl.BlockSpec((nb, tt, tn), lambda i, j, n, src: (i, j, jnp.minimum(n, nf - 1)),
                                pipeline_mode=pl.Buffered(1)),
                   pl.BlockSpec((nb, tt, tn), lambda i, j, n, src: (i, j, jnp.maximum(n - nf, 0)))],
        scratch_shapes=[pltpu.VMEM((nb * tt, D_MODEL), BF16)],
    )
    return pl.pallas_call(
        kern,
        grid_spec=grid_spec,
        out_shape=[jax.ShapeDtypeStruct((bsz, t, N_F32), F32),
                   jax.ShapeDtypeStruct((bsz, t, N_B16), BF16)],
        compiler_params=_cparams(("arbitrary", "arbitrary", "arbitrary")),
        name="in_projection",
    )(jnp.asarray(_proj_tile_order()), x, shift, scale, norm_g, w)


HEADS_PER_LOOP = 8
LEVELS = CHUNK.bit_length() - 1
MXU_LEVELS = 2


def _hgrn_kernel(q_ref, f_ref, i_ref, g_ref, z_ref, lb_ref, ng_ref, s0_ref, mz_ref,
                 u_ref, sout_ref, st_ref, *, n_chunks):
    t = pl.program_id(1)

    @pl.when(t == 0)
    def _():
        for h in range(HGRN_HEADS):
            st_ref[h] = s0_ref[0, h].T

    row = lax.broadcasted_iota(jnp.int32, (CHUNK, CHUNK), 0)
    col = lax.broadcasted_iota(jnp.int32, (CHUNK, CHUNK), 1)
    code = jnp.where(row > col, row ^ col, jnp.where(row == col, 0, 2 * CHUNK))
    small_masks = [jnp.right_shift(code, l) == 1 for l in range(LEVELS - MXU_LEVELS)]
    diag_mask = code == 0
    sub_row = lax.broadcasted_iota(jnp.int32, (SUBLANES, HGRN_DK), 0)
    upper_half = [(jnp.right_shift(sub_row, l) & 1) == 1 for l in range(SUBLANES.bit_length() - 1)]
    nt_dims = (((1,), (1,)), ((), ()))
    tn_dims = (((0,), (0,)), ((), ()))

    def padded(x, r0):
        parts = []
        if r0:
            parts.append(jnp.zeros((r0, HGRN_DK), F32))
        parts.append(x)
        if CHUNK - r0 - x.shape[0]:
            parts.append(jnp.zeros((CHUNK - r0 - x.shape[0], HGRN_DK), F32))
        return jnp.concatenate(parts, axis=0).astype(BF16)

    def head_chunk(h, rows):
        cols = slice(h * HGRN_DK, (h + 1) * HGRN_DK)
        lb = lb_ref[:, cols]
        a = f_ref[0, rows, cols]
        qa = q_ref[0, rows, cols].astype(F32)
        v = i_ref[0, rows, cols]
        a2 = a * LOG2E
        e = jnp.exp2(-a2)
        l1e = jnp.log(1.0 + e) * LOG2E
        log_f = jnp.log(1.0 + lb * e) * LOG2E - l1e
        k = jnp.exp2((jnp.log(1.0 - lb) * LOG2E - a2) - l1e)
        q = qa * jax.nn.sigmoid(qa)

        hi = log_f.astype(BF16)
        r1 = log_f - hi.astype(F32)
        mid = r1.astype(BF16)
        lo = (r1 - mid.astype(F32)).astype(BF16)
        zall = yield jnp.concatenate([hi, mid, lo], axis=0)
        b = zall[:CHUNK]

        def level_sums(l):
            i = LEVELS - l
            return zall[i * CHUNK:(i + 1) * CHUNK]

        st = st_ref[h]
        q_dec = (q * jnp.exp2(b)).astype(BF16)
        st16 = st.astype(BF16)
        x_cols, y_cols = [], []
        for l in range(LEVELS - 1, LEVELS - 1 - MXU_LEVELS, -1):
            half = 1 << l
            w = jnp.exp2(level_sums(l))
            for base in range(0, CHUNK, 2 * half):
                lo_rows = slice(base, base + half)
                hi_rows = slice(base + half, base + 2 * half)
                x_cols.append(padded(q[hi_rows] * w[hi_rows], base + half))
                y_cols.append(padded(k[lo_rows] * w[lo_rows], base))
        x_big, y_big = jnp.concatenate(x_cols, axis=1), jnp.concatenate(y_cols, axis=1)
        pairs = [(q.astype(BF16), k.astype(BF16))]
        for l in range(LEVELS - MXU_LEVELS):
            w = jnp.exp2(level_sums(l))
            if (1 << l) < SUBLANES:
                side = jnp.concatenate([jnp.where(upper_half[l], q[r:r + SUBLANES], k[r:r + SUBLANES])
                                        for r in range(0, CHUNK, SUBLANES)], axis=0)
            else:
                side = jnp.concatenate([q[r:r + SUBLANES] if (r >> l) & 1 else k[r:r + SUBLANES]
                                        for r in range(0, CHUNK, SUBLANES)], axis=0)
            tl = (w * side).astype(BF16)
            pairs.append((tl, tl))
        b_end = b[CHUNK - 1:CHUNK]
        k_dec = (k * jnp.exp2(b_end - b)).astype(BF16)
        st_new = st * jnp.exp2(b_end)
        yield
        o_inter = lax.dot_general(q_dec, st16, nt_dims, preferred_element_type=F32)
        a_big = lax.dot_general(x_big, y_big, nt_dims, preferred_element_type=F32)
        a_small = [lax.dot_general(x, y, nt_dims, preferred_element_type=F32) for x, y in pairs]
        st_add = lax.dot_general(v, k_dec, tn_dims, preferred_element_type=F32)
        yield
        small = jnp.where(diag_mask, a_small[0], 0.0)
        for l in range(LEVELS - MXU_LEVELS):
            small = jnp.where(small_masks[l], a_small[1 + l], small)
        a_mat = (a_big + small).astype(BF16)
        st_ref[h] = st_new + st_add
        yield
        o_intra = jnp.dot(a_mat, v, preferred_element_type=F32)
        yield
        o = o_inter + o_intra
        y = o * lax.rsqrt(jnp.mean(o * o, axis=-1, keepdims=True) + EPS) * ng_ref[:, cols]
        ga = g_ref[0, rows, cols].astype(F32)
        za = z_ref[0, rows, cols].astype(F32)
        u = y * jax.nn.sigmoid(ga) * (za * jax.nn.sigmoid(za))
        u_ref[0, rows, cols] = u.astype(BF16)

    for h0 in range(0, HGRN_HEADS, HEADS_PER_LOOP):
        def chunk(c, carry, h0=h0):
            rows = pl.ds(pl.multiple_of(c * CHUNK, CHUNK), CHUNK)
            stages = [head_chunk(h, rows) for h in range(h0, h0 + HEADS_PER_LOOP)]
            splits = [next(s) for s in stages]
            sums = [jnp.dot(mz_ref[...], jnp.concatenate(splits[i:i + 2], axis=1), preferred_element_type=F32)
                    for i in range(0, HEADS_PER_LOOP, 2)]
            for i, s in enumerate(stages):
                s.send(sums[i // 2][:, (i % 2) * HGRN_DK:(i % 2 + 1) * HGRN_DK])
            while stages:
                stages = [s for s in stages if next(s, "done") != "done"]
            return carry

        lax.fori_loop(0, n_chunks, chunk, 0)

    @pl.when(t == pl.num_programs(1) - 1)
    def _():
        for h in range(HGRN_HEADS):
            sout_ref[0, h] = st_ref[h].T


def _decay_sum_matrix():
    m = np.zeros((1 + LEVELS, CHUNK, CHUNK), np.float32)
    idx = np.arange(CHUNK)
    m[0] = idx[:, None] >= idx[None, :]
    for i, l in enumerate(range(LEVELS - 1, -1, -1)):
        half = 1 << l
        for r in range(CHUNK):
            pos = r % (2 * half)
            first_upper = r - pos + half
            if pos >= half:
                m[1 + i, r, first_upper:r + 1] = 1.0
            else:
                m[1 + i, r, r + 1:first_upper] = 1.0
    m = m.reshape(-1, CHUNK)
    return jnp.asarray(np.concatenate([m, m, m], axis=1), dtype=BF16)


def _hgrn(pf, pb, lb, ng, s0, *, tt):
    bsz, t, _ = pb.shape

    def col(c0):
        return pl.BlockSpec((1, tt, D_A), lambda b, j, c0=c0: (b, j, c0 // D_A))

    vec = pl.BlockSpec((1, D_A), lambda b, j: (0, 0))
    state = pl.BlockSpec((1, HGRN_HEADS, HGRN_DK, HGRN_DV), lambda b, j: (b, 0, 0, 0))
    mz = _decay_sum_matrix()
    return pl.pallas_call(
        functools.partial(_hgrn_kernel, n_chunks=tt // CHUNK),
        grid=(bsz, t // tt),
        in_specs=[col(COL_QA), col(0), col(COL_IA), col(COL_GA), col(COL_ZA), vec, vec, state,
                  pl.BlockSpec(mz.shape, lambda b, j: (0, 0))],
        out_specs=[pl.BlockSpec((1, tt, D_A), lambda b, j: (b, j, 0)), state],
        out_shape=[jax.ShapeDtypeStruct((bsz, t, D_A), BF16),
                   jax.ShapeDtypeStruct((bsz, HGRN_HEADS, HGRN_DK, HGRN_DV), F32)],
        scratch_shapes=[pltpu.VMEM((HGRN_HEADS, HGRN_DV, HGRN_DK), F32)],
        compiler_params=_cparams(("arbitrary", "arbitrary")),
        name="hgrn2",
    )(pb, pf, pb, pb, pb, lb, ng, s0, mz)


def _rope(x, cos, sin_lo, sin_hi):
    half = ROT_DIM // 2
    return x * cos + pltpu.roll(x, LANES - half, 1) * sin_lo + pltpu.roll(x, half, 1) * sin_hi


def _attn_kernel(sink_ref, q_ref, kv_ref, z_ref, cos_ref, slo_ref, shi_ref, ck_ref, cv_ref,
                 u_ref, ko_ref, vo_ref, kbuf, vbuf, *, tt, masked, outw):
    t = pl.program_id(1)
    last = t == pl.num_programs(1) - 1
    n_chunks = tt // CHUNK
    n_slabs = KV_W // LANES

    def store_band(buf, r0, x, lanes):
        buf[r0:r0 + x.shape[0], lanes] = x.astype(BF16)
        buf[r0:r0 + x.shape[0], slice(KV_W + lanes.start, KV_W + lanes.stop)] = (
            pltpu.roll(x, HEAD_DIM, 1).astype(BF16))

    @pl.when(t == 0)
    def _():
        for j in range(n_slabs):
            lanes = slice(j * LANES, (j + 1) * LANES)
            store_band(kbuf, 0, ck_ref[0, :, lanes], lanes)
            store_band(vbuf, 0, cv_ref[0, :, lanes], lanes)

    if tt >= WINDOW:
        @pl.when(t > 0)
        def _():
            kbuf[0:WINDOW, :] = kbuf[tt:tt + WINDOW, :]
            vbuf[0:WINDOW, :] = vbuf[tt:tt + WINDOW, :]

    for j in range(n_slabs):
        lanes = slice(j * LANES, (j + 1) * LANES)
        kr = _rope(kv_ref[0, :, lanes].astype(F32), cos_ref[...], slo_ref[...], shi_ref[...])
        store_band(kbuf, WINDOW, kr, lanes)
        store_band(vbuf, WINDOW, kv_ref[0, :, slice(KV_W + lanes.start, KV_W + lanes.stop)].astype(F32), lanes)

        @pl.when(last)
        def _(kr=kr, lanes=lanes):
            ko_ref[0, :, lanes] = kr[tt - outw:, :]

    @pl.when(last)
    def _():
        vo_ref[0] = kv_ref[0, tt - outw:tt, KV_W:2 * KV_W].astype(F32)

    lane = lax.broadcasted_iota(jnp.int32, (CHUNK, LANES), 1)
    low = lane < HEAD_DIM
    heads_per_slab = LANES // HEAD_DIM * GROUP

    def kv_in_low_half(p):
        return (2 * p) // GROUP == 0

    def matched_head(p):
        return 0 if kv_in_low_half(p) else 1

    sink_rows = []
    for j in range(n_slabs):
        order = ([heads_per_slab * j + 2 * p + matched_head(p) for p in range(GROUP)]
                 + [heads_per_slab * j + 2 * p + 1 - matched_head(p) for p in range(GROUP)])
        sink_rows.append(jnp.concatenate([jnp.full((1, CHUNK), sink_ref[h], F32) for h in order], axis=1))
    nt_dims = (((1,), (1,)), ((), ()))
    tn_dims = (((0,), (0,)), ((), ()))
    half_cols = GROUP * CHUNK

    def chunk(c, carry):
        r0 = pl.multiple_of(c * CHUNK, CHUNK)
        rows = pl.ds(r0, CHUNK)
        band = pl.ds(r0, BAND)
        cq, sl, sh = cos_ref[rows, :], slo_ref[rows, :], shi_ref[rows, :]
        seq_chunk = t * n_chunks + c

        def slab_chunk(j):
            plain = slice(j * LANES, (j + 1) * LANES)
            swapped = slice(KV_W + j * LANES, KV_W + (j + 1) * LANES)
            k_plain, k_swap = kbuf[band, plain], kbuf[band, swapped]
            v_plain, v_swap = vbuf[band, plain], vbuf[band, swapped]
            q_plain, q_swap = [], []
            for p in range(GROUP):
                slab = GROUP * j + p
                x = _rope(q_ref[0, rows, slab * LANES:(slab + 1) * LANES].astype(F32), cq, sl, sh) * ATTN_SCALE
                halves = [jnp.where(low, x, 0.0).astype(BF16), jnp.where(low, 0.0, x).astype(BF16)]
                q_plain.append(halves[matched_head(p)])
                q_swap.append(halves[1 - matched_head(p)])
            q_plain = jnp.concatenate(q_plain, axis=0)
            q_swap = jnp.concatenate(q_swap, axis=0)
            yield
            s = jnp.concatenate([lax.dot_general(k_plain, q_plain, nt_dims, preferred_element_type=F32),
                                 lax.dot_general(k_swap, q_swap, nt_dims, preferred_element_type=F32)],
                                axis=1)
            yield
            if masked:
                blocks = [jnp.where(seq_chunk >= N_LOOKBACK_CHUNKS - i, s[i * CHUNK:(i + 1) * CHUNK], NEG)
                          for i in range(N_LOOKBACK_CHUNKS)]
                s = jnp.concatenate(blocks + [s[WINDOW:]], axis=0)
            sink = sink_rows[j]
            m = jnp.maximum(jnp.max(s, axis=0, keepdims=True), sink)
            pr = jnp.exp(s - m)
            inv = 1.0 / (jnp.sum(pr, axis=0, keepdims=True) + jnp.exp(sink - m))
            pr16 = pr.astype(BF16)
            yield
            o_plain = lax.dot_general(v_plain, pr16[:, :half_cols], tn_dims, preferred_element_type=F32)
            o_swap = lax.dot_general(v_swap, pr16[:, half_cols:], tn_dims, preferred_element_type=F32)
            yield
            o_plain = o_plain * inv[:, :half_cols]
            o_swap = o_swap * inv[:, half_cols:]
            for pair in range(GROUP // 2):
                t_plain = o_plain[:, pair * LANES:(pair + 1) * LANES].T
                t_swap = o_swap[:, pair * LANES:(pair + 1) * LANES].T
                for i in range(2):
                    p = 2 * pair + i
                    slab = GROUP * j + p
                    qrows = slice(i * CHUNK, (i + 1) * CHUNK)
                    if matched_head(p) == 0:
                        ob = jnp.where(low, t_plain[qrows], t_swap[qrows])
                    else:
                        ob = jnp.where(low, t_swap[qrows], t_plain[qrows])
                    zb = z_ref[0, rows, slab * LANES:(slab + 1) * LANES].astype(F32)
                    u_ref[0, rows, slab * LANES:(slab + 1) * LANES] = (
                        ob * (zb * jax.nn.sigmoid(zb))).astype(BF16)

        stages = [slab_chunk(j) for j in range(n_slabs)]
        while stages:
            stages = [s for s in stages if next(s, "done") != "done"]
        return carry

    lax.fori_loop(0, n_chunks, chunk, 0)


def _rope_tables(pos):
    half = ROT_DIM // 2
    inv = (ROPE_THETA ** (-np.arange(0, ROT_DIM, 2) / ROT_DIM)).astype(np.float32)
    ang = pos.astype(F32)[:, None] * inv[None, :]
    cos, sin = jnp.cos(ang), jnp.sin(ang)
    n = pos.shape[0]
    rest = HEAD_DIM - ROT_DIM
    cos_h = jnp.concatenate([cos, cos, jnp.ones((n, rest), F32)], axis=1)
    slo_h = jnp.concatenate([-sin, jnp.zeros((n, half + rest), F32)], axis=1)
    shi_h = jnp.concatenate([jnp.zeros((n, half), F32), sin, jnp.zeros((n, rest), F32)], axis=1)
    reps = LANES // HEAD_DIM
    return tuple(jnp.tile(a, (1, reps)) for a in (cos_h, slo_h, shi_h))


def _attention(pb, sinks, tables, cache_k, cache_v, *, tt, masked, outw):
    bsz, t, _ = pb.shape
    tab = pl.BlockSpec((tt, LANES), lambda b, j, s: (j, 0))
    cache = pl.BlockSpec((1, WINDOW, KV_W), lambda b, j, s: (b, 0, 0))
    win = pl.BlockSpec((1, outw, KV_W), lambda b, j, s: (b, 0, 0))
    grid_spec = pltpu.PrefetchScalarGridSpec(
        num_scalar_prefetch=1,
        grid=(bsz, t // tt),
        in_specs=[pl.BlockSpec((1, tt, D_B), lambda b, j, s: (b, j, COL_QB // D_B)),
                  pl.BlockSpec((1, tt, 2 * KV_W), lambda b, j, s: (b, j, COL_KV // (2 * KV_W))),
                  pl.BlockSpec((1, tt, D_B), lambda b, j, s: (b, j, COL_ZB // D_B)),
                  tab, tab, tab, cache, cache],
        out_specs=[pl.BlockSpec((1, tt, D_B), lambda b, j, s: (b, j, 0)), win, win],
        scratch_shapes=[pltpu.VMEM((WINDOW + tt, 2 * KV_W), BF16), pltpu.VMEM((WINDOW + tt, 2 * KV_W), BF16)],
    )
    return pl.pallas_call(
        functools.partial(_attn_kernel, tt=tt, masked=masked, outw=outw),
        grid_spec=grid_spec,
        out_shape=[jax.ShapeDtypeStruct((bsz, t, D_B), BF16),
                   jax.ShapeDtypeStruct((bsz, outw, KV_W), F32),
                   jax.ShapeDtypeStruct((bsz, outw, KV_W), F32)],
        compiler_params=_cparams(("arbitrary", "arbitrary")),
        name="swa_attention",
    )(sinks, pb, pb, pb, *tables, cache_k, cache_v)


def _out_kernel(ua_ref, ub_ref, ma_ref, mb_ref, x_ref, gate_ref, wpa_ref, wpb_ref, wout_ref, fg_ref,
                y_ref, *, nb, tt, final):
    m = nb * tt
    pa = jnp.dot(ua_ref[...].reshape(m, D_A), wpa_ref[...], preferred_element_type=F32)
    pb = jnp.dot(ub_ref[...].reshape(m, D_B), wpb_ref[...], preferred_element_type=F32)
    ma = ma_ref[...].reshape(m, D_MODEL).astype(F32)
    mb = mb_ref[...].reshape(m, D_MODEL).astype(F32)
    merged = jax.nn.sigmoid(ma) * pa + jax.nn.sigmoid(mb) * pb
    o = jnp.dot(merged.astype(BF16), wout_ref[...], preferred_element_type=F32)
    y = x_ref[...] + gate_ref[...] * o.reshape(nb, tt, D_MODEL)
    if final:
        y = (y * lax.rsqrt(jnp.mean(y * y, axis=-1, keepdims=True) + EPS)) * fg_ref[...]
    y_ref[...] = y


def _output(ua, ub, pb, x, gate, wpa, wpb, wout, final_g, *, layer, nb, tt, final):
    bsz, t, _ = x.shape
    tok = lambda i, j: (i, j, 0)
    const = lambda i, j: (0, 0)
    of_layer = lambda i, j: (layer, 0, 0)
    single = pl.Buffered(1)
    return pl.pallas_call(
        functools.partial(_out_kernel, nb=nb, tt=tt, final=final),
        grid=(bsz // nb, t // tt),
        in_specs=[pl.BlockSpec((nb, tt, D_A), tok),
                  pl.BlockSpec((nb, tt, D_B), tok),
                  pl.BlockSpec((nb, tt, D_MODEL), lambda i, j: (i, j, COL_MA // D_MODEL)),
                  pl.BlockSpec((nb, tt, D_MODEL), lambda i, j: (i, j, COL_MB // D_MODEL)),
                  pl.BlockSpec((nb, tt, D_MODEL), tok),
                  pl.BlockSpec((nb, 1, D_MODEL), lambda i, j: (i, 0, 0)),
                  pl.BlockSpec((None, D_A, D_MODEL), of_layer, pipeline_mode=single),
                  pl.BlockSpec((None, D_B, D_MODEL), of_layer, pipeline_mode=single),
                  pl.BlockSpec((None, D_MODEL, D_MODEL), of_layer, pipeline_mode=single),
                  pl.BlockSpec((1, D_MODEL), const)],
        out_specs=pl.BlockSpec((nb, tt, D_MODEL), tok),
        out_shape=jax.ShapeDtypeStruct((bsz, t, D_MODEL), F32),
        compiler_params=_cparams(("arbitrary", "arbitrary")),
        name="merge_output",
    )(ua, ub, pb, pb, x, gate, wpa, wpb, wout, final_g)


def _tiles(t):
    if t >= 1024:
        return dict(proj=(1, min(t, 2048)), out=(1, 512), hgrn_tt=min(t, 1024), attn_tt=min(t, 1024))
    return dict(proj=(16, t), out=(4, t), hgrn_tt=t, attn_tt=t)


def _layer(x, mod, w, lb, s0, cache_k, cache_v, tables, *, final, final_g):
    bsz, t, _ = x.shape
    tl = _tiles(t)
    shift, scale, gate = (mod[:, None, i * D_MODEL:(i + 1) * D_MODEL] for i in range(3))
    nb, tt = tl["proj"]
    pf, pb = _in_projection(x, shift, scale, w["norm_g"], w["w_in"], layer=w["layer"], nb=min(nb, bsz), tt=tt)
    ua, s_new = _hgrn(pf, pb, lb, w["hgrn_g"], s0, tt=tl["hgrn_tt"])
    masked = cache_k is None
    if masked:
        cache_k = jnp.zeros((bsz, WINDOW, KV_W), F32)
        cache_v = cache_k
    outw = min(t, WINDOW)
    ub, k_new, v_new = _attention(pb, w["sinks"], tables, cache_k, cache_v,
                                  tt=tl["attn_tt"], masked=masked, outw=outw)
    nb, tt = tl["out"]
    y = _output(ua, ub, pb, x, gate, w["w_pa"], w["w_pb"], w["w_out"], final_g,
                layer=w["layer"], nb=min(nb, bsz), tt=tt, final=final)
    shape = (bsz, outw, N_KV, HEAD_DIM)
    return y, s_new, k_new.reshape(shape), v_new.reshape(shape)


def kernel(x_prompt, x_sample, c_prompt, c_sample, state_hgrn, cache_win_k, cache_win_v, ada_w, ada_b,
           norm_g, w_in, lb_logits, hgrn_norm_g, sinks, w_branch_a, w_branch_b, w_out, final_norm_g):
    bp, tp = x_prompt.shape[0], x_prompt.shape[1]
    bs, ts = x_sample.shape[0], x_sample.shape[1]

    w_in_p = w_in.astype(BF16)
    w_pa = w_branch_a.astype(BF16)
    w_pb = w_branch_b.astype(BF16)
    w_o = w_out.astype(BF16)
    prob = jax.nn.softmax(lb_logits.astype(F32), axis=0)
    lb = jnp.cumsum(prob, axis=0) - prob[:1]
    final_g = final_norm_g.reshape(1, D_MODEL)

    mod = _modulation(jnp.concatenate([c_prompt, c_sample], axis=0), ada_w, ada_b)
    tab_p = _rope_tables(jnp.arange(tp))
    tab_s = _rope_tables(PAST_LEN + jnp.arange(ts))
    zero_state = jnp.zeros((bp, HGRN_HEADS, HGRN_DK, HGRN_DV), F32)

    hp, hs = x_prompt, x_sample
    outs = [[] for _ in range(6)]
    for l in range(DEPTH):
        w = dict(norm_g=norm_g[l].reshape(1, D_MODEL), w_in=w_in_p, hgrn_g=hgrn_norm_g[l].reshape(1, D_A),
                 sinks=sinks[l], w_pa=w_pa, w_pb=w_pb, w_out=w_o, layer=l)
        lbl = lb[l].reshape(1, D_A)
        final = l == DEPTH - 1
        hp, s_p, k_p, v_p = _layer(hp, mod[l, :bp], w, lbl, zero_state, None, None, tab_p,
                                   final=final, final_g=final_g)
        hs, s_s, k_s, v_s = _layer(hs, mod[l, bp:], w, lbl, state_hgrn[l],
                                   cache_win_k[l].reshape(bs, WINDOW, KV_W),
                                   cache_win_v[l].reshape(bs, WINDOW, KV_W), tab_s,
                                   final=final, final_g=final_g)
        for acc, val in zip(outs, (s_p, k_p, v_p, s_s, k_s, v_s)):
            acc.append(val)
    return (hp, hs) + tuple(jnp.stack(o) for o in outs)
```

```python
import functools

import jax
import jax.numpy as jnp
import numpy as np
from jax import lax
from jax.experimental import pallas as pl
from jax.experimental.pallas import tpu as pltpu

F32 = jnp.float32
BF16 = jnp.bfloat16

D_MODEL = 2048
DEPTH = 4
PAST_LEN = 4096
CHUNK = 64
D_A = D_MODEL // 2
HGRN_DK = 128
HGRN_HEADS = D_A // HGRN_DK
HGRN_DV = D_A // HGRN_HEADS
D_B = D_MODEL // 2
HEAD_DIM = 64
N_Q = D_B // HEAD_DIM
N_KV = N_Q // 4
GROUP = N_Q // N_KV
WINDOW = 128
N_LOOKBACK_CHUNKS = WINDOW // CHUNK
BAND = WINDOW + CHUNK
ROT_DIM = HEAD_DIM // 4
ROPE_THETA = 500000.0
ATTN_SCALE = HEAD_DIM ** -0.5
EPS = 1e-6
NEG = -1e30
LOG2E = 1.4426950408889634
N_IN = 5 * D_A + N_Q * HEAD_DIM + 2 * N_KV * HEAD_DIM + D_B + 2 * D_MODEL

LANES = 128
SUBLANES = 8
KV_W = N_KV * HEAD_DIM
VMEM_LIMIT = 58 * 1024 * 1024

N_F32 = D_A
N_B16 = N_IN - N_F32
COL_MA, COL_MB = 0, D_MODEL
COL_QA, COL_IA, COL_GA, COL_ZA = (2 * D_MODEL + i * D_A for i in range(4))
COL_QB = COL_ZA + D_A
COL_ZB = COL_QB + D_B
COL_KV = COL_ZB + D_B
PROJ_TN = 512
PROJ_ROWS = 512


def _proj_tile_order():
    sizes = dict(qa=D_A, fa=D_A, ia=D_A, ga=D_A, za=D_A, qb=D_B, kv=2 * KV_W, zb=D_B, ma=D_MODEL, mb=D_MODEL)
    start, first = 0, {}
    for name, n in sizes.items():
        first[name] = start // PROJ_TN
        start += n
    order = []
    for name in ("fa", "ma", "mb", "qa", "ia", "ga", "za", "qb", "zb", "kv"):
        order += range(first[name], first[name] + sizes[name] // PROJ_TN)
    return np.asarray(order, np.int32)


def _cparams(sem):
    return pltpu.CompilerParams(dimension_semantics=sem, vmem_limit_bytes=VMEM_LIMIT)


def _mod_kernel(c_ref, w_ref, b_ref, o_ref):
    acc = jnp.dot(c_ref[...].astype(BF16), w_ref[0].astype(BF16), preferred_element_type=F32)
    o_ref[0] = acc + b_ref[0]


def _modulation(c_all, ada_w, ada_b):
    n, d3, tn = c_all.shape[0], 3 * D_MODEL, 768
    return pl.pallas_call(
        _mod_kernel,
        grid=(DEPTH, d3 // tn),
        in_specs=[pl.BlockSpec((n, D_MODEL), lambda l, j: (0, 0)),
                  pl.BlockSpec((1, D_MODEL, tn), lambda l, j: (l, 0, j)),
                  pl.BlockSpec((1, 1, tn), lambda l, j: (l, 0, j))],
        out_specs=pl.BlockSpec((1, n, tn), lambda l, j: (l, 0, j)),
        out_shape=jax.ShapeDtypeStruct((DEPTH, n, d3), F32),
        compiler_params=_cparams(("arbitrary", "arbitrary")),
        name="adaln_mod",
    )(c_all, ada_w, ada_b.reshape(DEPTH, 1, d3))


def _inproj_kernel(src_ref, x_ref, shift_ref, scale_ref, g_ref, w_ref, of_ref, ob_ref, h_ref, *,
                   nb, tt, rc, nf):
    del src_ref
    n = pl.program_id(2)

    @pl.when(n == 0)
    def _():
        g = g_ref[...]
        for b in range(nb):
            mul = 1.0 + scale_ref[b]
            add = shift_ref[b]

            def rows(r, carry, b=b, mul=mul, add=add):
                r0 = pl.multiple_of(r * rc, rc)
                x = x_ref[b, pl.ds(r0, rc), :]
                y = x * lax.rsqrt(jnp.mean(x * x, axis=-1, keepdims=True) + EPS)
                h = (y * g) * mul + add
                h_ref[pl.ds(b * tt + r0, rc), :] = h.astype(BF16)
                return carry

            lax.fori_loop(0, tt // rc, rows, 0)

    m = nb * tt
    step = min(m, PROJ_ROWS)

    def produce(o_ref):
        for r in range(0, m, step):
            acc = jnp.dot(h_ref[r:r + step, :], w_ref[...], preferred_element_type=F32).astype(o_ref.dtype)
            if nb == 1:
                o_ref[0, r:r + step, :] = acc
            else:
                o_ref[r // tt:(r + step) // tt] = acc.reshape(step // tt, tt, acc.shape[-1])

    pl.when(n < nf)(functools.partial(produce, of_ref))
    pl.when(n >= nf)(functools.partial(produce, ob_ref))


def _in_projection(x, shift, scale, norm_g, w, *, layer, nb, tt):
    bsz, t, _ = x.shape
    tn = PROJ_TN
    rc = min(tt, 128)
    nf = N_F32 // tn
    kern = functools.partial(_inproj_kernel, nb=nb, tt=tt, rc=rc, nf=nf)
    grid_spec = pltpu.PrefetchScalarGridSpec(
        num_scalar_prefetch=1,
        grid=(bsz // nb, t // tt, N_IN // tn),
        in_specs=[pl.BlockSpec((nb, tt, D_MODEL), lambda i, j, n, src: (i, j, 0)),
                  pl.BlockSpec((nb, 1, D_MODEL), lambda i, j, n, src: (i, 0, 0)),
                  pl.BlockSpec((nb, 1, D_MODEL), lambda i, j, n, src: (i, 0, 0)),
                  pl.BlockSpec((1, D_MODEL), lambda i, j, n, src: (0, 0)),
                  pl.BlockSpec((None, D_MODEL, tn), lambda i, j, n, src: (layer, 0, src[n]))],
        out_specs=[pl.BlockSpec((nb, tt, tn), lambda i, j, n, src: (i, j, jnp.minimum(n, nf - 1)),
                                pipeline_mode=pl.Buffered(1)),
                   pl.BlockSpec((nb, tt, tn), lambda i, j, n, src: (i, j, jnp.maximum(n - nf, 0)))],
        scratch_shapes=[pltpu.VMEM((nb * tt, D_MODEL), BF16)],
    )
    return pl.pallas_call(
        kern,
        grid_spec=grid_spec,
        out_shape=[jax.ShapeDtypeStruct((bsz, t, N_F32), F32),
                   jax.ShapeDtypeStruct((bsz, t, N_B16), BF16)],
        compiler_params=_cparams(("arbitrary", "arbitrary", "arbitrary")),
        name="in_projection",
    )(jnp.asarray(_proj_tile_order()), x, shift, scale, norm_g, w)


HEADS_PER_LOOP = 8
LEVELS = CHUNK.bit_length() - 1
MXU_LEVELS = 2


def _hgrn_kernel(q_ref, f_ref, i_ref, g_ref, z_ref, lb_ref, ng_ref, s0_ref, mz_ref,
                 u_ref, sout_ref, st_ref, *, n_chunks):
    t = pl.program_id(1)

    @pl.when(t == 0)
    def _():
        for h in range(HGRN_HEADS):
            st_ref[h] = s0_ref[0, h].T

    row = lax.broadcasted_iota(jnp.int32, (CHUNK, CHUNK), 0)
    col = lax.broadcasted_iota(jnp.int32, (CHUNK, CHUNK), 1)
    code = jnp.where(row > col, row ^ col, jnp.where(row == col, 0, 2 * CHUNK))
    small_masks = [jnp.right_shift(code, l) == 1 for l in range(LEVELS - MXU_LEVELS)]
    diag_mask = code == 0
    sub_row = lax.broadcasted_iota(jnp.int32, (SUBLANES, HGRN_DK), 0)
    upper_half = [(jnp.right_shift(sub_row, l) & 1) == 1 for l in range(SUBLANES.bit_length() - 1)]
    nt_dims = (((1,), (1,)), ((), ()))
    tn_dims = (((0,), (0,)), ((), ()))

    def padded(x, r0):
        parts = []
        if r0:
            parts.append(jnp.zeros((r0, HGRN_DK), F32))
        parts.append(x)
        if CHUNK - r0 - x.shape[0]:
            parts.append(jnp.zeros((CHUNK - r0 - x.shape[0], HGRN_DK), F32))
        return jnp.concatenate(parts, axis=0).astype(BF16)

    def head_chunk(h, rows):
        cols = slice(h * HGRN_DK, (h + 1) * HGRN_DK)
        lb = lb_ref[:, cols]
        a = f_ref[0, rows, cols]
        qa = q_ref[0, rows, cols].astype(F32)
        v = i_ref[0, rows, cols]
        a2 = a * LOG2E
        e = jnp.exp2(-a2)
        l1e = jnp.log(1.0 + e) * LOG2E
        log_f = jnp.log(1.0 + lb * e) * LOG2E - l1e
        k = jnp.exp2((jnp.log(1.0 - lb) * LOG2E - a2) - l1e)
        q = qa * jax.nn.sigmoid(qa)

        hi = log_f.astype(BF16)
        r1 = log_f - hi.astype(F32)
        mid = r1.astype(BF16)
        lo = (r1 - mid.astype(F32)).astype(BF16)
        zall = yield jnp.concatenate([hi, mid, lo], axis=0)
        b = zall[:CHUNK]

        def level_sums(l):
            i = LEVELS - l
            return zall[i * CHUNK:(i + 1) * CHUNK]

        st = st_ref[h]
        q_dec = (q * jnp.exp2(b)).astype(BF16)
        st16 = st.astype(BF16)
        x_cols, y_cols = [], []
        for l in range(LEVELS - 1, LEVELS - 1 - MXU_LEVELS, -1):
            half = 1 << l
            w = jnp.exp2(level_sums(l))
            for base in range(0, CHUNK, 2 * half):
                lo_rows = slice(base, base + half)
                hi_rows = slice(base + half, base + 2 * half)
                x_cols.append(padded(q[hi_rows] * w[hi_rows], base + half))
                y_cols.append(padded(k[lo_rows] * w[lo_rows], base))
        x_big, y_big = jnp.concatenate(x_cols, axis=1), jnp.concatenate(y_cols, axis=1)
        pairs = [(q.astype(BF16), k.astype(BF16))]
        for l in range(LEVELS - MXU_LEVELS):
            w = jnp.exp2(level_sums(l))
            if (1 << l) < SUBLANES:
                side = jnp.concatenate([jnp.where(upper_half[l], q[r:r + SUBLANES], k[r:r + SUBLANES])
                                        for r in range(0, CHUNK, SUBLANES)], axis=0)
            else:
                side = jnp.concatenate([q[r:r + SUBLANES] if (r >> l) & 1 else k[r:r + SUBLANES]
                                        for r in range(0, CHUNK, SUBLANES)], axis=0)
            tl = (w * side).astype(BF16)
            pairs.append((tl, tl))
        b_end = b[CHUNK - 1:CHUNK]
        k_dec = (k * jnp.exp2(b_end - b)).astype(BF16)
        st_new = st * jnp.exp2(b_end)
        yield
        o_inter = lax.dot_general(q_dec, st16, nt_dims, preferred_element_type=F32)
        a_big = lax.dot_general(x_big, y_big, nt_dims, preferred_element_type=F32)
        a_small = [lax.dot_general(x, y, nt_dims, preferred_element_type=F32) for x, y in pairs]
        st_add = lax.dot_general(v, k_dec, tn_dims, preferred_element_type=F32)
        yield
        small = jnp.where(diag_mask, a_small[0], 0.0)
        for l in range(LEVELS - MXU_LEVELS):
            small = jnp.where(small_masks[l], a_small[1 + l], small)
        a_mat = (a_big + small).astype(BF16)
        st_ref[h] = st_new + st_add
        yield
        o_intra = jnp.dot(a_mat, v, preferred_element_type=F32)
        yield
        o = o_inter + o_intra
        y = o * lax.rsqrt(jnp.mean(o * o, axis=-1, keepdims=True) + EPS) * ng_ref[:, cols]
        ga = g_ref[0, rows, cols].astype(F32)
        za = z_ref[0, rows, cols].astype(F32)
        u = y * jax.nn.sigmoid(ga) * (za * jax.nn.sigmoid(za))
        u_ref[0, rows, cols] = u.astype(BF16)

    for h0 in range(0, HGRN_HEADS, HEADS_PER_LOOP):
        def chunk(c, carry, h0=h0):
            rows = pl.ds(pl.multiple_of(c * CHUNK, CHUNK), CHUNK)
            stages = [head_chunk(h, rows) for h in range(h0, h0 + HEADS_PER_LOOP)]
            splits = [next(s) for s in stages]
            sums = [jnp.dot(mz_ref[...], jnp.concatenate(splits[i:i + 2], axis=1), preferred_element_type=F32)
                    for i in range(0, HEADS_PER_LOOP, 2)]
            for i, s in enumerate(stages):
                s.send(sums[i // 2][:, (i % 2) * HGRN_DK:(i % 2 + 1) * HGRN_DK])
            while stages:
                stages = [s for s in stages if next(s, "done") != "done"]
            return carry

        lax.fori_loop(0, n_chunks, chunk, 0)

    @pl.when(t == pl.num_programs(1) - 1)
    def _():
        for h in range(HGRN_HEADS):
            sout_ref[0, h] = st_ref[h].T


def _decay_sum_matrix():
    m = np.zeros((1 + LEVELS, CHUNK, CHUNK), np.float32)
    idx = np.arange(CHUNK)
    m[0] = idx[:, None] >= idx[None, :]
    for i, l in enumerate(range(LEVELS - 1, -1, -1)):
        half = 1 << l
        for r in range(CHUNK):
            pos = r % (2 * half)
            first_upper = r - pos + half
            if pos >= half:
                m[1 + i, r, first_upper:r + 1] = 1.0
            else:
                m[1 + i, r, r + 1:first_upper] = 1.0
    m = m.reshape(-1, CHUNK)
    return jnp.asarray(np.concatenate([m, m, m], axis=1), dtype=BF16)


def _hgrn(pf, pb, lb, ng, s0, *, tt):
    bsz, t, _ = pb.shape

    def col(c0):
        return pl.BlockSpec((1, tt, D_A), lambda b, j, c0=c0: (b, j, c0 // D_A))

    vec = pl.BlockSpec((1, D_A), lambda b, j: (0, 0))
    state = pl.BlockSpec((1, HGRN_HEADS, HGRN_DK, HGRN_DV), lambda b, j: (b, 0, 0, 0))
    mz = _decay_sum_matrix()
    return pl.pallas_call(
        functools.partial(_hgrn_kernel, n_chunks=tt // CHUNK),
        grid=(bsz, t // tt),
        in_specs=[col(COL_QA), col(0), col(COL_IA), col(COL_GA), col(COL_ZA), vec, vec, state,
                  pl.BlockSpec(mz.shape, lambda b, j: (0, 0))],
        out_specs=[pl.BlockSpec((1, tt, D_A), lambda b, j: (b, j, 0)), state],
        out_shape=[jax.ShapeDtypeStruct((bsz, t, D_A), BF16),
                   jax.ShapeDtypeStruct((bsz, HGRN_HEADS, HGRN_DK, HGRN_DV), F32)],
        scratch_shapes=[pltpu.VMEM((HGRN_HEADS, HGRN_DV, HGRN_DK), F32)],
        compiler_params=_cparams(("arbitrary", "arbitrary")),
        name="hgrn2",
    )(pb, pf, pb, pb, pb, lb, ng, s0, mz)


def _rope(x, cos, sin_lo, sin_hi):
    half = ROT_DIM // 2
    return x * cos + pltpu.roll(x, LANES - half, 1) * sin_lo + pltpu.roll(x, half, 1) * sin_hi


def _attn_kernel(sink_ref, q_ref, kv_ref, z_ref, cos_ref, slo_ref, shi_ref, ck_ref, cv_ref,
                 u_ref, ko_ref, vo_ref, kbuf, vbuf, *, tt, masked, outw):
    t = pl.program_id(1)
    last = t == pl.num_programs(1) - 1
    n_chunks = tt // CHUNK
    n_slabs = KV_W // LANES

    def store_band(buf, r0, x, lanes):
        buf[r0:r0 + x.shape[0], lanes] = x.astype(BF16)
        buf[r0:r0 + x.shape[0], slice(KV_W + lanes.start, KV_W + lanes.stop)] = (
            pltpu.roll(x, HEAD_DIM, 1).astype(BF16))

    @pl.when(t == 0)
    def _():
        for j in range(n_slabs):
            lanes = slice(j * LANES, (j + 1) * LANES)
            store_band(kbuf, 0, ck_ref[0, :, lanes], lanes)
            store_band(vbuf, 0, cv_ref[0, :, lanes], lanes)

    if tt >= WINDOW:
        @pl.when(t > 0)
        def _():
            kbuf[0:WINDOW, :] = kbuf[tt:tt + WINDOW, :]
            vbuf[0:WINDOW, :] = vbuf[tt:tt + WINDOW, :]

    for j in range(n_slabs):
        lanes = slice(j * LANES, (j + 1) * LANES)
        kr = _rope(kv_ref[0, :, lanes].astype(F32), cos_ref[...], slo_ref[...], shi_ref[...])
        store_band(kbuf, WINDOW, kr, lanes)
        store_band(vbuf, WINDOW, kv_ref[0, :, slice(KV_W + lanes.start, KV_W + lanes.stop)].astype(F32), lanes)

        @pl.when(last)
        def _(kr=kr, lanes=lanes):
            ko_ref[0, :, lanes] = kr[tt - outw:, :]

    @pl.when(last)
    def _():
        vo_ref[0] = kv_ref[0, tt - outw:tt, KV_W:2 * KV_W].astype(F32)

    lane = lax.broadcasted_iota(jnp.int32, (CHUNK, LANES), 1)
    low = lane < HEAD_DIM
    heads_per_slab = LANES // HEAD_DIM * GROUP

    def kv_in_low_half(p):
        return (2 * p) // GROUP == 0

    def matched_head(p):
        return 0 if kv_in_low_half(p) else 1

    sink_rows = []
    for j in range(n_slabs):
        order = ([heads_per_slab * j + 2 * p + matched_head(p) for p in range(GROUP)]
                 + [heads_per_slab * j + 2 * p + 1 - matched_head(p) for p in range(GROUP)])
        sink_rows.append(jnp.concatenate([jnp.full((1, CHUNK), sink_ref[h], F32) for h in order], axis=1))
    nt_dims = (((1,), (1,)), ((), ()))
    tn_dims = (((0,), (0,)), ((), ()))
    half_cols = GROUP * CHUNK

    def chunk(c, carry):
        r0 = pl.multiple_of(c * CHUNK, CHUNK)
        rows = pl.ds(r0, CHUNK)
        band = pl.ds(r0, BAND)
        cq, sl, sh = cos_ref[rows, :], slo_ref[rows, :], shi_ref[rows, :]
        seq_chunk = t * n_chunks + c

        def slab_chunk(j):
            plain = slice(j * LANES, (j + 1) * LANES)
            swapped = slice(KV_W + j * LANES, KV_W + (j + 1) * LANES)
            k_plain, k_swap = kbuf[band, plain], kbuf[band, swapped]
            v_plain, v_swap = vbuf[band, plain], vbuf[band, swapped]
            q_plain, q_swap = [], []
            for p in range(GROUP):
                slab = GROUP * j + p
                x = _rope(q_ref[0, rows, slab * LANES:(slab + 1) * LANES].astype(F32), cq, sl, sh) * ATTN_SCALE
                halves = [jnp.where(low, x, 0.0).astype(BF16), jnp.where(low, 0.0, x).astype(BF16)]
                q_plain.append(halves[matched_head(p)])
                q_swap.append(halves[1 - matched_head(p)])
            q_plain = jnp.concatenate(q_plain, axis=0)
            q_swap = jnp.concatenate(q_swap, axis=0)
            yield
            s = jnp.concatenate([lax.dot_general(k_plain, q_plain, nt_dims, preferred_element_type=F32),
                                 lax.dot_general(k_swap, q_swap, nt_dims, preferred_element_type=F32)],
                                axis=1)
            yield
            if masked:
                blocks = [jnp.where(seq_chunk >= N_LOOKBACK_CHUNKS - i, s[i * CHUNK:(i + 1) * CHUNK], NEG)
                          for i in range(N_LOOKBACK_CHUNKS)]
                s = jnp.concatenate(blocks + [s[WINDOW:]], axis=0)
            sink = sink_rows[j]
            m = jnp.maximum(jnp.max(s, axis=0, keepdims=True), sink)
            pr = jnp.exp(s - m)
            inv = 1.0 / (jnp.sum(pr, axis=0, keepdims=True) + jnp.exp(sink - m))
            pr16 = pr.astype(BF16)
            yield
            o_plain = lax.dot_general(v_plain, pr16[:, :half_cols], tn_dims, preferred_element_type=F32)
            o_swap = lax.dot_general(v_swap, pr16[:, half_cols:], tn_dims, preferred_element_type=F32)
            yield
            o_plain = o_plain * inv[:, :half_cols]
            o_swap = o_swap * inv[:, half_cols:]
            for pair in range(GROUP // 2):
                t_plain = o_plain[:, pair * LANES:(pair + 1) * LANES].T
                t_swap = o_swap[:, pair * LANES:(pair + 1) * LANES].T
                for i in range(2):
                    p = 2 * pair + i
                    slab = GROUP * j + p
                    qrows = slice(i * CHUNK, (i + 1) * CHUNK)
                    if matched_head(p) == 0:
                        ob = jnp.where(low, t_plain[qrows], t_swap[qrows])
                    else:
                        ob = jnp.where(low, t_swap[qrows], t_plain[qrows])
                    zb = z_ref[0, rows, slab * LANES:(slab + 1) * LANES].astype(F32)
                    u_ref[0, rows, slab * LANES:(slab + 1) * LANES] = (
                        ob * (zb * jax.nn.sigmoid(zb))).astype(BF16)

        stages = [slab_chunk(j) for j in range(n_slabs)]
        while stages:
            stages = [s for s in stages if next(s, "done") != "done"]
        return carry

    lax.fori_loop(0, n_chunks, chunk, 0)


def _rope_tables(pos):
    half = ROT_DIM // 2
    inv = (ROPE_THETA ** (-np.arange(0, ROT_DIM, 2) / ROT_DIM)).astype(np.float32)
    ang = pos.astype(F32)[:, None] * inv[None, :]
    cos, sin = jnp.cos(ang), jnp.sin(ang)
    n = pos.shape[0]
    rest = HEAD_DIM - ROT_DIM
    cos_h = jnp.concatenate([cos, cos, jnp.ones((n, rest), F32)], axis=1)
    slo_h = jnp.concatenate([-sin, jnp.zeros((n, half + rest), F32)], axis=1)
    shi_h = jnp.concatenate([jnp.zeros((n, half), F32), sin, jnp.zeros((n, rest), F32)], axis=1)
    reps = LANES // HEAD_DIM
    return tuple(jnp.tile(a, (1, reps)) for a in (cos_h, slo_h, shi_h))


def _attention(pb, sinks, tables, cache_k, cache_v, *, tt, masked, outw):
    bsz, t, _ = pb.shape
    tab = pl.BlockSpec((tt, LANES), lambda b, j, s: (j, 0))
    cache = pl.BlockSpec((1, WINDOW, KV_W), lambda b, j, s: (b, 0, 0))
    win = pl.BlockSpec((1, outw, KV_W), lambda b, j, s: (b, 0, 0))
    grid_spec = pltpu.PrefetchScalarGridSpec(
        num_scalar_prefetch=1,
        grid=(bsz, t // tt),
        in_specs=[pl.BlockSpec((1, tt, D_B), lambda b, j, s: (b, j, COL_QB // D_B)),
                  pl.BlockSpec((1, tt, 2 * KV_W), lambda b, j, s: (b, j, COL_KV // (2 * KV_W))),
                  pl.BlockSpec((1, tt, D_B), lambda b, j, s: (b, j, COL_ZB // D_B)),
                  tab, tab, tab, cache, cache],
        out_specs=[pl.BlockSpec((1, tt, D_B), lambda b, j, s: (b, j, 0)), win, win],
        scratch_shapes=[pltpu.VMEM((WINDOW + tt, 2 * KV_W), BF16), pltpu.VMEM((WINDOW + tt, 2 * KV_W), BF16)],
    )
    return pl.pallas_call(
        functools.partial(_attn_kernel, tt=tt, masked=masked, outw=outw),
        grid_spec=grid_spec,
        out_shape=[jax.ShapeDtypeStruct((bsz, t, D_B), BF16),
                   jax.ShapeDtypeStruct((bsz, outw, KV_W), F32),
                   jax.ShapeDtypeStruct((bsz, outw, KV_W), F32)],
        compiler_params=_cparams(("arbitrary", "arbitrary")),
        name="swa_attention",
    )(sinks, pb, pb, pb, *tables, cache_k, cache_v)


def _out_kernel(ua_ref, ub_ref, ma_ref, mb_ref, x_ref, gate_ref, wpa_ref, wpb_ref, wout_ref, fg_ref,
                y_ref, *, nb, tt, final):
    m = nb * tt
    pa = jnp.dot(ua_ref[...].reshape(m, D_A), wpa_ref[...], preferred_element_type=F32)
    pb = jnp.dot(ub_ref[...].reshape(m, D_B), wpb_ref[...], preferred_element_type=F32)
    ma = ma_ref[...].reshape(m, D_MODEL).astype(F32)
    mb = mb_ref[...].reshape(m, D_MODEL).astype(F32)
    merged = jax.nn.sigmoid(ma) * pa + jax.nn.sigmoid(mb) * pb
    o = jnp.dot(merged.astype(BF16), wout_ref[...], preferred_element_type=F32)
    y = x_ref[...] + gate_ref[...] * o.reshape(nb, tt, D_MODEL)
    if final:
        y = (y * lax.rsqrt(jnp.mean(y * y, axis=-1, keepdims=True) + EPS)) * fg_ref[...]
    y_ref[...] = y


def _output(ua, ub, pb, x, gate, wpa, wpb, wout, final_g, *, layer, nb, tt, final):
    bsz, t, _ = x.shape
    tok = lambda i, j: (i, j, 0)
    const = lambda i, j: (0, 0)
    of_layer = lambda i, j: (layer, 0, 0)
    single = pl.Buffered(1)
    return pl.pallas_call(
        functools.partial(_out_kernel, nb=nb, tt=tt, final=final),
        grid=(bsz // nb, t // tt),
        in_specs=[pl.BlockSpec((nb, tt, D_A), tok),
                  pl.BlockSpec((nb, tt, D_B), tok),
                  pl.BlockSpec((nb, tt, D_MODEL), lambda i, j: (i, j, COL_MA // D_MODEL)),
                  pl.BlockSpec((nb, tt, D_MODEL), lambda i, j: (i, j, COL_MB // D_MODEL)),
                  pl.BlockSpec((nb, tt, D_MODEL), tok),
                  pl.BlockSpec((nb, 1, D_MODEL), lambda i, j: (i, 0, 0)),
                  pl.BlockSpec((None, D_A, D_MODEL), of_layer, pipeline_mode=single),
                  pl.BlockSpec((None, D_B, D_MODEL), of_layer, pipeline_mode=single),
                  pl.BlockSpec((None, D_MODEL, D_MODEL), of_layer, pipeline_mode=single),
                  pl.BlockSpec((1, D_MODEL), const)],
        out_specs=pl.BlockSpec((nb, tt, D_MODEL), tok),
        out_shape=jax.ShapeDtypeStruct((bsz, t, D_MODEL), F32),
        compiler_params=_cparams(("arbitrary", "arbitrary")),
        name="merge_output",
    )(ua, ub, pb, pb, x, gate, wpa, wpb, wout, final_g)


def _tiles(t):
    if t >= 1024:
        return dict(proj=(1, min(t, 2048)), out=(1, 512), hgrn_tt=min(t, 1024), attn_tt=min(t, 1024))
    return dict(proj=(16, t), out=(4, t), hgrn_tt=t, attn_tt=t)


def _layer(x, mod, w, lb, s0, cache_k, cache_v, tables, *, final, final_g):
    bsz, t, _ = x.shape
    tl = _tiles(t)
    shift, scale, gate = (mod[:, None, i * D_MODEL:(i + 1) * D_MODEL] for i in range(3))
    nb, tt = tl["proj"]
    pf, pb = _in_projection(x, shift, scale, w["norm_g"], w["w_in"], layer=w["layer"], nb=min(nb, bsz), tt=tt)
    ua, s_new = _hgrn(pf, pb, lb, w["hgrn_g"], s0, tt=tl["hgrn_tt"])
    masked = cache_k is None
    if masked:
        cache_k = jnp.zeros((bsz, WINDOW, KV_W), F32)
        cache_v = cache_k
    outw = min(t, WINDOW)
    ub, k_new, v_new = _attention(pb, w["sinks"], tables, cache_k, cache_v,
                                  tt=tl["attn_tt"], masked=masked, outw=outw)
    nb, tt = tl["out"]
    y = _output(ua, ub, pb, x, gate, w["w_pa"], w["w_pb"], w["w_out"], final_g,
                layer=w["layer"], nb=min(nb, bsz), tt=tt, final=final)
    shape = (bsz, outw, N_KV, HEAD_DIM)
    return y, s_new, k_new.reshape(shape), v_new.reshape(shape)


def kernel(x_prompt, x_sample, c_prompt, c_sample, state_hgrn, cache_win_k, cache_win_v, ada_w, ada_b,
           norm_g, w_in, lb_logits, hgrn_norm_g, sinks, w_branch_a, w_branch_b, w_out, final_norm_g):
    bp, tp = x_prompt.shape[0], x_prompt.shape[1]
    bs, ts = x_sample.shape[0], x_sample.shape[1]

    w_in_p = w_in.astype(BF16)
    w_pa = w_branch_a.astype(BF16)
    w_pb = w_branch_b.astype(BF16)
    w_o = w_out.astype(BF16)
    prob = jax.nn.softmax(lb_logits.astype(F32), axis=0)
    lb = jnp.cumsum(prob, axis=0) - prob[:1]
    final_g = final_norm_g.reshape(1, D_MODEL)

    mod = _modulation(jnp.concatenate([c_prompt, c_sample], axis=0), ada_w, ada_b)
    tab_p = _rope_tables(jnp.arange(tp))
    tab_s = _rope_tables(PAST_LEN + jnp.arange(ts))
    zero_state = jnp.zeros((bp, HGRN_HEADS, HGRN_DK, HGRN_DV), F32)

    hp, hs = x_prompt, x_sample
    outs = [[] for _ in range(6)]
    for l in range(DEPTH):
        w = dict(norm_g=norm_g[l].reshape(1, D_MODEL), w_in=w_in_p, hgrn_g=hgrn_norm_g[l].reshape(1, D_A),
                 sinks=sinks[l], w_pa=w_pa, w_pb=w_pb, w_out=w_o, layer=l)
        lbl = lb[l].reshape(1, D_A)
        final = l == DEPTH - 1
        hp, s_p, k_p, v_p = _layer(hp, mod[l, :bp], w, lbl, zero_state, None, None, tab_p,
                                   final=final, final_g=final_g)
        hs, s_s, k_s, v_s = _layer(hs, mod[l, bp:], w, lbl, state_hgrn[l],
                                   cache_win_k[l].reshape(bs, WINDOW, KV_W),
                                   cache_win_v[l].reshape(bs, WINDOW, KV_W), tab_s,
                                   final=final, final_g=final_g)
        for acc, val in zip(outs, (s_p, k_p, v_p, s_s, k_s, v_s)):
            acc.append(val)
    return (hp, hs) + tuple(jnp.stack(o) for o in outs)
```

```python
import functools

import jax
import jax.numpy as jnp
import numpy as np
from jax import lax
from jax.experimental import pallas as pl
from jax.experimental.pallas import tpu as pltpu

F32 = jnp.float32
BF16 = jnp.bfloat16

D_MODEL = 2048
DEPTH = 4
PAST_LEN = 4096
CHUNK = 64
D_A = D_MODEL // 2
HGRN_DK = 128
HGRN_HEADS = D_A // HGRN_DK
HGRN_DV = D_A // HGRN_HEADS
D_B = D_MODEL // 2
HEAD_DIM = 64
N_Q = D_B // HEAD_DIM
N_KV = N_Q // 4
GROUP = N_Q // N_KV
WINDOW = 128
N_LOOKBACK_CHUNKS = WINDOW // CHUNK
BAND = WINDOW + CHUNK
ROT_DIM = HEAD_DIM // 4
ROPE_THETA = 500000.0
ATTN_SCALE = HEAD_DIM ** -0.5
EPS = 1e-6
NEG = -1e30
LOG2E = 1.4426950408889634
N_IN = 5 * D_A + N_Q * HEAD_DIM + 2 * N_KV * HEAD_DIM + D_B + 2 * D_MODEL

LANES = 128
SUBLANES = 8
KV_W = N_KV * HEAD_DIM
VMEM_LIMIT = 58 * 1024 * 1024

N_F32 = D_A
N_B16 = N_IN - N_F32
COL_MA, COL_MB = 0, D_MODEL
COL_QA, COL_IA, COL_GA, COL_ZA = (2 * D_MODEL + i * D_A for i in range(4))
COL_QB = COL_ZA + D_A
COL_ZB = COL_QB + D_B
COL_KV = COL_ZB + D_B
PROJ_TN = 512
PROJ_ROWS = 512


def _proj_tile_order():
    sizes = dict(qa=D_A, fa=D_A, ia=D_A, ga=D_A, za=D_A, qb=D_B, kv=2 * KV_W, zb=D_B, ma=D_MODEL, mb=D_MODEL)
    start, first = 0, {}
    for name, n in sizes.items():
        first[name] = start // PROJ_TN
        start += n
    order = []
    for name in ("fa", "ma", "mb", "qa", "ia", "ga", "za", "qb", "zb", "kv"):
        order += range(first[name], first[name] + sizes[name] // PROJ_TN)
    return np.asarray(order, np.int32)


def _cparams(sem):
    return pltpu.CompilerParams(dimension_semantics=sem, vmem_limit_bytes=VMEM_LIMIT)


def _mod_kernel(c_ref, w_ref, b_ref, o_ref):
    acc = jnp.dot(c_ref[...].astype(BF16), w_ref[0].astype(BF16), preferred_element_type=F32)
    o_ref[0] = acc + b_ref[0]


def _modulation(c_all, ada_w, ada_b):
    n, d3, tn = c_all.shape[0], 3 * D_MODEL, 768
    return pl.pallas_call(
        _mod_kernel,
        grid=(DEPTH, d3 // tn),
        in_specs=[pl.BlockSpec((n, D_MODEL), lambda l, j: (0, 0)),
                  pl.BlockSpec((1, D_MODEL, tn), lambda l, j: (l, 0, j)),
                  pl.BlockSpec((1, 1, tn), lambda l, j: (l, 0, j))],
        out_specs=pl.BlockSpec((1, n, tn), lambda l, j: (l, 0, j)),
        out_shape=jax.ShapeDtypeStruct((DEPTH, n, d3), F32),
        compiler_params=_cparams(("arbitrary", "arbitrary")),
        name="adaln_mod",
    )(c_all, ada_w, ada_b.reshape(DEPTH, 1, d3))


def _inproj_kernel(src_ref, x_ref, shift_ref, scale_ref, g_ref, w_ref, of_ref, ob_ref, h_ref, *,
                   nb, tt, rc, nf):
    del src_ref
    n = pl.program_id(2)
    m = nb * tt
    step = min(m, PROJ_ROWS)

    def normalise(r0, r1):
        for r in range(r0, r1, rc):
            b, q = divmod(r, tt)
            x = x_ref[b, q:q + rc, :]
            y = x * lax.rsqrt(jnp.mean(x * x, axis=-1, keepdims=True) + EPS)
            h = (y * g_ref[...]) * (1.0 + scale_ref[b]) + shift_ref[b]
            h_ref[r:r + rc, :] = h.astype(BF16)

    def produce(o_ref, with_norm):
        for r in range(0, m, step):
            if with_norm:
                normalise(r, r + step)
            acc = jnp.dot(h_ref[r:r + step, :], w_ref[...], preferred_element_type=F32).astype(o_ref.dtype)
            if nb == 1:
                o_ref[0, r:r + step, :] = acc
            else:
                o_ref[r // tt:(r + step) // tt] = acc.reshape(step // tt, tt, acc.shape[-1])

    pl.when(n == 0)(functools.partial(produce, of_ref, True))
    pl.when(jnp.logical_and(n > 0, n < nf))(functools.partial(produce, of_ref, False))
    pl.when(n >= nf)(functools.partial(produce, ob_ref, False))


def _in_projection(x, shift, scale, norm_g, w, *, layer, nb, tt):
    bsz, t, _ = x.shape
    tn = PROJ_TN
    rc = min(tt, 128)
    nf = N_F32 // tn
    kern = functools.partial(_inproj_kernel, nb=nb, tt=tt, rc=rc, nf=nf)
    grid_spec = pltpu.PrefetchScalarGridSpec(
        num_scalar_prefetch=1,
        grid=(bsz // nb, t // tt, N_IN // tn),
        in_specs=[pl.BlockSpec((nb, tt, D_MODEL), lambda i, j, n, src: (i, j, 0)),
                  pl.BlockSpec((nb, 1, D_MODEL), lambda i, j, n, src: (i, 0, 0)),
                  pl.BlockSpec((nb, 1, D_MODEL), lambda i, j, n, src: (i, 0, 0)),
                  pl.BlockSpec((1, D_MODEL), lambda i, j, n, src: (0, 0)),
                  pl.BlockSpec((None, D_MODEL, tn), lambda i, j, n, src: (layer, 0, src[n]))],
        out_specs=[pl.BlockSpec((nb, tt, tn), lambda i, j, n, src: (i, j, jnp.minimum(n, nf - 1)),
                                pipeline_mode=pl.Buffered(1)),
                   pl.BlockSpec((nb, tt, tn), lambda i, j, n, src: (i, j, jnp.maximum(n - nf, 0)))],
        scratch_shapes=[pltpu.VMEM((nb * tt, D_MODEL), BF16)],
    )
    return pl.pallas_call(
        kern,
        grid_spec=grid_spec,
        out_shape=[jax.ShapeDtypeStruct((bsz, t, N_F32), F32),
                   jax.ShapeDtypeStruct((bsz, t, N_B16), BF16)],
        compiler_params=_cparams(("arbitrary", "arbitrary", "arbitrary")),
        name="in_projection",
    )(jnp.asarray(_proj_tile_order()), x, shift, scale, norm_g, w)


HEADS_PER_LOOP = 8
LEVELS = CHUNK.bit_length() - 1
MXU_LEVELS = 2


def _hgrn_kernel(q_ref, f_ref, i_ref, g_ref, z_ref, lb_ref, ng_ref, s0_ref, mz_ref,
                 u_ref, sout_ref, st_ref, *, n_chunks):
    t = pl.program_id(1)

    @pl.when(t == 0)
    def _():
        for h in range(HGRN_HEADS):
            st_ref[h] = s0_ref[0, h].T

    row = lax.broadcasted_iota(jnp.int32, (CHUNK, CHUNK), 0)
    col = lax.broadcasted_iota(jnp.int32, (CHUNK, CHUNK), 1)
    code = jnp.where(row > col, row ^ col, jnp.where(row == col, 0, 2 * CHUNK))
    small_masks = [jnp.right_shift(code, l) == 1 for l in range(LEVELS - MXU_LEVELS)]
    diag_mask = code == 0
    sub_row = lax.broadcasted_iota(jnp.int32, (SUBLANES, HGRN_DK), 0)
    upper_half = [(jnp.right_shift(sub_row, l) & 1) == 1 for l in range(SUBLANES.bit_length() - 1)]
    nt_dims = (((1,), (1,)), ((), ()))
    tn_dims = (((0,), (0,)), ((), ()))

    def padded(x, r0):
        parts = []
        if r0:
            parts.append(jnp.zeros((r0, HGRN_DK), F32))
        parts.append(x)
        if CHUNK - r0 - x.shape[0]:
            parts.append(jnp.zeros((CHUNK - r0 - x.shape[0], HGRN_DK), F32))
        return jnp.concatenate(parts, axis=0).astype(BF16)

    def head_chunk(h, rows):
        cols = slice(h * HGRN_DK, (h + 1) * HGRN_DK)
        lb = lb_ref[:, cols]
        a = f_ref[0, rows, cols]
        qa = q_ref[0, rows, cols].astype(F32)
        v = i_ref[0, rows, cols]
        a2 = a * LOG2E
        e = jnp.exp2(-a2)
        l1e = jnp.log(1.0 + e) * LOG2E
        log_f = jnp.log(1.0 + lb * e) * LOG2E - l1e
        k = jnp.exp2((jnp.log(1.0 - lb) * LOG2E - a2) - l1e)
        q = qa * jax.nn.sigmoid(qa)

        hi = log_f.astype(BF16)
        r1 = log_f - hi.astype(F32)
        mid = r1.astype(BF16)
        lo = (r1 - mid.astype(F32)).astype(BF16)
        zall = yield jnp.concatenate([hi, mid, lo], axis=0)
        b = zall[:CHUNK]

        def level_sums(l):
            i = LEVELS - l
            return zall[i * CHUNK:(i + 1) * CHUNK]

        st = st_ref[h]
        q_dec = (q * jnp.exp2(b)).astype(BF16)
        st16 = st.astype(BF16)
        x_cols, y_cols = [], []
        for l in range(LEVELS - 1, LEVELS - 1 - MXU_LEVELS, -1):
            half = 1 << l
            w = jnp.exp2(level_sums(l))
            for base in range(0, CHUNK, 2 * half):
                lo_rows = slice(base, base + half)
                hi_rows = slice(base + half, base + 2 * half)
                x_cols.append(padded(q[hi_rows] * w[hi_rows], base + half))
                y_cols.append(padded(k[lo_rows] * w[lo_rows], base))
        x_big, y_big = jnp.concatenate(x_cols, axis=1), jnp.concatenate(y_cols, axis=1)
        pairs = [(q.astype(BF16), k.astype(BF16))]
        for l in range(LEVELS - MXU_LEVELS):
            w = jnp.exp2(level_sums(l))
            if (1 << l) < SUBLANES:
                side = jnp.concatenate([jnp.where(upper_half[l], q[r:r + SUBLANES], k[r:r + SUBLANES])
                                        for r in range(0, CHUNK, SUBLANES)], axis=0)
            else:
                side = jnp.concatenate([q[r:r + SUBLANES] if (r >> l) & 1 else k[r:r + SUBLANES]
                                        for r in range(0, CHUNK, SUBLANES)], axis=0)
            tl = (w * side).astype(BF16)
            pairs.append((tl, tl))
        b_end = b[CHUNK - 1:CHUNK]
        k_dec = (k * jnp.exp2(b_end - b)).astype(BF16)
        st_new = st * jnp.exp2(b_end)
        yield
        o_inter = lax.dot_general(q_dec, st16, nt_dims, preferred_element_type=F32)
        a_big = lax.dot_general(x_big, y_big, nt_dims, preferred_element_type=F32)
        a_small = [lax.dot_general(x, y, nt_dims, preferred_element_type=F32) for x, y in pairs]
        st_add = lax.dot_general(v, k_dec, tn_dims, preferred_element_type=F32)
        yield
        small = jnp.where(diag_mask, a_small[0], 0.0)
        for l in range(LEVELS - MXU_LEVELS):
            small = jnp.where(small_masks[l], a_small[1 + l], small)
        a_mat = (a_big + small).astype(BF16)
        st_ref[h] = st_new + st_add
        yield
        o_intra = jnp.dot(a_mat, v, preferred_element_type=F32)
        yield
        o = o_inter + o_intra
        y = o * lax.rsqrt(jnp.mean(o * o, axis=-1, keepdims=True) + EPS) * ng_ref[:, cols]
        ga = g_ref[0, rows, cols].astype(F32)
        za = z_ref[0, rows, cols].astype(F32)
        u = y * jax.nn.sigmoid(ga) * (za * jax.nn.sigmoid(za))
        u_ref[0, rows, cols] = u.astype(BF16)

    for h0 in range(0, HGRN_HEADS, HEADS_PER_LOOP):
        def chunk(c, carry, h0=h0):
            rows = pl.ds(pl.multiple_of(c * CHUNK, CHUNK), CHUNK)
            stages = [head_chunk(h, rows) for h in range(h0, h0 + HEADS_PER_LOOP)]
            splits = [next(s) for s in stages]
            sums = [jnp.dot(mz_ref[...], jnp.concatenate(splits[i:i + 2], axis=1), preferred_element_type=F32)
                    for i in range(0, HEADS_PER_LOOP, 2)]
            for i, s in enumerate(stages):
                s.send(sums[i // 2][:, (i % 2) * HGRN_DK:(i % 2 + 1) * HGRN_DK])
            while stages:
                stages = [s for s in stages if next(s, "done") != "done"]
            return carry

        lax.fori_loop(0, n_chunks, chunk, 0)

    @pl.when(t == pl.num_programs(1) - 1)
    def _():
        for h in range(HGRN_HEADS):
            sout_ref[0, h] = st_ref[h].T


def _decay_sum_matrix():
    m = np.zeros((1 + LEVELS, CHUNK, CHUNK), np.float32)
    idx = np.arange(CHUNK)
    m[0] = idx[:, None] >= idx[None, :]
    for i, l in enumerate(range(LEVELS - 1, -1, -1)):
        half = 1 << l
        for r in range(CHUNK):
            pos = r % (2 * half)
            first_upper = r - pos + half
            if pos >= half:
                m[1 + i, r, first_upper:r + 1] = 1.0
            else:
                m[1 + i, r, r + 1:first_upper] = 1.0
    m = m.reshape(-1, CHUNK)
    return jnp.asarray(np.concatenate([m, m, m], axis=1), dtype=BF16)


def _hgrn(pf, pb, lb, ng, s0, *, tt):
    bsz, t, _ = pb.shape

    def col(c0):
        return pl.BlockSpec((1, tt, D_A), lambda b, j, c0=c0: (b, j, c0 // D_A))

    vec = pl.BlockSpec((1, D_A), lambda b, j: (0, 0))
    state = pl.BlockSpec((1, HGRN_HEADS, HGRN_DK, HGRN_DV), lambda b, j: (b, 0, 0, 0))
    mz = _decay_sum_matrix()
    return pl.pallas_call(
        functools.partial(_hgrn_kernel, n_chunks=tt // CHUNK),
        grid=(bsz, t // tt),
        in_specs=[col(COL_QA), col(0), col(COL_IA), col(COL_GA), col(COL_ZA), vec, vec, state,
                  pl.BlockSpec(mz.shape, lambda b, j: (0, 0))],
        out_specs=[pl.BlockSpec((1, tt, D_A), lambda b, j: (b, j, 0)), state],
        out_shape=[jax.ShapeDtypeStruct((bsz, t, D_A), BF16),
                   jax.ShapeDtypeStruct((bsz, HGRN_HEADS, HGRN_DK, HGRN_DV), F32)],
        scratch_shapes=[pltpu.VMEM((HGRN_HEADS, HGRN_DV, HGRN_DK), F32)],
        compiler_params=_cparams(("arbitrary", "arbitrary")),
        name="hgrn2",
    )(pb, pf, pb, pb, pb, lb, ng, s0, mz)


def _rope(x, cos, sin_lo, sin_hi):
    half = ROT_DIM // 2
    return x * cos + pltpu.roll(x, LANES - half, 1) * sin_lo + pltpu.roll(x, half, 1) * sin_hi


def _attn_kernel(sink_ref, q_ref, kv_ref, z_ref, cos_ref, slo_ref, shi_ref, ck_ref, cv_ref,
                 u_ref, ko_ref, vo_ref, kbuf, vbuf, *, tt, masked, outw):
    t = pl.program_id(1)
    last = t == pl.num_programs(1) - 1
    n_chunks = tt // CHUNK
    n_slabs = KV_W // LANES

    def store_band(buf, r0, x, lanes):
        buf[r0:r0 + x.shape[0], lanes] = x.astype(BF16)
        buf[r0:r0 + x.shape[0], slice(KV_W + lanes.start, KV_W + lanes.stop)] = (
            pltpu.roll(x, HEAD_DIM, 1).astype(BF16))

    @pl.when(t == 0)
    def _():
        for j in range(n_slabs):
            lanes = slice(j * LANES, (j + 1) * LANES)
            store_band(kbuf, 0, ck_ref[0, :, lanes], lanes)
            store_band(vbuf, 0, cv_ref[0, :, lanes], lanes)

    if tt >= WINDOW:
        @pl.when(t > 0)
        def _():
            kbuf[0:WINDOW, :] = kbuf[tt:tt + WINDOW, :]
            vbuf[0:WINDOW, :] = vbuf[tt:tt + WINDOW, :]

    for j in range(n_slabs):
        lanes = slice(j * LANES, (j + 1) * LANES)
        kr = _rope(kv_ref[0, :, lanes].astype(F32), cos_ref[...], slo_ref[...], shi_ref[...])
        store_band(kbuf, WINDOW, kr, lanes)
        store_band(vbuf, WINDOW, kv_ref[0, :, slice(KV_W + lanes.start, KV_W + lanes.stop)].astype(F32), lanes)

        @pl.when(last)
        def _(kr=kr, lanes=lanes):
            ko_ref[0, :, lanes] = kr[tt - outw:, :]

    @pl.when(last)
    def _():
        vo_ref[0] = kv_ref[0, tt - outw:tt, KV_W:2 * KV_W].astype(F32)

    lane = lax.broadcasted_iota(jnp.int32, (CHUNK, LANES), 1)
    low = lane < HEAD_DIM
    heads_per_slab = LANES // HEAD_DIM * GROUP

    def kv_in_low_half(p):
        return (2 * p) // GROUP == 0

    def matched_head(p):
        return 0 if kv_in_low_half(p) else 1

    sink_rows = []
    for j in range(n_slabs):
        order = ([heads_per_slab * j + 2 * p + matched_head(p) for p in range(GROUP)]
                 + [heads_per_slab * j + 2 * p + 1 - matched_head(p) for p in range(GROUP)])
        sink_rows.append(jnp.concatenate([jnp.full((1, CHUNK), sink_ref[h], F32) for h in order], axis=1))
    nt_dims = (((1,), (1,)), ((), ()))
    tn_dims = (((0,), (0,)), ((), ()))
    half_cols = GROUP * CHUNK

    def chunk(c, carry):
        r0 = pl.multiple_of(c * CHUNK, CHUNK)
        rows = pl.ds(r0, CHUNK)
        band = pl.ds(r0, BAND)
        cq, sl, sh = cos_ref[rows, :], slo_ref[rows, :], shi_ref[rows, :]
        seq_chunk = t * n_chunks + c

        def slab_chunk(j):
            plain = slice(j * LANES, (j + 1) * LANES)
            swapped = slice(KV_W + j * LANES, KV_W + (j + 1) * LANES)
            k_plain, k_swap = kbuf[band, plain], kbuf[band, swapped]
            v_plain, v_swap = vbuf[band, plain], vbuf[band, swapped]
            q_plain, q_swap = [], []
            for p in range(GROUP):
                slab = GROUP * j + p
                x = _rope(q_ref[0, rows, slab * LANES:(slab + 1) * LANES].astype(F32), cq, sl, sh) * ATTN_SCALE
                halves = [jnp.where(low, x, 0.0).astype(BF16), jnp.where(low, 0.0, x).astype(BF16)]
                q_plain.append(halves[matched_head(p)])
                q_swap.append(halves[1 - matched_head(p)])
            q_plain = jnp.concatenate(q_plain, axis=0)
            q_swap = jnp.concatenate(q_swap, axis=0)
            yield
            s = jnp.concatenate([lax.dot_general(k_plain, q_plain, nt_dims, preferred_element_type=F32),
                                 lax.dot_general(k_swap, q_swap, nt_dims, preferred_element_type=F32)],
                                axis=1)
            yield
            if masked:
                blocks = [jnp.where(seq_chunk >= N_LOOKBACK_CHUNKS - i, s[i * CHUNK:(i + 1) * CHUNK], NEG)
                          for i in range(N_LOOKBACK_CHUNKS)]
                s = jnp.concatenate(blocks + [s[WINDOW:]], axis=0)
            sink = sink_rows[j]
            m = jnp.maximum(jnp.max(s, axis=0, keepdims=True), sink)
            pr = jnp.exp(s - m)
            inv = 1.0 / (jnp.sum(pr, axis=0, keepdims=True) + jnp.exp(sink - m))
            pr16 = pr.astype(BF16)
            yield
            o_plain = lax.dot_general(v_plain, pr16[:, :half_cols], tn_dims, preferred_element_type=F32)
            o_swap = lax.dot_general(v_swap, pr16[:, half_cols:], tn_dims, preferred_element_type=F32)
            yield
            o_plain = o_plain * inv[:, :half_cols]
            o_swap = o_swap * inv[:, half_cols:]
            for pair in range(GROUP // 2):
                t_plain = o_plain[:, pair * LANES:(pair + 1) * LANES].T
                t_swap = o_swap[:, pair * LANES:(pair + 1) * LANES].T
                for i in range(2):
                    p = 2 * pair + i
                    slab = GROUP * j + p
                    qrows = slice(i * CHUNK, (i + 1) * CHUNK)
                    if matched_head(p) == 0:
                        ob = jnp.where(low, t_plain[qrows], t_swap[qrows])
                    else:
                        ob = jnp.where(low, t_swap[qrows], t_plain[qrows])
                    zb = z_ref[0, rows, slab * LANES:(slab + 1) * LANES].astype(F32)
                    u_ref[0, rows, slab * LANES:(slab + 1) * LANES] = (
                        ob * (zb * jax.nn.sigmoid(zb))).astype(BF16)

        stages = [slab_chunk(j) for j in range(n_slabs)]
        while stages:
            stages = [s for s in stages if next(s, "done") != "done"]
        return carry

    lax.fori_loop(0, n_chunks, chunk, 0)


def _rope_tables(pos):
    half = ROT_DIM // 2
    inv = (ROPE_THETA ** (-np.arange(0, ROT_DIM, 2) / ROT_DIM)).astype(np.float32)
    ang = pos.astype(F32)[:, None] * inv[None, :]
    cos, sin = jnp.cos(ang), jnp.sin(ang)
    n = pos.shape[0]
    rest = HEAD_DIM - ROT_DIM
    cos_h = jnp.concatenate([cos, cos, jnp.ones((n, rest), F32)], axis=1)
    slo_h = jnp.concatenate([-sin, jnp.zeros((n, half + rest), F32)], axis=1)
    shi_h = jnp.concatenate([jnp.zeros((n, half), F32), sin, jnp.zeros((n, rest), F32)], axis=1)
    reps = LANES // HEAD_DIM
    return tuple(jnp.tile(a, (1, reps)) for a in (cos_h, slo_h, shi_h))


def _attention(pb, sinks, tables, cache_k, cache_v, *, tt, masked, outw):
    bsz, t, _ = pb.shape
    tab = pl.BlockSpec((tt, LANES), lambda b, j, s: (j, 0))
    cache = pl.BlockSpec((1, WINDOW, KV_W), lambda b, j, s: (b, 0, 0))
    win = pl.BlockSpec((1, outw, KV_W), lambda b, j, s: (b, 0, 0))
    grid_spec = pltpu.PrefetchScalarGridSpec(
        num_scalar_prefetch=1,
        grid=(bsz, t // tt),
        in_specs=[pl.BlockSpec((1, tt, D_B), lambda b, j, s: (b, j, COL_QB // D_B)),
                  pl.BlockSpec((1, tt, 2 * KV_W), lambda b, j, s: (b, j, COL_KV // (2 * KV_W))),
                  pl.BlockSpec((1, tt, D_B), lambda b, j, s: (b, j, COL_ZB // D_B)),
                  tab, tab, tab, cache, cache],
        out_specs=[pl.BlockSpec((1, tt, D_B), lambda b, j, s: (b, j, 0)), win, win],
        scratch_shapes=[pltpu.VMEM((WINDOW + tt, 2 * KV_W), BF16), pltpu.VMEM((WINDOW + tt, 2 * KV_W), BF16)],
    )
    return pl.pallas_call(
        functools.partial(_attn_kernel, tt=tt, masked=masked, outw=outw),
        grid_spec=grid_spec,
        out_shape=[jax.ShapeDtypeStruct((bsz, t, D_B), BF16),
                   jax.ShapeDtypeStruct((bsz, outw, KV_W), F32),
                   jax.ShapeDtypeStruct((bsz, outw, KV_W), F32)],
        compiler_params=_cparams(("arbitrary", "arbitrary")),
        name="swa_attention",
    )(sinks, pb, pb, pb, *tables, cache_k, cache_v)


def _out_kernel(ua_ref, ub_ref, ma_ref, mb_ref, x_ref, gate_ref, wpa_ref, wpb_ref, wout_ref, fg_ref,
                y_ref, *, nb, tt, final):
    m = nb * tt
    pa = jnp.dot(ua_ref[...].reshape(m, D_A), wpa_ref[...], preferred_element_type=F32)
    pb = jnp.dot(ub_ref[...].reshape(m, D_B), wpb_ref[...], preferred_element_type=F32)
    ma = ma_ref[...].reshape(m, D_MODEL).astype(F32)
    mb = mb_ref[...].reshape(m, D_MODEL).astype(F32)
    merged = jax.nn.sigmoid(ma) * pa + jax.nn.sigmoid(mb) * pb
    o = jnp.dot(merged.astype(BF16), wout_ref[...], preferred_element_type=F32)
    y = x_ref[...] + gate_ref[...] * o.reshape(nb, tt, D_MODEL)
    if final:
        y = (y * lax.rsqrt(jnp.mean(y * y, axis=-1, keepdims=True) + EPS)) * fg_ref[...]
    y_ref[...] = y


def _output(ua, ub, pb, x, gate, wpa, wpb, wout, final_g, *, layer, nb, tt, final):
    bsz, t, _ = x.shape
    tok = lambda i, j: (i, j, 0)
    const = lambda i, j: (0, 0)
    of_layer = lambda i, j: (layer, 0, 0)
    single = pl.Buffered(1)
    return pl.pallas_call(
        functools.partial(_out_kernel, nb=nb, tt=tt, final=final),
        grid=(bsz // nb, t // tt),
        in_specs=[pl.BlockSpec((nb, tt, D_A), tok),
                  pl.BlockSpec((nb, tt, D_B), tok),
                  pl.BlockSpec((nb, tt, D_MODEL), lambda i, j: (i, j, COL_MA // D_MODEL)),
                  pl.BlockSpec((nb, tt, D_MODEL), lambda i, j: (i, j, COL_MB // D_MODEL)),
                  pl.BlockSpec((nb, tt, D_MODEL), tok),
                  pl.BlockSpec((nb, 1, D_MODEL), lambda i, j: (i, 0, 0)),
                  pl.BlockSpec((None, D_A, D_MODEL), of_layer, pipeline_mode=single),
                  pl.BlockSpec((None, D_B, D_MODEL), of_layer, pipeline_mode=single),
                  pl.BlockSpec((None, D_MODEL, D_MODEL), of_layer, pipeline_mode=single),
                  pl.BlockSpec((1, D_MODEL), const)],
        out_specs=pl.BlockSpec((nb, tt, D_MODEL), tok),
        out_shape=jax.ShapeDtypeStruct((bsz, t, D_MODEL), F32),
        compiler_params=_cparams(("arbitrary", "arbitrary")),
        name="merge_output",
    )(ua, ub, pb, pb, x, gate, wpa, wpb, wout, final_g)


def _tiles(t):
    if t >= 1024:
        return dict(proj=(1, min(t, 2048)), out=(1, 512), hgrn_tt=min(t, 1024), attn_tt=min(t, 1024))
    return dict(proj=(16, t), out=(4, t), hgrn_tt=t, attn_tt=t)


def _layer(x, mod, w, lb, s0, cache_k, cache_v, tables, *, final, final_g):
    bsz, t, _ = x.shape
    tl = _tiles(t)
    shift, scale, gate = (mod[:, None, i * D_MODEL:(i + 1) * D_MODEL] for i in range(3))
    nb, tt = tl["proj"]
    pf, pb = _in_projection(x, shift, scale, w["norm_g"], w["w_in"], layer=w["layer"], nb=min(nb, bsz), tt=tt)
    ua, s_new = _hgrn(pf, pb, lb, w["hgrn_g"], s0, tt=tl["hgrn_tt"])
    masked = cache_k is None
    if masked:
        cache_k = jnp.zeros((bsz, WINDOW, KV_W), F32)
        cache_v = cache_k
    outw = min(t, WINDOW)
    ub, k_new, v_new = _attention(pb, w["sinks"], tables, cache_k, cache_v,
                                  tt=tl["attn_tt"], masked=masked, outw=outw)
    nb, tt = tl["out"]
    y = _output(ua, ub, pb, x, gate, w["w_pa"], w["w_pb"], w["w_out"], final_g,
                layer=w["layer"], nb=min(nb, bsz), tt=tt, final=final)
    shape = (bsz, outw, N_KV, HEAD_DIM)
    return y, s_new, k_new.reshape(shape), v_new.reshape(shape)


def kernel(x_prompt, x_sample, c_prompt, c_sample, state_hgrn, cache_win_k, cache_win_v, ada_w, ada_b,
           norm_g, w_in, lb_logits, hgrn_norm_g, sinks, w_branch_a, w_branch_b, w_out, final_norm_g):
    bp, tp = x_prompt.shape[0], x_prompt.shape[1]
    bs, ts = x_sample.shape[0], x_sample.shape[1]

    w_in_p = w_in.astype(BF16)
    w_pa = w_branch_a.astype(BF16)
    w_pb = w_branch_b.astype(BF16)
    w_o = w_out.astype(BF16)
    prob = jax.nn.softmax(lb_logits.astype(F32), axis=0)
    lb = jnp.cumsum(prob, axis=0) - prob[:1]
    final_g = final_norm_g.reshape(1, D_MODEL)

    mod = _modulation(jnp.concatenate([c_prompt, c_sample], axis=0), ada_w, ada_b)
    tab_p = _rope_tables(jnp.arange(tp))
    tab_s = _rope_tables(PAST_LEN + jnp.arange(ts))
    zero_state = jnp.zeros((bp, HGRN_HEADS, HGRN_DK, HGRN_DV), F32)

    hp, hs = x_prompt, x_sample
    outs = [[] for _ in range(6)]
    for l in range(DEPTH):
        w = dict(norm_g=norm_g[l].reshape(1, D_MODEL), w_in=w_in_p, hgrn_g=hgrn_norm_g[l].reshape(1, D_A),
                 sinks=sinks[l], w_pa=w_pa, w_pb=w_pb, w_out=w_o, layer=l)
        lbl = lb[l].reshape(1, D_A)
        final = l == DEPTH - 1
        hp, s_p, k_p, v_p = _layer(hp, mod[l, :bp], w, lbl, zero_state, None, None, tab_p,
                                   final=final, final_g=final_g)
        hs, s_s, k_s, v_s = _layer(hs, mod[l, bp:], w, lbl, state_hgrn[l],
                                   cache_win_k[l].reshape(bs, WINDOW, KV_W),
                                   cache_win_v[l].reshape(bs, WINDOW, KV_W), tab_s,
                                   final=final, final_g=final_g)
        for acc, val in zip(outs, (s_p, k_p, v_p, s_s, k_s, v_s)):
            acc.append(val)
    return (hp, hs) + tuple(jnp.stack(o) for o in outs)
```

```python
import functools

import jax
import jax.numpy as jnp
import numpy as np
from jax import lax
from jax.experimental import pallas as pl
from jax.experimental.pallas import tpu as pltpu

F32 = jnp.float32
BF16 = jnp.bfloat16

D_MODEL = 2048
DEPTH = 4
PAST_LEN = 4096
CHUNK = 64
D_A = D_MODEL // 2
HGRN_DK = 128
HGRN_HEADS = D_A // HGRN_DK
HGRN_DV = D_A // HGRN_HEADS
D_B = D_MODEL // 2
HEAD_DIM = 64
N_Q = D_B // HEAD_DIM
N_KV = N_Q // 4
GROUP = N_Q // N_KV
WINDOW = 128
N_LOOKBACK_CHUNKS = WINDOW // CHUNK
BAND = WINDOW + CHUNK
ROT_DIM = HEAD_DIM // 4
ROPE_THETA = 500000.0
ATTN_SCALE = HEAD_DIM ** -0.5
EPS = 1e-6
NEG = -1e30
LOG2E = 1.4426950408889634
N_IN = 5 * D_A + N_Q * HEAD_DIM + 2 * N_KV * HEAD_DIM + D_B + 2 * D_MODEL

LANES = 128
SUBLANES = 8
KV_W = N_KV * HEAD_DIM
VMEM_LIMIT = 58 * 1024 * 1024

N_F32 = D_A
N_B16 = N_IN - N_F32
COL_MA, COL_MB = 0, D_MODEL
COL_QA, COL_IA, COL_GA, COL_ZA = (2 * D_MODEL + i * D_A for i in range(4))
COL_QB = COL_ZA + D_A
COL_ZB = COL_QB + D_B
COL_KV = COL_ZB + D_B
PROJ_TN = 512
PROJ_ROWS = 512


def _proj_tile_order():
    sizes = dict(qa=D_A, fa=D_A, ia=D_A, ga=D_A, za=D_A, qb=D_B, kv=2 * KV_W, zb=D_B, ma=D_MODEL, mb=D_MODEL)
    start, first = 0, {}
    for name, n in sizes.items():
        first[name] = start // PROJ_TN
        start += n
    order = []
    for name in ("fa", "ma", "mb", "qa", "ia", "ga", "za", "qb", "zb", "kv"):
        order += range(first[name], first[name] + sizes[name] // PROJ_TN)
    return np.asarray(order, np.int32)


def _cparams(sem):
    return pltpu.CompilerParams(dimension_semantics=sem, vmem_limit_bytes=VMEM_LIMIT)


def _mod_kernel(c_ref, w_ref, b_ref, o_ref):
    acc = jnp.dot(c_ref[...].astype(BF16), w_ref[0].astype(BF16), preferred_element_type=F32)
    o_ref[0] = acc + b_ref[0]


def _modulation(c_all, ada_w, ada_b):
    n, d3, tn = c_all.shape[0], 3 * D_MODEL, 768
    return pl.pallas_call(
        _mod_kernel,
        grid=(DEPTH, d3 // tn),
        in_specs=[pl.BlockSpec((n, D_MODEL), lambda l, j: (0, 0)),
                  pl.BlockSpec((1, D_MODEL, tn), lambda l, j: (l, 0, j)),
                  pl.BlockSpec((1, 1, tn), lambda l, j: (l, 0, j))],
        out_specs=pl.BlockSpec((1, n, tn), lambda l, j: (l, 0, j)),
        out_shape=jax.ShapeDtypeStruct((DEPTH, n, d3), F32),
        compiler_params=_cparams(("arbitrary", "arbitrary")),
        name="adaln_mod",
    )(c_all, ada_w, ada_b.reshape(DEPTH, 1, d3))


def _inproj_kernel(src_ref, x_ref, shift_ref, scale_ref, g_ref, w_ref, of_ref, ob_ref, h_ref, *,
                   nb, tt, rc, nf):
    del src_ref
    n = pl.program_id(2)
    m = nb * tt
    step = min(m, PROJ_ROWS)

    def normalise(r0, r1):
        for r in range(r0, r1, rc):
            b, q = divmod(r, tt)
            x = x_ref[b, q:q + rc, :]
            y = x * lax.rsqrt(jnp.mean(x * x, axis=-1, keepdims=True) + EPS)
            h = (y * g_ref[...]) * (1.0 + scale_ref[b]) + shift_ref[b]
            h_ref[r:r + rc, :] = h.astype(BF16)

    def produce(o_ref, with_norm):
        for r in range(0, m, step):
            if with_norm:
                normalise(r, r + step)
            acc = jnp.dot(h_ref[r:r + step, :], w_ref[...], preferred_element_type=F32).astype(o_ref.dtype)
            if nb == 1:
                o_ref[0, r:r + step, :] = acc
            else:
                o_ref[r // tt:(r + step) // tt] = acc.reshape(step // tt, tt, acc.shape[-1])

    pl.when(n == 0)(functools.partial(produce, of_ref, True))
    pl.when(jnp.logical_and(n > 0, n < nf))(functools.partial(produce, of_ref, False))
    pl.when(n >= nf)(functools.partial(produce, ob_ref, False))


def _in_projection(x, shift, scale, norm_g, w, *, layer, nb, tt):
    bsz, t, _ = x.shape
    tn = PROJ_TN
    rc = min(tt, 128)
    nf = N_F32 // tn
    kern = functools.partial(_inproj_kernel, nb=nb, tt=tt, rc=rc, nf=nf)
    grid_spec = pltpu.PrefetchScalarGridSpec(
        num_scalar_prefetch=1,
        grid=(bsz // nb, t // tt, N_IN // tn),
        in_specs=[pl.BlockSpec((nb, tt, D_MODEL), lambda i, j, n, src: (i, j, 0)),
                  pl.BlockSpec((nb, 1, D_MODEL), lambda i, j, n, src: (i, 0, 0)),
                  pl.BlockSpec((nb, 1, D_MODEL), lambda i, j, n, src: (i, 0, 0)),
                  pl.BlockSpec((1, D_MODEL), lambda i, j, n, src: (0, 0)),
                  pl.BlockSpec((None, D_MODEL, tn), lambda i, j, n, src: (layer, 0, src[n]))],
        out_specs=[pl.BlockSpec((nb, tt, tn), lambda i, j, n, src: (i, j, jnp.minimum(n, nf - 1)),
                                pipeline_mode=pl.Buffered(1)),
                   pl.BlockSpec((nb, tt, tn), lambda i, j, n, src: (i, j, jnp.maximum(n - nf, 0)))],
        scratch_shapes=[pltpu.VMEM((nb * tt, D_MODEL), BF16)],
    )
    return pl.pallas_call(
        kern,
        grid_spec=grid_spec,
        out_shape=[jax.ShapeDtypeStruct((bsz, t, N_F32), F32),
                   jax.ShapeDtypeStruct((bsz, t, N_B16), BF16)],
        compiler_params=_cparams(("arbitrary", "arbitrary", "arbitrary")),
        name="in_projection",
    )(jnp.asarray(_proj_tile_order()), x, shift, scale, norm_g, w)


HEADS_PER_LOOP = 8
LEVELS = CHUNK.bit_length() - 1
MXU_LEVELS = 2


def _hgrn_kernel(q_ref, f_ref, i_ref, g_ref, z_ref, lb_ref, ng_ref, s0_ref, mz_ref,
                 u_ref, sout_ref, st_ref, *, n_chunks):
    t = pl.program_id(1)

    @pl.when(t == 0)
    def _():
        for h in range(HGRN_HEADS):
            st_ref[h] = s0_ref[0, h].T

    row = lax.broadcasted_iota(jnp.int32, (CHUNK, CHUNK), 0)
    col = lax.broadcasted_iota(jnp.int32, (CHUNK, CHUNK), 1)
    code = jnp.where(row > col, row ^ col, jnp.where(row == col, 0, 2 * CHUNK))
    small_masks = [jnp.right_shift(code, l) == 1 for l in range(LEVELS - MXU_LEVELS)]
    diag_mask = code == 0
    sub_row = lax.broadcasted_iota(jnp.int32, (SUBLANES, HGRN_DK), 0)
    upper_half = [(jnp.right_shift(sub_row, l) & 1) == 1 for l in range(SUBLANES.bit_length() - 1)]
    nt_dims = (((1,), (1,)), ((), ()))
    tn_dims = (((0,), (0,)), ((), ()))

    def padded(x, r0):
        parts = []
        if r0:
            parts.append(jnp.zeros((r0, HGRN_DK), F32))
        parts.append(x)
        if CHUNK - r0 - x.shape[0]:
            parts.append(jnp.zeros((CHUNK - r0 - x.shape[0], HGRN_DK), F32))
        return jnp.concatenate(parts, axis=0).astype(BF16)

    def head_chunk(h, rows):
        cols = slice(h * HGRN_DK, (h + 1) * HGRN_DK)
        lb = lb_ref[:, cols]
        a = f_ref[0, rows, cols]
        qa = q_ref[0, rows, cols].astype(F32)
        v = i_ref[0, rows, cols]
        a2 = a * LOG2E
        e = jnp.exp2(-a2)
        l1e = jnp.log(1.0 + e) * LOG2E
        log_f = jnp.log(1.0 + lb * e) * LOG2E - l1e
        k = jnp.exp2((jnp.log(1.0 - lb) * LOG2E - a2) - l1e)
        q = qa * jax.nn.sigmoid(qa)

        hi = log_f.astype(BF16)
        r1 = log_f - hi.astype(F32)
        mid = r1.astype(BF16)
        lo = (r1 - mid.astype(F32)).astype(BF16)
        zall = yield jnp.concatenate([hi, mid, lo], axis=0)
        b = zall[:CHUNK]

        def level_sums(l):
            i = LEVELS - l
            return zall[i * CHUNK:(i + 1) * CHUNK]

        st = st_ref[h]
        q_dec = (q * jnp.exp2(b)).astype(BF16)
        st16 = st.astype(BF16)
        x_cols, y_cols = [], []
        for l in range(LEVELS - 1, LEVELS - 1 - MXU_LEVELS, -1):
            half = 1 << l
            w = jnp.exp2(level_sums(l))
            for base in range(0, CHUNK, 2 * half):
                lo_rows = slice(base, base + half)
                hi_rows = slice(base + half, base + 2 * half)
                x_cols.append(padded(q[hi_rows] * w[hi_rows], base + half))
                y_cols.append(padded(k[lo_rows] * w[lo_rows], base))
        x_big, y_big = jnp.concatenate(x_cols, axis=1), jnp.concatenate(y_cols, axis=1)
        pairs = [(q.astype(BF16), k.astype(BF16))]
        for l in range(LEVELS - MXU_LEVELS):
            w = jnp.exp2(level_sums(l))
            if (1 << l) < SUBLANES:
                side = jnp.concatenate([jnp.where(upper_half[l], q[r:r + SUBLANES], k[r:r + SUBLANES])
                                        for r in range(0, CHUNK, SUBLANES)], axis=0)
            else:
                side = jnp.concatenate([q[r:r + SUBLANES] if (r >> l) & 1 else k[r:r + SUBLANES]
                                        for r in range(0, CHUNK, SUBLANES)], axis=0)
            tl = (w * side).astype(BF16)
            pairs.append((tl, tl))
        b_end = b[CHUNK - 1:CHUNK]
        k_dec = (k * jnp.exp2(b_end - b)).astype(BF16)
        st_new = st * jnp.exp2(b_end)
        yield
        o_inter = lax.dot_general(q_dec, st16, nt_dims, preferred_element_type=F32)
        a_big = lax.dot_general(x_big, y_big, nt_dims, preferred_element_type=F32)
        a_small = [lax.dot_general(x, y, nt_dims, preferred_element_type=F32) for x, y in pairs]
        st_add = lax.dot_general(v, k_dec, tn_dims, preferred_element_type=F32)
        yield
        small = jnp.where(diag_mask, a_small[0], 0.0)
        for l in range(LEVELS - MXU_LEVELS):
            small = jnp.where(small_masks[l], a_small[1 + l], small)
        a_mat = (a_big + small).astype(BF16)
        st_ref[h] = st_new + st_add
        yield
        o_intra = jnp.dot(a_mat, v, preferred_element_type=F32)
        yield
        o = o_inter + o_intra
        y = o * lax.rsqrt(jnp.mean(o * o, axis=-1, keepdims=True) + EPS) * ng_ref[:, cols]
        ga = g_ref[0, rows, cols].astype(F32)
        za = z_ref[0, rows, cols].astype(F32)
        u = y * jax.nn.sigmoid(ga) * (za * jax.nn.sigmoid(za))
        u_ref[0, rows, cols] = u.astype(BF16)

    for h0 in range(0, HGRN_HEADS, HEADS_PER_LOOP):
        def chunk(c, carry, h0=h0):
            rows = pl.ds(pl.multiple_of(c * CHUNK, CHUNK), CHUNK)
            stages = [head_chunk(h, rows) for h in range(h0, h0 + HEADS_PER_LOOP)]
            splits = [next(s) for s in stages]
            sums = [jnp.dot(mz_ref[...], jnp.concatenate(splits[i:i + 2], axis=1), preferred_element_type=F32)
                    for i in range(0, HEADS_PER_LOOP, 2)]
            for i, s in enumerate(stages):
                s.send(sums[i // 2][:, (i % 2) * HGRN_DK:(i % 2 + 1) * HGRN_DK])
            while stages:
                stages = [s for s in stages if next(s, "done") != "done"]
            return carry

        lax.fori_loop(0, n_chunks, chunk, 0)

    @pl.when(t == pl.num_programs(1) - 1)
    def _():
        for h in range(HGRN_HEADS):
            sout_ref[0, h] = st_ref[h].T


def _decay_sum_matrix():
    m = np.zeros((1 + LEVELS, CHUNK, CHUNK), np.float32)
    idx = np.arange(CHUNK)
    m[0] = idx[:, None] >= idx[None, :]
    for i, l in enumerate(range(LEVELS - 1, -1, -1)):
        half = 1 << l
        for r in range(CHUNK):
            pos = r % (2 * half)
            first_upper = r - pos + half
            if pos >= half:
                m[1 + i, r, first_upper:r + 1] = 1.0
            else:
                m[1 + i, r, r + 1:first_upper] = 1.0
    m = m.reshape(-1, CHUNK)
    return jnp.asarray(np.concatenate([m, m, m], axis=1), dtype=BF16)


def _hgrn(pf, pb, lb, ng, s0, *, tt):
    bsz, t, _ = pb.shape

    def col(c0):
        return pl.BlockSpec((1, tt, D_A), lambda b, j, c0=c0: (b, j, c0 // D_A))

    vec = pl.BlockSpec((1, D_A), lambda b, j: (0, 0))
    state = pl.BlockSpec((1, HGRN_HEADS, HGRN_DK, HGRN_DV), lambda b, j: (b, 0, 0, 0))
    mz = _decay_sum_matrix()
    return pl.pallas_call(
        functools.partial(_hgrn_kernel, n_chunks=tt // CHUNK),
        grid=(bsz, t // tt),
        in_specs=[col(COL_QA), col(0), col(COL_IA), col(COL_GA), col(COL_ZA), vec, vec, state,
                  pl.BlockSpec(mz.shape, lambda b, j: (0, 0))],
        out_specs=[pl.BlockSpec((1, tt, D_A), lambda b, j: (b, j, 0)), state],
        out_shape=[jax.ShapeDtypeStruct((bsz, t, D_A), BF16),
                   jax.ShapeDtypeStruct((bsz, HGRN_HEADS, HGRN_DK, HGRN_DV), F32)],
        scratch_shapes=[pltpu.VMEM((HGRN_HEADS, HGRN_DV, HGRN_DK), F32)],
        compiler_params=_cparams(("arbitrary", "arbitrary")),
        name="hgrn2",
    )(pb, pf, pb, pb, pb, lb, ng, s0, mz)


def _rope(x, cos, sin_lo, sin_hi):
    half = ROT_DIM // 2
    return x * cos + pltpu.roll(x, LANES - half, 1) * sin_lo + pltpu.roll(x, half, 1) * sin_hi


def _attn_kernel(sink_ref, q_ref, kv_ref, z_ref, cos_ref, slo_ref, shi_ref, ck_ref, cv_ref,
                 u_ref, ko_ref, vo_ref, kbuf, vbuf, *, tt, masked, outw):
    t = pl.program_id(1)
    last = t == pl.num_programs(1) - 1
    n_chunks = tt // CHUNK
    n_slabs = KV_W // LANES

    def store_band(buf, r0, x, lanes):
        buf[r0:r0 + x.shape[0], lanes] = x.astype(BF16)
        buf[r0:r0 + x.shape[0], slice(KV_W + lanes.start, KV_W + lanes.stop)] = (
            pltpu.roll(x, HEAD_DIM, 1).astype(BF16))

    @pl.when(t == 0)
    def _():
        for j in range(n_slabs):
            lanes = slice(j * LANES, (j + 1) * LANES)
            store_band(kbuf, 0, ck_ref[0, :, lanes], lanes)
            store_band(vbuf, 0, cv_ref[0, :, lanes], lanes)

    if tt >= WINDOW:
        @pl.when(t > 0)
        def _():
            kbuf[0:WINDOW, :] = kbuf[tt:tt + WINDOW, :]
            vbuf[0:WINDOW, :] = vbuf[tt:tt + WINDOW, :]

    for j in range(n_slabs):
        lanes = slice(j * LANES, (j + 1) * LANES)
        kr = _rope(kv_ref[0, :, lanes].astype(F32), cos_ref[...], slo_ref[...], shi_ref[...])
        store_band(kbuf, WINDOW, kr, lanes)
        store_band(vbuf, WINDOW, kv_ref[0, :, slice(KV_W + lanes.start, KV_W + lanes.stop)].astype(F32), lanes)

        @pl.when(last)
        def _(kr=kr, lanes=lanes):
            ko_ref[0, :, lanes] = kr[tt - outw:, :]

    @pl.when(last)
    def _():
        vo_ref[0] = kv_ref[0, tt - outw:tt, KV_W:2 * KV_W].astype(F32)

    lane = lax.broadcasted_iota(jnp.int32, (CHUNK, LANES), 1)
    low = lane < HEAD_DIM
    heads_per_slab = LANES // HEAD_DIM * GROUP

    def kv_in_low_half(p):
        return (2 * p) // GROUP == 0

    def matched_head(p):
        return 0 if kv_in_low_half(p) else 1

    sink_rows = []
    for j in range(n_slabs):
        order = ([heads_per_slab * j + 2 * p + matched_head(p) for p in range(GROUP)]
                 + [heads_per_slab * j + 2 * p + 1 - matched_head(p) for p in range(GROUP)])
        sink_rows.append(jnp.concatenate([jnp.full((1, CHUNK), sink_ref[h] * LOG2E, F32) for h in order],
                                         axis=1))
    nt_dims = (((1,), (1,)), ((), ()))
    tn_dims = (((0,), (0,)), ((), ()))
    half_cols = GROUP * CHUNK

    def chunk(c, carry):
        r0 = pl.multiple_of(c * CHUNK, CHUNK)
        rows = pl.ds(r0, CHUNK)
        band = pl.ds(r0, BAND)
        cq, sl, sh = cos_ref[rows, :], slo_ref[rows, :], shi_ref[rows, :]
        seq_chunk = t * n_chunks + c

        def slab_chunk(j):
            plain = slice(j * LANES, (j + 1) * LANES)
            swapped = slice(KV_W + j * LANES, KV_W + (j + 1) * LANES)
            k_plain, k_swap = kbuf[band, plain], kbuf[band, swapped]
            v_plain, v_swap = vbuf[band, plain], vbuf[band, swapped]
            q_plain, q_swap = [], []
            for p in range(GROUP):
                slab = GROUP * j + p
                x = _rope(q_ref[0, rows, slab * LANES:(slab + 1) * LANES].astype(F32), cq, sl, sh) * (
                    ATTN_SCALE * LOG2E)
                halves = [jnp.where(low, x, 0.0).astype(BF16), jnp.where(low, 0.0, x).astype(BF16)]
                q_plain.append(halves[matched_head(p)])
                q_swap.append(halves[1 - matched_head(p)])
            q_plain = jnp.concatenate(q_plain, axis=0)
            q_swap = jnp.concatenate(q_swap, axis=0)
            yield
            s = jnp.concatenate([lax.dot_general(k_plain, q_plain, nt_dims, preferred_element_type=F32),
                                 lax.dot_general(k_swap, q_swap, nt_dims, preferred_element_type=F32)],
                                axis=1)
            yield
            if masked:
                blocks = [jnp.where(seq_chunk >= N_LOOKBACK_CHUNKS - i, s[i * CHUNK:(i + 1) * CHUNK], NEG)
                          for i in range(N_LOOKBACK_CHUNKS)]
                s = jnp.concatenate(blocks + [s[WINDOW:]], axis=0)
            sink = sink_rows[j]
            m = jnp.maximum(jnp.max(s, axis=0, keepdims=True), sink)
            pr = jnp.exp2(s - m)
            inv = 1.0 / (jnp.sum(pr, axis=0, keepdims=True) + jnp.exp2(sink - m))
            pr16 = pr.astype(BF16)
            yield
            o_plain = lax.dot_general(v_plain, pr16[:, :half_cols], tn_dims, preferred_element_type=F32)
            o_swap = lax.dot_general(v_swap, pr16[:, half_cols:], tn_dims, preferred_element_type=F32)
            yield
            o_plain = o_plain * inv[:, :half_cols]
            o_swap = o_swap * inv[:, half_cols:]
            for pair in range(GROUP // 2):
                t_plain = o_plain[:, pair * LANES:(pair + 1) * LANES].T
                t_swap = o_swap[:, pair * LANES:(pair + 1) * LANES].T
                for i in range(2):
                    p = 2 * pair + i
                    slab = GROUP * j + p
                    qrows = slice(i * CHUNK, (i + 1) * CHUNK)
                    if matched_head(p) == 0:
                        ob = jnp.where(low, t_plain[qrows], t_swap[qrows])
                    else:
                        ob = jnp.where(low, t_swap[qrows], t_plain[qrows])
                    zb = z_ref[0, rows, slab * LANES:(slab + 1) * LANES].astype(F32)
                    u_ref[0, rows, slab * LANES:(slab + 1) * LANES] = (
                        ob * (zb * jax.nn.sigmoid(zb))).astype(BF16)

        stages = [slab_chunk(j) for j in range(n_slabs)]
        while stages:
            stages = [s for s in stages if next(s, "done") != "done"]
        return carry

    lax.fori_loop(0, n_chunks, chunk, 0)


def _rope_tables(pos):
    half = ROT_DIM // 2
    inv = (ROPE_THETA ** (-np.arange(0, ROT_DIM, 2) / ROT_DIM)).astype(np.float32)
    ang = pos.astype(F32)[:, None] * inv[None, :]
    cos, sin = jnp.cos(ang), jnp.sin(ang)
    n = pos.shape[0]
    rest = HEAD_DIM - ROT_DIM
    cos_h = jnp.concatenate([cos, cos, jnp.ones((n, rest), F32)], axis=1)
    slo_h = jnp.concatenate([-sin, jnp.zeros((n, half + rest), F32)], axis=1)
    shi_h = jnp.concatenate([jnp.zeros((n, half), F32), sin, jnp.zeros((n, rest), F32)], axis=1)
    reps = LANES // HEAD_DIM
    return tuple(jnp.tile(a, (1, reps)) for a in (cos_h, slo_h, shi_h))


def _attention(pb, sinks, tables, cache_k, cache_v, *, tt, masked, outw):
    bsz, t, _ = pb.shape
    tab = pl.BlockSpec((tt, LANES), lambda b, j, s: (j, 0))
    cache = pl.BlockSpec((1, WINDOW, KV_W), lambda b, j, s: (b, 0, 0))
    win = pl.BlockSpec((1, outw, KV_W), lambda b, j, s: (b, 0, 0))
    grid_spec = pltpu.PrefetchScalarGridSpec(
        num_scalar_prefetch=1,
        grid=(bsz, t // tt),
        in_specs=[pl.BlockSpec((1, tt, D_B), lambda b, j, s: (b, j, COL_QB // D_B)),
                  pl.BlockSpec((1, tt, 2 * KV_W), lambda b, j, s: (b, j, COL_KV // (2 * KV_W))),
                  pl.BlockSpec((1, tt, D_B), lambda b, j, s: (b, j, COL_ZB // D_B)),
                  tab, tab, tab, cache, cache],
        out_specs=[pl.BlockSpec((1, tt, D_B), lambda b, j, s: (b, j, 0)), win, win],
        scratch_shapes=[pltpu.VMEM((WINDOW + tt, 2 * KV_W), BF16), pltpu.VMEM((WINDOW + tt, 2 * KV_W), BF16)],
    )
    return pl.pallas_call(
        functools.partial(_attn_kernel, tt=tt, masked=masked, outw=outw),
        grid_spec=grid_spec,
        out_shape=[jax.ShapeDtypeStruct((bsz, t, D_B), BF16),
                   jax.ShapeDtypeStruct((bsz, outw, KV_W), F32),
                   jax.ShapeDtypeStruct((bsz, outw, KV_W), F32)],
        compiler_params=_cparams(("arbitrary", "arbitrary")),
        name="swa_attention",
    )(sinks, pb, pb, pb, *tables, cache_k, cache_v)


def _out_kernel(ua_ref, ub_ref, ma_ref, mb_ref, x_ref, gate_ref, wpa_ref, wpb_ref, wout_ref, fg_ref,
                y_ref, *, nb, tt, final):
    m = nb * tt
    pa = jnp.dot(ua_ref[...].reshape(m, D_A), wpa_ref[...], preferred_element_type=F32)
    pb = jnp.dot(ub_ref[...].reshape(m, D_B), wpb_ref[...], preferred_element_type=F32)
    ma = ma_ref[...].reshape(m, D_MODEL).astype(F32)
    mb = mb_ref[...].reshape(m, D_MODEL).astype(F32)
    merged = jax.nn.sigmoid(ma) * pa + jax.nn.sigmoid(mb) * pb
    o = jnp.dot(merged.astype(BF16), wout_ref[...], preferred_element_type=F32)
    y = x_ref[...] + gate_ref[...] * o.reshape(nb, tt, D_MODEL)
    if final:
        y = (y * lax.rsqrt(jnp.mean(y * y, axis=-1, keepdims=True) + EPS)) * fg_ref[...]
    y_ref[...] = y


def _output(ua, ub, pb, x, gate, wpa, wpb, wout, final_g, *, layer, nb, tt, final):
    bsz, t, _ = x.shape
    tok = lambda i, j: (i, j, 0)
    const = lambda i, j: (0, 0)
    of_layer = lambda i, j: (layer, 0, 0)
    single = pl.Buffered(1)
    return pl.pallas_call(
        functools.partial(_out_kernel, nb=nb, tt=tt, final=final),
        grid=(bsz // nb, t // tt),
        in_specs=[pl.BlockSpec((nb, tt, D_A), tok),
                  pl.BlockSpec((nb, tt, D_B), tok),
                  pl.BlockSpec((nb, tt, D_MODEL), lambda i, j: (i, j, COL_MA // D_MODEL)),
                  pl.BlockSpec((nb, tt, D_MODEL), lambda i, j: (i, j, COL_MB // D_MODEL)),
                  pl.BlockSpec((nb, tt, D_MODEL), tok),
                  pl.BlockSpec((nb, 1, D_MODEL), lambda i, j: (i, 0, 0)),
                  pl.BlockSpec((None, D_A, D_MODEL), of_layer, pipeline_mode=single),
                  pl.BlockSpec((None, D_B, D_MODEL), of_layer, pipeline_mode=single),
                  pl.BlockSpec((None, D_MODEL, D_MODEL), of_layer, pipeline_mode=single),
                  pl.BlockSpec((1, D_MODEL), const)],
        out_specs=pl.BlockSpec((nb, tt, D_MODEL), tok),
        out_shape=jax.ShapeDtypeStruct((bsz, t, D_MODEL), F32),
        compiler_params=_cparams(("arbitrary", "arbitrary")),
        name="merge_output",
    )(ua, ub, pb, pb, x, gate, wpa, wpb, wout, final_g)


def _tiles(t):
    if t >= 1024:
        return dict(proj=(1, min(t, 2048)), out=(1, 512), hgrn_tt=min(t, 1024), attn_tt=min(t, 1024))
    return dict(proj=(16, t), out=(4, t), hgrn_tt=t, attn_tt=t)


def _layer(x, mod, w, lb, s0, cache_k, cache_v, tables, *, final, final_g):
    bsz, t, _ = x.shape
    tl = _tiles(t)
    shift, scale, gate = (mod[:, None, i * D_MODEL:(i + 1) * D_MODEL] for i in range(3))
    nb, tt = tl["proj"]
    pf, pb = _in_projection(x, shift, scale, w["norm_g"], w["w_in"], layer=w["layer"], nb=min(nb, bsz), tt=tt)
    ua, s_new = _hgrn(pf, pb, lb, w["hgrn_g"], s0, tt=tl["hgrn_tt"])
    masked = cache_k is None
    if masked:
        cache_k = jnp.zeros((bsz, WINDOW, KV_W), F32)
        cache_v = cache_k
    outw = min(t, WINDOW)
    ub, k_new, v_new = _attention(pb, w["sinks"], tables, cache_k, cache_v,
                                  tt=tl["attn_tt"], masked=masked, outw=outw)
    nb, tt = tl["out"]
    y = _output(ua, ub, pb, x, gate, w["w_pa"], w["w_pb"], w["w_out"], final_g,
                layer=w["layer"], nb=min(nb, bsz), tt=tt, final=final)
    shape = (bsz, outw, N_KV, HEAD_DIM)
    return y, s_new, k_new.reshape(shape), v_new.reshape(shape)


def kernel(x_prompt, x_sample, c_prompt, c_sample, state_hgrn, cache_win_k, cache_win_v, ada_w, ada_b,
           norm_g, w_in, lb_logits, hgrn_norm_g, sinks, w_branch_a, w_branch_b, w_out, final_norm_g):
    bp, tp = x_prompt.shape[0], x_prompt.shape[1]
    bs, ts = x_sample.shape[0], x_sample.shape[1]

    w_in_p = w_in.astype(BF16)
    w_pa = w_branch_a.astype(BF16)
    w_pb = w_branch_b.astype(BF16)
    w_o = w_out.astype(BF16)
    prob = jax.nn.softmax(lb_logits.astype(F32), axis=0)
    lb = jnp.cumsum(prob, axis=0) - prob[:1]
    final_g = final_norm_g.reshape(1, D_MODEL)

    mod = _modulation(jnp.concatenate([c_prompt, c_sample], axis=0), ada_w, ada_b)
    tab_p = _rope_tables(jnp.arange(tp))
    tab_s = _rope_tables(PAST_LEN + jnp.arange(ts))
    zero_state = jnp.zeros((bp, HGRN_HEADS, HGRN_DK, HGRN_DV), F32)

    hp, hs = x_prompt, x_sample
    outs = [[] for _ in range(6)]
    for l in range(DEPTH):
        w = dict(norm_g=norm_g[l].reshape(1, D_MODEL), w_in=w_in_p, hgrn_g=hgrn_norm_g[l].reshape(1, D_A),
                 sinks=sinks[l], w_pa=w_pa, w_pb=w_pb, w_out=w_o, layer=l)
        lbl = lb[l].reshape(1, D_A)
        final = l == DEPTH - 1
        hp, s_p, k_p, v_p = _layer(hp, mod[l, :bp], w, lbl, zero_state, None, None, tab_p,
                                   final=final, final_g=final_g)
        hs, s_s, k_s, v_s = _layer(hs, mod[l, bp:], w, lbl, state_hgrn[l],
                                   cache_win_k[l].reshape(bs, WINDOW, KV_W),
                                   cache_win_v[l].reshape(bs, WINDOW, KV_W), tab_s,
                                   final=final, final_g=final_g)
        for acc, val in zip(outs, (s_p, k_p, v_p, s_s, k_s, v_s)):
            acc.append(val)
    return (hp, hs) + tuple(jnp.stack(o) for o in outs)
```

```python
import functools

import jax
import jax.numpy as jnp
import numpy as np
from jax import lax
from jax.experimental import pallas as pl
from jax.experimental.pallas import tpu as pltpu

F32 = jnp.float32
BF16 = jnp.bfloat16

D_MODEL = 2048
DEPTH = 4
PAST_LEN = 4096
CHUNK = 64
D_A = D_MODEL // 2
HGRN_DK = 128
HGRN_HEADS = D_A // HGRN_DK
HGRN_DV = D_A // HGRN_HEADS
D_B = D_MODEL // 2
HEAD_DIM = 64
N_Q = D_B // HEAD_DIM
N_KV = N_Q // 4
GROUP = N_Q // N_KV
WINDOW = 128
N_LOOKBACK_CHUNKS = WINDOW // CHUNK
BAND = WINDOW + CHUNK
ROT_DIM = HEAD_DIM // 4
ROPE_THETA = 500000.0
ATTN_SCALE = HEAD_DIM ** -0.5
EPS = 1e-6
NEG = -1e30
LOG2E = 1.4426950408889634
N_IN = 5 * D_A + N_Q * HEAD_DIM + 2 * N_KV * HEAD_DIM + D_B + 2 * D_MODEL

LANES = 128
SUBLANES = 8
KV_W = N_KV * HEAD_DIM
VMEM_LIMIT = 58 * 1024 * 1024

N_F32 = D_A
N_B16 = N_IN - N_F32
COL_MA, COL_MB = 0, D_MODEL
COL_QA, COL_IA, COL_GA, COL_ZA = (2 * D_MODEL + i * D_A for i in range(4))
COL_QB = COL_ZA + D_A
COL_ZB = COL_QB + D_B
COL_KV = COL_ZB + D_B
PROJ_TN = 512
PROJ_ROWS = 512


def _proj_tile_order():
    sizes = dict(qa=D_A, fa=D_A, ia=D_A, ga=D_A, za=D_A, qb=D_B, kv=2 * KV_W, zb=D_B, ma=D_MODEL, mb=D_MODEL)
    start, first = 0, {}
    for name, n in sizes.items():
        first[name] = start // PROJ_TN
        start += n
    order = []
    for name in ("fa", "ma", "mb", "qa", "ia", "ga", "za", "qb", "zb", "kv"):
        order += range(first[name], first[name] + sizes[name] // PROJ_TN)
    return np.asarray(order, np.int32)


def _sigmoid(x):
    return 0.5 * jnp.tanh(0.5 * x) + 0.5


def _cparams(sem):
    return pltpu.CompilerParams(dimension_semantics=sem, vmem_limit_bytes=VMEM_LIMIT)


def _mod_kernel(c_ref, w_ref, b_ref, o_ref):
    acc = jnp.dot(c_ref[...].astype(BF16), w_ref[0].astype(BF16), preferred_element_type=F32)
    o_ref[0] = acc + b_ref[0]


def _modulation(c_all, ada_w, ada_b):
    n, d3, tn = c_all.shape[0], 3 * D_MODEL, 768
    return pl.pallas_call(
        _mod_kernel,
        grid=(DEPTH, d3 // tn),
        in_specs=[pl.BlockSpec((n, D_MODEL), lambda l, j: (0, 0)),
                  pl.BlockSpec((1, D_MODEL, tn), lambda l, j: (l, 0, j)),
                  pl.BlockSpec((1, 1, tn), lambda l, j: (l, 0, j))],
        out_specs=pl.BlockSpec((1, n, tn), lambda l, j: (l, 0, j)),
        out_shape=jax.ShapeDtypeStruct((DEPTH, n, d3), F32),
        compiler_params=_cparams(("arbitrary", "arbitrary")),
        name="adaln_mod",
    )(c_all, ada_w, ada_b.reshape(DEPTH, 1, d3))


def _inproj_kernel(src_ref, x_ref, shift_ref, scale_ref, g_ref, w_ref, of_ref, ob_ref, h_ref, *,
                   nb, tt, rc, nf):
    del src_ref
    n = pl.program_id(2)
    m = nb * tt
    step = min(m, PROJ_ROWS)

    def normalise(r0, r1):
        for r in range(r0, r1, rc):
            b, q = divmod(r, tt)
            x = x_ref[b, q:q + rc, :]
            y = x * lax.rsqrt(jnp.mean(x * x, axis=-1, keepdims=True) + EPS)
            h = (y * g_ref[...]) * (1.0 + scale_ref[b]) + shift_ref[b]
            h_ref[r:r + rc, :] = h.astype(BF16)

    def produce(o_ref, with_norm):
        for r in range(0, m, step):
            if with_norm:
                normalise(r, r + step)
            acc = jnp.dot(h_ref[r:r + step, :], w_ref[...], preferred_element_type=F32).astype(o_ref.dtype)
            if nb == 1:
                o_ref[0, r:r + step, :] = acc
            else:
                o_ref[r // tt:(r + step) // tt] = acc.reshape(step // tt, tt, acc.shape[-1])

    pl.when(n == 0)(functools.partial(produce, of_ref, True))
    pl.when(jnp.logical_and(n > 0, n < nf))(functools.partial(produce, of_ref, False))
    pl.when(n >= nf)(functools.partial(produce, ob_ref, False))


def _in_projection(x, shift, scale, norm_g, w, *, layer, nb, tt):
    bsz, t, _ = x.shape
    tn = PROJ_TN
    rc = min(tt, 128)
    nf = N_F32 // tn
    kern = functools.partial(_inproj_kernel, nb=nb, tt=tt, rc=rc, nf=nf)
    grid_spec = pltpu.PrefetchScalarGridSpec(
        num_scalar_prefetch=1,
        grid=(bsz // nb, t // tt, N_IN // tn),
        in_specs=[pl.BlockSpec((nb, tt, D_MODEL), lambda i, j, n, src: (i, j, 0)),
                  pl.BlockSpec((nb, 1, D_MODEL), lambda i, j, n, src: (i, 0, 0)),
                  pl.BlockSpec((nb, 1, D_MODEL), lambda i, j, n, src: (i, 0, 0)),
                  pl.BlockSpec((1, D_MODEL), lambda i, j, n, src: (0, 0)),
                  pl.BlockSpec((None, D_MODEL, tn), lambda i, j, n, src: (layer, 0, src[n]))],
        out_specs=[pl.BlockSpec((nb, tt, tn), lambda i, j, n, src: (i, j, jnp.minimum(n, nf - 1)),
                                pipeline_mode=pl.Buffered(1)),
                   pl.BlockSpec((nb, tt, tn), lambda i, j, n, src: (i, j, jnp.maximum(n - nf, 0)))],
        scratch_shapes=[pltpu.VMEM((nb * tt, D_MODEL), BF16)],
    )
    return pl.pallas_call(
        kern,
        grid_spec=grid_spec,
        out_shape=[jax.ShapeDtypeStruct((bsz, t, N_F32), F32),
                   jax.ShapeDtypeStruct((bsz, t, N_B16), BF16)],
        compiler_params=_cparams(("arbitrary", "arbitrary", "arbitrary")),
        name="in_projection",
    )(jnp.asarray(_proj_tile_order()), x, shift, scale, norm_g, w)


HEADS_PER_LOOP = 8
LEVELS = CHUNK.bit_length() - 1
MXU_LEVELS = 2


def _hgrn_kernel(q_ref, f_ref, i_ref, g_ref, z_ref, lb_ref, ng_ref, s0_ref, mz_ref,
                 u_ref, sout_ref, st_ref, *, n_chunks):
    t = pl.program_id(1)

    @pl.when(t == 0)
    def _():
        for h in range(HGRN_HEADS):
            st_ref[h] = s0_ref[0, h].T

    row = lax.broadcasted_iota(jnp.int32, (CHUNK, CHUNK), 0)
    col = lax.broadcasted_iota(jnp.int32, (CHUNK, CHUNK), 1)
    code = jnp.where(row > col, row ^ col, jnp.where(row == col, 0, 2 * CHUNK))
    small_masks = [jnp.right_shift(code, l) == 1 for l in range(LEVELS - MXU_LEVELS)]
    diag_mask = code == 0
    sub_row = lax.broadcasted_iota(jnp.int32, (SUBLANES, HGRN_DK), 0)
    upper_half = [(jnp.right_shift(sub_row, l) & 1) == 1 for l in range(SUBLANES.bit_length() - 1)]
    nt_dims = (((1,), (1,)), ((), ()))
    tn_dims = (((0,), (0,)), ((), ()))

    def padded(x, r0):
        parts = []
        if r0:
            parts.append(jnp.zeros((r0, HGRN_DK), F32))
        parts.append(x)
        if CHUNK - r0 - x.shape[0]:
            parts.append(jnp.zeros((CHUNK - r0 - x.shape[0], HGRN_DK), F32))
        return jnp.concatenate(parts, axis=0).astype(BF16)

    def head_chunk(h, rows):
        cols = slice(h * HGRN_DK, (h + 1) * HGRN_DK)
        lb = lb_ref[:, cols]
        a = f_ref[0, rows, cols]
        qa = q_ref[0, rows, cols].astype(F32)
        v = i_ref[0, rows, cols]
        a2 = a * LOG2E
        e = jnp.exp2(-a2)
        l1e = jnp.log(1.0 + e) * LOG2E
        log_f = jnp.log(1.0 + lb * e) * LOG2E - l1e
        k = jnp.exp2((jnp.log(1.0 - lb) * LOG2E - a2) - l1e)
        q = qa * _sigmoid(qa)

        hi = log_f.astype(BF16)
        r1 = log_f - hi.astype(F32)
        mid = r1.astype(BF16)
        lo = (r1 - mid.astype(F32)).astype(BF16)
        zall = yield jnp.concatenate([hi, mid, lo], axis=0)
        b = zall[:CHUNK]

        def level_sums(l):
            i = LEVELS - l
            return zall[i * CHUNK:(i + 1) * CHUNK]

        st = st_ref[h]
        q_dec = (q * jnp.exp2(b)).astype(BF16)
        st16 = st.astype(BF16)
        x_cols, y_cols = [], []
        for l in range(LEVELS - 1, LEVELS - 1 - MXU_LEVELS, -1):
            half = 1 << l
            w = jnp.exp2(level_sums(l))
            for base in range(0, CHUNK, 2 * half):
                lo_rows = slice(base, base + half)
                hi_rows = slice(base + half, base + 2 * half)
                x_cols.append(padded(q[hi_rows] * w[hi_rows], base + half))
                y_cols.append(padded(k[lo_rows] * w[lo_rows], base))
        x_big, y_big = jnp.concatenate(x_cols, axis=1), jnp.concatenate(y_cols, axis=1)
        pairs = [(q.astype(BF16), k.astype(BF16))]
        for l in range(LEVELS - MXU_LEVELS):
            w = jnp.exp2(level_sums(l))
            if (1 << l) < SUBLANES:
                side = jnp.concatenate([jnp.where(upper_half[l], q[r:r + SUBLANES], k[r:r + SUBLANES])
                                        for r in range(0, CHUNK, SUBLANES)], axis=0)
            else:
                side = jnp.concatenate([q[r:r + SUBLANES] if (r >> l) & 1 else k[r:r + SUBLANES]
                                        for r in range(0, CHUNK, SUBLANES)], axis=0)
            tl = (w * side).astype(BF16)
            pairs.append((tl, tl))
        b_end = b[CHUNK - 1:CHUNK]
        k_dec = (k * jnp.exp2(b_end - b)).astype(BF16)
        st_new = st * jnp.exp2(b_end)
        yield
        o_inter = lax.dot_general(q_dec, st16, nt_dims, preferred_element_type=F32)
        a_big = lax.dot_general(x_big, y_big, nt_dims, preferred_element_type=F32)
        a_small = [lax.dot_general(x, y, nt_dims, preferred_element_type=F32) for x, y in pairs]
        st_add = lax.dot_general(v, k_dec, tn_dims, preferred_element_type=F32)
        yield
        small = jnp.where(diag_mask, a_small[0], 0.0)
        for l in range(LEVELS - MXU_LEVELS):
            small = jnp.where(small_masks[l], a_small[1 + l], small)
        a_mat = (a_big + small).astype(BF16)
        st_ref[h] = st_new + st_add
        yield
        o_intra = jnp.dot(a_mat, v, preferred_element_type=F32)
        yield
        o = o_inter + o_intra
        y = o * lax.rsqrt(jnp.mean(o * o, axis=-1, keepdims=True) + EPS) * ng_ref[:, cols]
        ga = g_ref[0, rows, cols].astype(F32)
        za = z_ref[0, rows, cols].astype(F32)
        u = y * _sigmoid(ga) * (za * _sigmoid(za))
        u_ref[0, rows, cols] = u.astype(BF16)

    for h0 in range(0, HGRN_HEADS, HEADS_PER_LOOP):
        def chunk(c, carry, h0=h0):
            rows = pl.ds(pl.multiple_of(c * CHUNK, CHUNK), CHUNK)
            stages = [head_chunk(h, rows) for h in range(h0, h0 + HEADS_PER_LOOP)]
            splits = [next(s) for s in stages]
            sums = [jnp.dot(mz_ref[...], jnp.concatenate(splits[i:i + 2], axis=1), preferred_element_type=F32)
                    for i in range(0, HEADS_PER_LOOP, 2)]
            for i, s in enumerate(stages):
                s.send(sums[i // 2][:, (i % 2) * HGRN_DK:(i % 2 + 1) * HGRN_DK])
            while stages:
                stages = [s for s in stages if next(s, "done") != "done"]
            return carry

        lax.fori_loop(0, n_chunks, chunk, 0)

    @pl.when(t == pl.num_programs(1) - 1)
    def _():
        for h in range(HGRN_HEADS):
            sout_ref[0, h] = st_ref[h].T


def _decay_sum_matrix():
    m = np.zeros((1 + LEVELS, CHUNK, CHUNK), np.float32)
    idx = np.arange(CHUNK)
    m[0] = idx[:, None] >= idx[None, :]
    for i, l in enumerate(range(LEVELS - 1, -1, -1)):
        half = 1 << l
        for r in range(CHUNK):
            pos = r % (2 * half)
            first_upper = r - pos + half
            if pos >= half:
                m[1 + i, r, first_upper:r + 1] = 1.0
            else:
                m[1 + i, r, r + 1:first_upper] = 1.0
    m = m.reshape(-1, CHUNK)
    return jnp.asarray(np.concatenate([m, m, m], axis=1), dtype=BF16)


def _hgrn(pf, pb, lb, ng, s0, *, tt):
    bsz, t, _ = pb.shape

    def col(c0):
        return pl.BlockSpec((1, tt, D_A), lambda b, j, c0=c0: (b, j, c0 // D_A))

    vec = pl.BlockSpec((1, D_A), lambda b, j: (0, 0))
    state = pl.BlockSpec((1, HGRN_HEADS, HGRN_DK, HGRN_DV), lambda b, j: (b, 0, 0, 0))
    mz = _decay_sum_matrix()
    return pl.pallas_call(
        functools.partial(_hgrn_kernel, n_chunks=tt // CHUNK),
        grid=(bsz, t // tt),
        in_specs=[col(COL_QA), col(0), col(COL_IA), col(COL_GA), col(COL_ZA), vec, vec, state,
                  pl.BlockSpec(mz.shape, lambda b, j: (0, 0))],
        out_specs=[pl.BlockSpec((1, tt, D_A), lambda b, j: (b, j, 0)), state],
        out_shape=[jax.ShapeDtypeStruct((bsz, t, D_A), BF16),
                   jax.ShapeDtypeStruct((bsz, HGRN_HEADS, HGRN_DK, HGRN_DV), F32)],
        scratch_shapes=[pltpu.VMEM((HGRN_HEADS, HGRN_DV, HGRN_DK), F32)],
        compiler_params=_cparams(("arbitrary", "arbitrary")),
        name="hgrn2",
    )(pb, pf, pb, pb, pb, lb, ng, s0, mz)


def _rope(x, cos, sin_lo, sin_hi):
    half = ROT_DIM // 2
    return x * cos + pltpu.roll(x, LANES - half, 1) * sin_lo + pltpu.roll(x, half, 1) * sin_hi


def _attn_kernel(sink_ref, q_ref, kv_ref, z_ref, cos_ref, slo_ref, shi_ref, ck_ref, cv_ref,
                 u_ref, ko_ref, vo_ref, kbuf, vbuf, *, tt, masked, outw):
    t = pl.program_id(1)
    last = t == pl.num_programs(1) - 1
    n_chunks = tt // CHUNK
    n_slabs = KV_W // LANES

    def store_band(buf, r0, x, lanes):
        buf[r0:r0 + x.shape[0], lanes] = x.astype(BF16)
        buf[r0:r0 + x.shape[0], slice(KV_W + lanes.start, KV_W + lanes.stop)] = (
            pltpu.roll(x, HEAD_DIM, 1).astype(BF16))

    @pl.when(t == 0)
    def _():
        for j in range(n_slabs):
            lanes = slice(j * LANES, (j + 1) * LANES)
            store_band(kbuf, 0, ck_ref[0, :, lanes], lanes)
            store_band(vbuf, 0, cv_ref[0, :, lanes], lanes)

    if tt >= WINDOW:
        @pl.when(t > 0)
        def _():
            kbuf[0:WINDOW, :] = kbuf[tt:tt + WINDOW, :]
            vbuf[0:WINDOW, :] = vbuf[tt:tt + WINDOW, :]

    for j in range(n_slabs):
        lanes = slice(j * LANES, (j + 1) * LANES)
        kr = _rope(kv_ref[0, :, lanes].astype(F32), cos_ref[...], slo_ref[...], shi_ref[...])
        store_band(kbuf, WINDOW, kr, lanes)
        store_band(vbuf, WINDOW, kv_ref[0, :, slice(KV_W + lanes.start, KV_W + lanes.stop)].astype(F32), lanes)

        @pl.when(last)
        def _(kr=kr, lanes=lanes):
            ko_ref[0, :, lanes] = kr[tt - outw:, :]

    @pl.when(last)
    def _():
        vo_ref[0] = kv_ref[0, tt - outw:tt, KV_W:2 * KV_W].astype(F32)

    lane = lax.broadcasted_iota(jnp.int32, (CHUNK, LANES), 1)
    low = lane < HEAD_DIM
    heads_per_slab = LANES // HEAD_DIM * GROUP

    def kv_in_low_half(p):
        return (2 * p) // GROUP == 0

    def matched_head(p):
        return 0 if kv_in_low_half(p) else 1

    sink_rows = []
    for j in range(n_slabs):
        order = ([heads_per_slab * j + 2 * p + matched_head(p) for p in range(GROUP)]
                 + [heads_per_slab * j + 2 * p + 1 - matched_head(p) for p in range(GROUP)])
        sink_rows.append(jnp.concatenate([jnp.full((1, CHUNK), sink_ref[h] * LOG2E, F32) for h in order],
                                         axis=1))
    nt_dims = (((1,), (1,)), ((), ()))
    tn_dims = (((0,), (0,)), ((), ()))
    half_cols = GROUP * CHUNK

    def chunk(c, carry):
        r0 = pl.multiple_of(c * CHUNK, CHUNK)
        rows = pl.ds(r0, CHUNK)
        band = pl.ds(r0, BAND)
        cq, sl, sh = cos_ref[rows, :], slo_ref[rows, :], shi_ref[rows, :]
        seq_chunk = t * n_chunks + c

        def slab_chunk(j):
            plain = slice(j * LANES, (j + 1) * LANES)
            swapped = slice(KV_W + j * LANES, KV_W + (j + 1) * LANES)
            k_plain, k_swap = kbuf[band, plain], kbuf[band, swapped]
            v_plain, v_swap = vbuf[band, plain], vbuf[band, swapped]
            q_plain, q_swap = [], []
            for p in range(GROUP):
                slab = GROUP * j + p
                x = _rope(q_ref[0, rows, slab * LANES:(slab + 1) * LANES].astype(F32), cq, sl, sh) * (
                    ATTN_SCALE * LOG2E)
                halves = [jnp.where(low, x, 0.0).astype(BF16), jnp.where(low, 0.0, x).astype(BF16)]
                q_plain.append(halves[matched_head(p)])
                q_swap.append(halves[1 - matched_head(p)])
            q_plain = jnp.concatenate(q_plain, axis=0)
            q_swap = jnp.concatenate(q_swap, axis=0)
            yield
            s = jnp.concatenate([lax.dot_general(k_plain, q_plain, nt_dims, preferred_element_type=F32),
                                 lax.dot_general(k_swap, q_swap, nt_dims, preferred_element_type=F32)],
                                axis=1)
            yield
            if masked:
                blocks = [jnp.where(seq_chunk >= N_LOOKBACK_CHUNKS - i, s[i * CHUNK:(i + 1) * CHUNK], NEG)
                          for i in range(N_LOOKBACK_CHUNKS)]
                s = jnp.concatenate(blocks + [s[WINDOW:]], axis=0)
            sink = sink_rows[j]
            m = jnp.maximum(jnp.max(s, axis=0, keepdims=True), sink)
            pr = jnp.exp2(s - m)
            inv = 1.0 / (jnp.sum(pr, axis=0, keepdims=True) + jnp.exp2(sink - m))
            pr16 = pr.astype(BF16)
            yield
            o_plain = lax.dot_general(v_plain, pr16[:, :half_cols], tn_dims, preferred_element_type=F32)
            o_swap = lax.dot_general(v_swap, pr16[:, half_cols:], tn_dims, preferred_element_type=F32)
            yield
            o_plain = o_plain * inv[:, :half_cols]
            o_swap = o_swap * inv[:, half_cols:]
            for pair in range(GROUP // 2):
                t_plain = o_plain[:, pair * LANES:(pair + 1) * LANES].T
                t_swap = o_swap[:, pair * LANES:(pair + 1) * LANES].T
                for i in range(2):
                    p = 2 * pair + i
                    slab = GROUP * j + p
                    qrows = slice(i * CHUNK, (i + 1) * CHUNK)
                    if matched_head(p) == 0:
                        ob = jnp.where(low, t_plain[qrows], t_swap[qrows])
                    else:
                        ob = jnp.where(low, t_swap[qrows], t_plain[qrows])
                    zb = z_ref[0, rows, slab * LANES:(slab + 1) * LANES].astype(F32)
                    u_ref[0, rows, slab * LANES:(slab + 1) * LANES] = (
                        ob * (zb * _sigmoid(zb))).astype(BF16)

        stages = [slab_chunk(j) for j in range(n_slabs)]
        while stages:
            stages = [s for s in stages if next(s, "done") != "done"]
        return carry

    lax.fori_loop(0, n_chunks, chunk, 0)


def _rope_tables(pos):
    half = ROT_DIM // 2
    inv = (ROPE_THETA ** (-np.arange(0, ROT_DIM, 2) / ROT_DIM)).astype(np.float32)
    ang = pos.astype(F32)[:, None] * inv[None, :]
    cos, sin = jnp.cos(ang), jnp.sin(ang)
    n = pos.shape[0]
    rest = HEAD_DIM - ROT_DIM
    cos_h = jnp.concatenate([cos, cos, jnp.ones((n, rest), F32)], axis=1)
    slo_h = jnp.concatenate([-sin, jnp.zeros((n, half + rest), F32)], axis=1)
    shi_h = jnp.concatenate([jnp.zeros((n, half), F32), sin, jnp.zeros((n, rest), F32)], axis=1)
    reps = LANES // HEAD_DIM
    return tuple(jnp.tile(a, (1, reps)) for a in (cos_h, slo_h, shi_h))


def _attention(pb, sinks, tables, cache_k, cache_v, *, tt, masked, outw):
    bsz, t, _ = pb.shape
    tab = pl.BlockSpec((tt, LANES), lambda b, j, s: (j, 0))
    cache = pl.BlockSpec((1, WINDOW, KV_W), lambda b, j, s: (b, 0, 0))
    win = pl.BlockSpec((1, outw, KV_W), lambda b, j, s: (b, 0, 0))
    grid_spec = pltpu.PrefetchScalarGridSpec(
        num_scalar_prefetch=1,
        grid=(bsz, t // tt),
        in_specs=[pl.BlockSpec((1, tt, D_B), lambda b, j, s: (b, j, COL_QB // D_B)),
                  pl.BlockSpec((1, tt, 2 * KV_W), lambda b, j, s: (b, j, COL_KV // (2 * KV_W))),
                  pl.BlockSpec((1, tt, D_B), lambda b, j, s: (b, j, COL_ZB // D_B)),
                  tab, tab, tab, cache, cache],
        out_specs=[pl.BlockSpec((1, tt, D_B), lambda b, j, s: (b, j, 0)), win, win],
        scratch_shapes=[pltpu.VMEM((WINDOW + tt, 2 * KV_W), BF16), pltpu.VMEM((WINDOW + tt, 2 * KV_W), BF16)],
    )
    return pl.pallas_call(
        functools.partial(_attn_kernel, tt=tt, masked=masked, outw=outw),
        grid_spec=grid_spec,
        out_shape=[jax.ShapeDtypeStruct((bsz, t, D_B), BF16),
                   jax.ShapeDtypeStruct((bsz, outw, KV_W), F32),
                   jax.ShapeDtypeStruct((bsz, outw, KV_W), F32)],
        compiler_params=_cparams(("arbitrary", "arbitrary")),
        name="swa_attention",
    )(sinks, pb, pb, pb, *tables, cache_k, cache_v)


def _out_kernel(ua_ref, ub_ref, ma_ref, mb_ref, x_ref, gate_ref, wpa_ref, wpb_ref, wout_ref, fg_ref,
                y_ref, *, nb, tt, final):
    m = nb * tt
    pa = jnp.dot(ua_ref[...].reshape(m, D_A), wpa_ref[...], preferred_element_type=F32)
    pb = jnp.dot(ub_ref[...].reshape(m, D_B), wpb_ref[...], preferred_element_type=F32)
    ma = ma_ref[...].reshape(m, D_MODEL).astype(F32)
    mb = mb_ref[...].reshape(m, D_MODEL).astype(F32)
    merged = _sigmoid(ma) * pa + _sigmoid(mb) * pb
    o = jnp.dot(merged.astype(BF16), wout_ref[...], preferred_element_type=F32)
    y = x_ref[...] + gate_ref[...] * o.reshape(nb, tt, D_MODEL)
    if final:
        y = (y * lax.rsqrt(jnp.mean(y * y, axis=-1, keepdims=True) + EPS)) * fg_ref[...]
    y_ref[...] = y


def _output(ua, ub, pb, x, gate, wpa, wpb, wout, final_g, *, layer, nb, tt, final):
    bsz, t, _ = x.shape
    tok = lambda i, j: (i, j, 0)
    const = lambda i, j: (0, 0)
    of_layer = lambda i, j: (layer, 0, 0)
    single = pl.Buffered(1)
    return pl.pallas_call(
        functools.partial(_out_kernel, nb=nb, tt=tt, final=final),
        grid=(bsz // nb, t // tt),
        in_specs=[pl.BlockSpec((nb, tt, D_A), tok),
                  pl.BlockSpec((nb, tt, D_B), tok),
                  pl.BlockSpec((nb, tt, D_MODEL), lambda i, j: (i, j, COL_MA // D_MODEL)),
                  pl.BlockSpec((nb, tt, D_MODEL), lambda i, j: (i, j, COL_MB // D_MODEL)),
                  pl.BlockSpec((nb, tt, D_MODEL), tok),
                  pl.BlockSpec((nb, 1, D_MODEL), lambda i, j: (i, 0, 0)),
                  pl.BlockSpec((None, D_A, D_MODEL), of_layer, pipeline_mode=single),
                  pl.BlockSpec((None, D_B, D_MODEL), of_layer, pipeline_mode=single),
                  pl.BlockSpec((None, D_MODEL, D_MODEL), of_layer, pipeline_mode=single),
                  pl.BlockSpec((1, D_MODEL), const)],
        out_specs=pl.BlockSpec((nb, tt, D_MODEL), tok),
        out_shape=jax.ShapeDtypeStruct((bsz, t, D_MODEL), F32),
        compiler_params=_cparams(("arbitrary", "arbitrary")),
        name="merge_output",
    )(ua, ub, pb, pb, x, gate, wpa, wpb, wout, final_g)


def _tiles(t):
    if t >= 1024:
        return dict(proj=(1, min(t, 2048)), out=(1, 512), hgrn_tt=min(t, 1024), attn_tt=min(t, 1024))
    return dict(proj=(16, t), out=(4, t), hgrn_tt=t, attn_tt=t)


def _layer(x, mod, w, lb, s0, cache_k, cache_v, tables, *, final, final_g):
    bsz, t, _ = x.shape
    tl = _tiles(t)
    shift, scale, gate = (mod[:, None, i * D_MODEL:(i + 1) * D_MODEL] for i in range(3))
    nb, tt = tl["proj"]
    pf, pb = _in_projection(x, shift, scale, w["norm_g"], w["w_in"], layer=w["layer"], nb=min(nb, bsz), tt=tt)
    ua, s_new = _hgrn(pf, pb, lb, w["hgrn_g"], s0, tt=tl["hgrn_tt"])
    masked = cache_k is None
    if masked:
        cache_k = jnp.zeros((bsz, WINDOW, KV_W), F32)
        cache_v = cache_k
    outw = min(t, WINDOW)
    ub, k_new, v_new = _attention(pb, w["sinks"], tables, cache_k, cache_v,
                                  tt=tl["attn_tt"], masked=masked, outw=outw)
    nb, tt = tl["out"]
    y = _output(ua, ub, pb, x, gate, w["w_pa"], w["w_pb"], w["w_out"], final_g,
                layer=w["layer"], nb=min(nb, bsz), tt=tt, final=final)
    shape = (bsz, outw, N_KV, HEAD_DIM)
    return y, s_new, k_new.reshape(shape), v_new.reshape(shape)


def kernel(x_prompt, x_sample, c_prompt, c_sample, state_hgrn, cache_win_k, cache_win_v, ada_w, ada_b,
           norm_g, w_in, lb_logits, hgrn_norm_g, sinks, w_branch_a, w_branch_b, w_out, final_norm_g):
    bp, tp = x_prompt.shape[0], x_prompt.shape[1]
    bs, ts = x_sample.shape[0], x_sample.shape[1]

    w_in_p = w_in.astype(BF16)
    w_pa = w_branch_a.astype(BF16)
    w_pb = w_branch_b.astype(BF16)
    w_o = w_out.astype(BF16)
    prob = jax.nn.softmax(lb_logits.astype(F32), axis=0)
    lb = jnp.cumsum(prob, axis=0) - prob[:1]
    final_g = final_norm_g.reshape(1, D_MODEL)

    mod = _modulation(jnp.concatenate([c_prompt, c_sample], axis=0), ada_w, ada_b)
    tab_p = _rope_tables(jnp.arange(tp))
    tab_s = _rope_tables(PAST_LEN + jnp.arange(ts))
    zero_state = jnp.zeros((bp, HGRN_HEADS, HGRN_DK, HGRN_DV), F32)

    hp, hs = x_prompt, x_sample
    outs = [[] for _ in range(6)]
    for l in range(DEPTH):
        w = dict(norm_g=norm_g[l].reshape(1, D_MODEL), w_in=w_in_p, hgrn_g=hgrn_norm_g[l].reshape(1, D_A),
                 sinks=sinks[l], w_pa=w_pa, w_pb=w_pb, w_out=w_o, layer=l)
        lbl = lb[l].reshape(1, D_A)
        final = l == DEPTH - 1
        hp, s_p, k_p, v_p = _layer(hp, mod[l, :bp], w, lbl, zero_state, None, None, tab_p,
                                   final=final, final_g=final_g)
        hs, s_s, k_s, v_s = _layer(hs, mod[l, bp:], w, lbl, state_hgrn[l],
                                   cache_win_k[l].reshape(bs, WINDOW, KV_W),
                                   cache_win_v[l].reshape(bs, WINDOW, KV_W), tab_s,
                                   final=final, final_g=final_g)
        for acc, val in zip(outs, (s_p, k_p, v_p, s_s, k_s, v_s)):
            acc.append(val)
    return (hp, hs) + tuple(jnp.stack(o) for o in outs)
```

```python
import functools

import jax
import jax.numpy as jnp
import numpy as np
from jax import lax
from jax.experimental import pallas as pl
from jax.experimental.pallas import tpu as pltpu

F32 = jnp.float32
BF16 = jnp.bfloat16

D_MODEL = 2048
DEPTH = 4
PAST_LEN = 4096
CHUNK = 64
D_A = D_MODEL // 2
HGRN_DK = 128
HGRN_HEADS = D_A // HGRN_DK
HGRN_DV = D_A // HGRN_HEADS
D_B = D_MODEL // 2
HEAD_DIM = 64
N_Q = D_B // HEAD_DIM
N_KV = N_Q // 4
GROUP = N_Q // N_KV
WINDOW = 128
N_LOOKBACK_CHUNKS = WINDOW // CHUNK
BAND = WINDOW + CHUNK
ROT_DIM = HEAD_DIM // 4
ROPE_THETA = 500000.0
ATTN_SCALE = HEAD_DIM ** -0.5
EPS = 1e-6
NEG = -1e30
LOG2E = 1.4426950408889634
N_IN = 5 * D_A + N_Q * HEAD_DIM + 2 * N_KV * HEAD_DIM + D_B + 2 * D_MODEL

LANES = 128
SUBLANES = 8
KV_W = N_KV * HEAD_DIM
VMEM_LIMIT = 58 * 1024 * 1024

N_F32 = D_A
N_B16 = N_IN - N_F32
COL_MA, COL_MB = 0, D_MODEL
COL_QA, COL_IA, COL_GA, COL_ZA = (2 * D_MODEL + i * D_A for i in range(4))
COL_QB = COL_ZA + D_A
COL_ZB = COL_QB + D_B
COL_KV = COL_ZB + D_B
PROJ_TN = 512
PROJ_ROWS = 512


def _proj_tile_order():
    sizes = dict(qa=D_A, fa=D_A, ia=D_A, ga=D_A, za=D_A, qb=D_B, kv=2 * KV_W, zb=D_B, ma=D_MODEL, mb=D_MODEL)
    start, first = 0, {}
    for name, n in sizes.items():
        first[name] = start // PROJ_TN
        start += n
    order = []
    for name in ("fa", "ma", "mb", "qa", "ia", "ga", "za", "qb", "zb", "kv"):
        order += range(first[name], first[name] + sizes[name] // PROJ_TN)
    return np.asarray(order, np.int32)


def _sigmoid(x):
    return 0.5 * jnp.tanh(0.5 * x) + 0.5


def _cparams(sem):
    return pltpu.CompilerParams(dimension_semantics=sem, vmem_limit_bytes=VMEM_LIMIT)


def _mod_kernel(c_ref, w_ref, b_ref, o_ref):
    acc = jnp.dot(c_ref[...].astype(BF16), w_ref[0].astype(BF16), preferred_element_type=F32)
    o_ref[0] = acc + b_ref[0]


def _modulation(c_all, ada_w, ada_b):
    n, d3, tn = c_all.shape[0], 3 * D_MODEL, 768
    return pl.pallas_call(
        _mod_kernel,
        grid=(DEPTH, d3 // tn),
        in_specs=[pl.BlockSpec((n, D_MODEL), lambda l, j: (0, 0)),
                  pl.BlockSpec((1, D_MODEL, tn), lambda l, j: (l, 0, j)),
                  pl.BlockSpec((1, 1, tn), lambda l, j: (l, 0, j))],
        out_specs=pl.BlockSpec((1, n, tn), lambda l, j: (l, 0, j)),
        out_shape=jax.ShapeDtypeStruct((DEPTH, n, d3), F32),
        compiler_params=_cparams(("arbitrary", "arbitrary")),
        name="adaln_mod",
    )(c_all, ada_w, ada_b.reshape(DEPTH, 1, d3))


def _inproj_kernel(src_ref, x_ref, shift_ref, scale_ref, g_ref, w_ref, of_ref, ob_ref, h_ref, *,
                   nb, tt, rc, nf):
    del src_ref
    n = pl.program_id(2)
    m = nb * tt
    step = min(m, PROJ_ROWS)

    def normalise(r0, r1):
        for r in range(r0, r1, rc):
            b, q = divmod(r, tt)
            x = x_ref[b, q:q + rc, :]
            y = x * lax.rsqrt(jnp.mean(x * x, axis=-1, keepdims=True) + EPS)
            h = (y * g_ref[...]) * (1.0 + scale_ref[b]) + shift_ref[b]
            h_ref[r:r + rc, :] = h.astype(BF16)

    def produce(o_ref, with_norm):
        for r in range(0, m, step):
            if with_norm:
                normalise(r, r + step)
            acc = jnp.dot(h_ref[r:r + step, :], w_ref[...], preferred_element_type=F32).astype(o_ref.dtype)
            if nb == 1:
                o_ref[0, r:r + step, :] = acc
            else:
                o_ref[r // tt:(r + step) // tt] = acc.reshape(step // tt, tt, acc.shape[-1])

    pl.when(n == 0)(functools.partial(produce, of_ref, True))
    pl.when(jnp.logical_and(n > 0, n < nf))(functools.partial(produce, of_ref, False))
    pl.when(n >= nf)(functools.partial(produce, ob_ref, False))


def _in_projection(x, shift, scale, norm_g, w, *, layer, nb, tt):
    bsz, t, _ = x.shape
    tn = PROJ_TN
    rc = min(tt, 128)
    nf = N_F32 // tn
    kern = functools.partial(_inproj_kernel, nb=nb, tt=tt, rc=rc, nf=nf)
    grid_spec = pltpu.PrefetchScalarGridSpec(
        num_scalar_prefetch=1,
        grid=(bsz // nb, t // tt, N_IN // tn),
        in_specs=[pl.BlockSpec((nb, tt, D_MODEL), lambda i, j, n, src: (i, j, 0)),
                  pl.BlockSpec((nb, 1, D_MODEL), lambda i, j, n, src: (i, 0, 0)),
                  pl.BlockSpec((nb, 1, D_MODEL), lambda i, j, n, src: (i, 0, 0)),
                  pl.BlockSpec((1, D_MODEL), lambda i, j, n, src: (0, 0)),
                  pl.BlockSpec((None, D_MODEL, tn), lambda i, j, n, src: (layer, 0, src[n]))],
        out_specs=[pl.BlockSpec((nb, tt, tn), lambda i, j, n, src: (i, j, jnp.minimum(n, nf - 1)),
                                pipeline_mode=pl.Buffered(1)),
                   pl.BlockSpec((nb, tt, tn), lambda i, j, n, src: (i, j, jnp.maximum(n - nf, 0)))],
        scratch_shapes=[pltpu.VMEM((nb * tt, D_MODEL), BF16)],
    )
    return pl.pallas_call(
        kern,
        grid_spec=grid_spec,
        out_shape=[jax.ShapeDtypeStruct((bsz, t, N_F32), F32),
                   jax.ShapeDtypeStruct((bsz, t, N_B16), BF16)],
        compiler_params=_cparams(("arbitrary", "arbitrary", "arbitrary")),
        name="in_projection",
    )(jnp.asarray(_proj_tile_order()), x, shift, scale, norm_g, w)


HEADS_PER_LOOP = 8
LEVELS = CHUNK.bit_length() - 1
MXU_LEVELS = 2


def _hgrn_kernel(q_ref, f_ref, i_ref, g_ref, z_ref, lb_ref, ng_ref, s0_ref, mz_ref,
                 u_ref, sout_ref, st_ref, *, n_chunks):
    t = pl.program_id(1)

    @pl.when(t == 0)
    def _():
        for h in range(HGRN_HEADS):
            st_ref[h] = s0_ref[0, h].T

    row = lax.broadcasted_iota(jnp.int32, (CHUNK, CHUNK), 0)
    col = lax.broadcasted_iota(jnp.int32, (CHUNK, CHUNK), 1)
    code = jnp.where(row > col, row ^ col, jnp.where(row == col, 0, 2 * CHUNK))
    small_masks = [jnp.right_shift(code, l) == 1 for l in range(LEVELS - MXU_LEVELS)]
    diag_mask = code == 0
    sub_row = lax.broadcasted_iota(jnp.int32, (SUBLANES, HGRN_DK), 0)
    upper_half = [(jnp.right_shift(sub_row, l) & 1) == 1 for l in range(SUBLANES.bit_length() - 1)]
    nt_dims = (((1,), (1,)), ((), ()))
    tn_dims = (((0,), (0,)), ((), ()))

    def padded(x, r0):
        parts = []
        if r0:
            parts.append(jnp.zeros((r0, HGRN_DK), F32))
        parts.append(x)
        if CHUNK - r0 - x.shape[0]:
            parts.append(jnp.zeros((CHUNK - r0 - x.shape[0], HGRN_DK), F32))
        return jnp.concatenate(parts, axis=0).astype(BF16)

    def head_chunk(h, rows):
        cols = slice(h * HGRN_DK, (h + 1) * HGRN_DK)
        lb = lb_ref[:, cols]
        a = f_ref[0, rows, cols]
        qa = q_ref[0, rows, cols].astype(F32)
        v = i_ref[0, rows, cols]
        e = jnp.exp2(a * -LOG2E)
        r = 1.0 / (1.0 + e)
        log_f = jnp.log((1.0 + lb * e) * r) * LOG2E
        k = ((1.0 - lb) * e) * r
        q = qa * _sigmoid(qa)

        hi = log_f.astype(BF16)
        r1 = log_f - hi.astype(F32)
        mid = r1.astype(BF16)
        lo = (r1 - mid.astype(F32)).astype(BF16)
        zall = yield jnp.concatenate([hi, mid, lo], axis=0)
        b = zall[:CHUNK]

        def level_sums(l):
            i = LEVELS - l
            return zall[i * CHUNK:(i + 1) * CHUNK]

        st = st_ref[h]
        q_dec = (q * jnp.exp2(b)).astype(BF16)
        st16 = st.astype(BF16)
        x_cols, y_cols = [], []
        for l in range(LEVELS - 1, LEVELS - 1 - MXU_LEVELS, -1):
            half = 1 << l
            w = jnp.exp2(level_sums(l))
            for base in range(0, CHUNK, 2 * half):
                lo_rows = slice(base, base + half)
                hi_rows = slice(base + half, base + 2 * half)
                x_cols.append(padded(q[hi_rows] * w[hi_rows], base + half))
                y_cols.append(padded(k[lo_rows] * w[lo_rows], base))
        x_big, y_big = jnp.concatenate(x_cols, axis=1), jnp.concatenate(y_cols, axis=1)
        pairs = [(q.astype(BF16), k.astype(BF16))]
        for l in range(LEVELS - MXU_LEVELS):
            w = jnp.exp2(level_sums(l))
            if (1 << l) < SUBLANES:
                side = jnp.concatenate([jnp.where(upper_half[l], q[r:r + SUBLANES], k[r:r + SUBLANES])
                                        for r in range(0, CHUNK, SUBLANES)], axis=0)
            else:
                side = jnp.concatenate([q[r:r + SUBLANES] if (r >> l) & 1 else k[r:r + SUBLANES]
                                        for r in range(0, CHUNK, SUBLANES)], axis=0)
            tl = (w * side).astype(BF16)
            pairs.append((tl, tl))
        b_end = b[CHUNK - 1:CHUNK]
        k_dec = (k * jnp.exp2(b_end - b)).astype(BF16)
        st_new = st * jnp.exp2(b_end)
        yield
        o_inter = lax.dot_general(q_dec, st16, nt_dims, preferred_element_type=F32)
        a_big = lax.dot_general(x_big, y_big, nt_dims, preferred_element_type=F32)
        a_small = [lax.dot_general(x, y, nt_dims, preferred_element_type=F32) for x, y in pairs]
        st_add = lax.dot_general(v, k_dec, tn_dims, preferred_element_type=F32)
        yield
        small = jnp.where(diag_mask, a_small[0], 0.0)
        for l in range(LEVELS - MXU_LEVELS):
            small = jnp.where(small_masks[l], a_small[1 + l], small)
        a_mat = (a_big + small).astype(BF16)
        st_ref[h] = st_new + st_add
        yield
        o_intra = jnp.dot(a_mat, v, preferred_element_type=F32)
        yield
        o = o_inter + o_intra
        y = o * lax.rsqrt(jnp.mean(o * o, axis=-1, keepdims=True) + EPS) * ng_ref[:, cols]
        ga = g_ref[0, rows, cols].astype(F32)
        za = z_ref[0, rows, cols].astype(F32)
        u = y * _sigmoid(ga) * (za * _sigmoid(za))
        u_ref[0, rows, cols] = u.astype(BF16)

    for h0 in range(0, HGRN_HEADS, HEADS_PER_LOOP):
        def chunk(c, carry, h0=h0):
            rows = pl.ds(pl.multiple_of(c * CHUNK, CHUNK), CHUNK)
            stages = [head_chunk(h, rows) for h in range(h0, h0 + HEADS_PER_LOOP)]
            splits = [next(s) for s in stages]
            sums = [jnp.dot(mz_ref[...], jnp.concatenate(splits[i:i + 2], axis=1), preferred_element_type=F32)
                    for i in range(0, HEADS_PER_LOOP, 2)]
            for i, s in enumerate(stages):
                s.send(sums[i // 2][:, (i % 2) * HGRN_DK:(i % 2 + 1) * HGRN_DK])
            while stages:
                stages = [s for s in stages if next(s, "done") != "done"]
            return carry

        lax.fori_loop(0, n_chunks, chunk, 0)

    @pl.when(t == pl.num_programs(1) - 1)
    def _():
        for h in range(HGRN_HEADS):
            sout_ref[0, h] = st_ref[h].T


def _decay_sum_matrix():
    m = np.zeros((1 + LEVELS, CHUNK, CHUNK), np.float32)
    idx = np.arange(CHUNK)
    m[0] = idx[:, None] >= idx[None, :]
    for i, l in enumerate(range(LEVELS - 1, -1, -1)):
        half = 1 << l
        for r in range(CHUNK):
            pos = r % (2 * half)
            first_upper = r - pos + half
            if pos >= half:
                m[1 + i, r, first_upper:r + 1] = 1.0
            else:
                m[1 + i, r, r + 1:first_upper] = 1.0
    m = m.reshape(-1, CHUNK)
    return jnp.asarray(np.concatenate([m, m, m], axis=1), dtype=BF16)


def _hgrn(pf, pb, lb, ng, s0, *, tt):
    bsz, t, _ = pb.shape

    def col(c0):
        return pl.BlockSpec((1, tt, D_A), lambda b, j, c0=c0: (b, j, c0 // D_A))

    vec = pl.BlockSpec((1, D_A), lambda b, j: (0, 0))
    state = pl.BlockSpec((1, HGRN_HEADS, HGRN_DK, HGRN_DV), lambda b, j: (b, 0, 0, 0))
    mz = _decay_sum_matrix()
    return pl.pallas_call(
        functools.partial(_hgrn_kernel, n_chunks=tt // CHUNK),
        grid=(bsz, t // tt),
        in_specs=[col(COL_QA), col(0), col(COL_IA), col(COL_GA), col(COL_ZA), vec, vec, state,
                  pl.BlockSpec(mz.shape, lambda b, j: (0, 0))],
        out_specs=[pl.BlockSpec((1, tt, D_A), lambda b, j: (b, j, 0)), state],
        out_shape=[jax.ShapeDtypeStruct((bsz, t, D_A), BF16),
                   jax.ShapeDtypeStruct((bsz, HGRN_HEADS, HGRN_DK, HGRN_DV), F32)],
        scratch_shapes=[pltpu.VMEM((HGRN_HEADS, HGRN_DV, HGRN_DK), F32)],
        compiler_params=_cparams(("arbitrary", "arbitrary")),
        name="hgrn2",
    )(pb, pf, pb, pb, pb, lb, ng, s0, mz)


def _rope(x, cos, sin_lo, sin_hi):
    half = ROT_DIM // 2
    return x * cos + pltpu.roll(x, LANES - half, 1) * sin_lo + pltpu.roll(x, half, 1) * sin_hi


def _attn_kernel(sink_ref, q_ref, kv_ref, z_ref, cos_ref, slo_ref, shi_ref, ck_ref, cv_ref,
                 u_ref, ko_ref, vo_ref, kbuf, vbuf, *, tt, masked, outw):
    t = pl.program_id(1)
    last = t == pl.num_programs(1) - 1
    n_chunks = tt // CHUNK
    n_slabs = KV_W // LANES

    def store_band(buf, r0, x, lanes):
        buf[r0:r0 + x.shape[0], lanes] = x.astype(BF16)
        buf[r0:r0 + x.shape[0], slice(KV_W + lanes.start, KV_W + lanes.stop)] = (
            pltpu.roll(x, HEAD_DIM, 1).astype(BF16))

    @pl.when(t == 0)
    def _():
        for j in range(n_slabs):
            lanes = slice(j * LANES, (j + 1) * LANES)
            store_band(kbuf, 0, ck_ref[0, :, lanes], lanes)
            store_band(vbuf, 0, cv_ref[0, :, lanes], lanes)

    if tt >= WINDOW:
        @pl.when(t > 0)
        def _():
            kbuf[0:WINDOW, :] = kbuf[tt:tt + WINDOW, :]
            vbuf[0:WINDOW, :] = vbuf[tt:tt + WINDOW, :]

    for j in range(n_slabs):
        lanes = slice(j * LANES, (j + 1) * LANES)
        kr = _rope(kv_ref[0, :, lanes].astype(F32), cos_ref[...], slo_ref[...], shi_ref[...])
        store_band(kbuf, WINDOW, kr, lanes)
        store_band(vbuf, WINDOW, kv_ref[0, :, slice(KV_W + lanes.start, KV_W + lanes.stop)].astype(F32), lanes)

        @pl.when(last)
        def _(kr=kr, lanes=lanes):
            ko_ref[0, :, lanes] = kr[tt - outw:, :]

    @pl.when(last)
    def _():
        vo_ref[0] = kv_ref[0, tt - outw:tt, KV_W:2 * KV_W].astype(F32)

    lane = lax.broadcasted_iota(jnp.int32, (CHUNK, LANES), 1)
    low = lane < HEAD_DIM
    heads_per_slab = LANES // HEAD_DIM * GROUP

    def kv_in_low_half(p):
        return (2 * p) // GROUP == 0

    def matched_head(p):
        return 0 if kv_in_low_half(p) else 1

    sink_rows = []
    for j in range(n_slabs):
        order = ([heads_per_slab * j + 2 * p + matched_head(p) for p in range(GROUP)]
                 + [heads_per_slab * j + 2 * p + 1 - matched_head(p) for p in range(GROUP)])
        sink_rows.append(jnp.concatenate([jnp.full((1, CHUNK), sink_ref[h] * LOG2E, F32) for h in order],
                                         axis=1))
    nt_dims = (((1,), (1,)), ((), ()))
    tn_dims = (((0,), (0,)), ((), ()))
    half_cols = GROUP * CHUNK

    def chunk(c, carry):
        r0 = pl.multiple_of(c * CHUNK, CHUNK)
        rows = pl.ds(r0, CHUNK)
        band = pl.ds(r0, BAND)
        cq, sl, sh = cos_ref[rows, :], slo_ref[rows, :], shi_ref[rows, :]
        seq_chunk = t * n_chunks + c

        def slab_chunk(j):
            plain = slice(j * LANES, (j + 1) * LANES)
            swapped = slice(KV_W + j * LANES, KV_W + (j + 1) * LANES)
            k_plain, k_swap = kbuf[band, plain], kbuf[band, swapped]
            v_plain, v_swap = vbuf[band, plain], vbuf[band, swapped]
            q_plain, q_swap = [], []
            for p in range(GROUP):
                slab = GROUP * j + p
                x = _rope(q_ref[0, rows, slab * LANES:(slab + 1) * LANES].astype(F32), cq, sl, sh) * (
                    ATTN_SCALE * LOG2E)
                halves = [jnp.where(low, x, 0.0).astype(BF16), jnp.where(low, 0.0, x).astype(BF16)]
                q_plain.append(halves[matched_head(p)])
                q_swap.append(halves[1 - matched_head(p)])
            q_plain = jnp.concatenate(q_plain, axis=0)
            q_swap = jnp.concatenate(q_swap, axis=0)
            yield
            s = jnp.concatenate([lax.dot_general(k_plain, q_plain, nt_dims, preferred_element_type=F32),
                                 lax.dot_general(k_swap, q_swap, nt_dims, preferred_element_type=F32)],
                                axis=1)
            yield
            if masked:
                blocks = [jnp.where(seq_chunk >= N_LOOKBACK_CHUNKS - i, s[i * CHUNK:(i + 1) * CHUNK], NEG)
                          for i in range(N_LOOKBACK_CHUNKS)]
                s = jnp.concatenate(blocks + [s[WINDOW:]], axis=0)
            sink = sink_rows[j]
            m = jnp.maximum(jnp.max(s, axis=0, keepdims=True), sink)
            pr = jnp.exp2(s - m)
            inv = 1.0 / (jnp.sum(pr, axis=0, keepdims=True) + jnp.exp2(sink - m))
            pr16 = pr.astype(BF16)
            yield
            o_plain = lax.dot_general(v_plain, pr16[:, :half_cols], tn_dims, preferred_element_type=F32)
            o_swap = lax.dot_general(v_swap, pr16[:, half_cols:], tn_dims, preferred_element_type=F32)
            yield
            o_plain = o_plain * inv[:, :half_cols]
            o_swap = o_swap * inv[:, half_cols:]
            for pair in range(GROUP // 2):
                t_plain = o_plain[:, pair * LANES:(pair + 1) * LANES].T
                t_swap = o_swap[:, pair * LANES:(pair + 1) * LANES].T
                for i in range(2):
                    p = 2 * pair + i
                    slab = GROUP * j + p
                    qrows = slice(i * CHUNK, (i + 1) * CHUNK)
                    if matched_head(p) == 0:
                        ob = jnp.where(low, t_plain[qrows], t_swap[qrows])
                    else:
                        ob = jnp.where(low, t_swap[qrows], t_plain[qrows])
                    zb = z_ref[0, rows, slab * LANES:(slab + 1) * LANES].astype(F32)
                    u_ref[0, rows, slab * LANES:(slab + 1) * LANES] = (
                        ob * (zb * _sigmoid(zb))).astype(BF16)

        stages = [slab_chunk(j) for j in range(n_slabs)]
        while stages:
            stages = [s for s in stages if next(s, "done") != "done"]
        return carry

    lax.fori_loop(0, n_chunks, chunk, 0)


def _rope_tables(pos):
    half = ROT_DIM // 2
    inv = (ROPE_THETA ** (-np.arange(0, ROT_DIM, 2) / ROT_DIM)).astype(np.float32)
    ang = pos.astype(F32)[:, None] * inv[None, :]
    cos, sin = jnp.cos(ang), jnp.sin(ang)
    n = pos.shape[0]
    rest = HEAD_DIM - ROT_DIM
    cos_h = jnp.concatenate([cos, cos, jnp.ones((n, rest), F32)], axis=1)
    slo_h = jnp.concatenate([-sin, jnp.zeros((n, half + rest), F32)], axis=1)
    shi_h = jnp.concatenate([jnp.zeros((n, half), F32), sin, jnp.zeros((n, rest), F32)], axis=1)
    reps = LANES // HEAD_DIM
    return tuple(jnp.tile(a, (1, reps)) for a in (cos_h, slo_h, shi_h))


def _attention(pb, sinks, tables, cache_k, cache_v, *, tt, masked, outw):
    bsz, t, _ = pb.shape
    tab = pl.BlockSpec((tt, LANES), lambda b, j, s: (j, 0))
    cache = pl.BlockSpec((1, WINDOW, KV_W), lambda b, j, s: (b, 0, 0))
    win = pl.BlockSpec((1, outw, KV_W), lambda b, j, s: (b, 0, 0))
    grid_spec = pltpu.PrefetchScalarGridSpec(
        num_scalar_prefetch=1,
        grid=(bsz, t // tt),
        in_specs=[pl.BlockSpec((1, tt, D_B), lambda b, j, s: (b, j, COL_QB // D_B)),
                  pl.BlockSpec((1, tt, 2 * KV_W), lambda b, j, s: (b, j, COL_KV // (2 * KV_W))),
                  pl.BlockSpec((1, tt, D_B), lambda b, j, s: (b, j, COL_ZB // D_B)),
                  tab, tab, tab, cache, cache],
        out_specs=[pl.BlockSpec((1, tt, D_B), lambda b, j, s: (b, j, 0)), win, win],
        scratch_shapes=[pltpu.VMEM((WINDOW + tt, 2 * KV_W), BF16), pltpu.VMEM((WINDOW + tt, 2 * KV_W), BF16)],
    )
    return pl.pallas_call(
        functools.partial(_attn_kernel, tt=tt, masked=masked, outw=outw),
        grid_spec=grid_spec,
        out_shape=[jax.ShapeDtypeStruct((bsz, t, D_B), BF16),
                   jax.ShapeDtypeStruct((bsz, outw, KV_W), F32),
                   jax.ShapeDtypeStruct((bsz, outw, KV_W), F32)],
        compiler_params=_cparams(("arbitrary", "arbitrary")),
        name="swa_attention",
    )(sinks, pb, pb, pb, *tables, cache_k, cache_v)


def _out_kernel(ua_ref, ub_ref, ma_ref, mb_ref, x_ref, gate_ref, wpa_ref, wpb_ref, wout_ref, fg_ref,
                y_ref, *, nb, tt, final):
    m = nb * tt
    pa = jnp.dot(ua_ref[...].reshape(m, D_A), wpa_ref[...], preferred_element_type=F32)
    pb = jnp.dot(ub_ref[...].reshape(m, D_B), wpb_ref[...], preferred_element_type=F32)
    ma = ma_ref[...].reshape(m, D_MODEL).astype(F32)
    mb = mb_ref[...].reshape(m, D_MODEL).astype(F32)
    merged = _sigmoid(ma) * pa + _sigmoid(mb) * pb
    o = jnp.dot(merged.astype(BF16), wout_ref[...], preferred_element_type=F32)
    y = x_ref[...] + gate_ref[...] * o.reshape(nb, tt, D_MODEL)
    if final:
        y = (y * lax.rsqrt(jnp.mean(y * y, axis=-1, keepdims=True) + EPS)) * fg_ref[...]
    y_ref[...] = y


def _output(ua, ub, pb, x, gate, wpa, wpb, wout, final_g, *, layer, nb, tt, final):
    bsz, t, _ = x.shape
    tok = lambda i, j: (i, j, 0)
    const = lambda i, j: (0, 0)
    of_layer = lambda i, j: (layer, 0, 0)
    single = pl.Buffered(1)
    return pl.pallas_call(
        functools.partial(_out_kernel, nb=nb, tt=tt, final=final),
        grid=(bsz // nb, t // tt),
        in_specs=[pl.BlockSpec((nb, tt, D_A), tok),
                  pl.BlockSpec((nb, tt, D_B), tok),
                  pl.BlockSpec((nb, tt, D_MODEL), lambda i, j: (i, j, COL_MA // D_MODEL)),
                  pl.BlockSpec((nb, tt, D_MODEL), lambda i, j: (i, j, COL_MB // D_MODEL)),
                  pl.BlockSpec((nb, tt, D_MODEL), tok),
                  pl.BlockSpec((nb, 1, D_MODEL), lambda i, j: (i, 0, 0)),
                  pl.BlockSpec((None, D_A, D_MODEL), of_layer, pipeline_mode=single),
                  pl.BlockSpec((None, D_B, D_MODEL), of_layer, pipeline_mode=single),
                  pl.BlockSpec((None, D_MODEL, D_MODEL), of_layer, pipeline_mode=single),
                  pl.BlockSpec((1, D_MODEL), const)],
        out_specs=pl.BlockSpec((nb, tt, D_MODEL), tok),
        out_shape=jax.ShapeDtypeStruct((bsz, t, D_MODEL), F32),
        compiler_params=_cparams(("arbitrary", "arbitrary")),
        name="merge_output",
    )(ua, ub, pb, pb, x, gate, wpa, wpb, wout, final_g)


def _tiles(t):
    if t >= 1024:
        return dict(proj=(1, min(t, 2048)), out=(1, 512), hgrn_tt=min(t, 1024), attn_tt=min(t, 1024))
    return dict(proj=(16, t), out=(4, t), hgrn_tt=t, attn_tt=t)


def _layer(x, mod, w, lb, s0, cache_k, cache_v, tables, *, final, final_g):
    bsz, t, _ = x.shape
    tl = _tiles(t)
    shift, scale, gate = (mod[:, None, i * D_MODEL:(i + 1) * D_MODEL] for i in range(3))
    nb, tt = tl["proj"]
    pf, pb = _in_projection(x, shift, scale, w["norm_g"], w["w_in"], layer=w["layer"], nb=min(nb, bsz), tt=tt)
    ua, s_new = _hgrn(pf, pb, lb, w["hgrn_g"], s0, tt=tl["hgrn_tt"])
    masked = cache_k is None
    if masked:
        cache_k = jnp.zeros((bsz, WINDOW, KV_W), F32)
        cache_v = cache_k
    outw = min(t, WINDOW)
    ub, k_new, v_new = _attention(pb, w["sinks"], tables, cache_k, cache_v,
                                  tt=tl["attn_tt"], masked=masked, outw=outw)
    nb, tt = tl["out"]
    y = _output(ua, ub, pb, x, gate, w["w_pa"], w["w_pb"], w["w_out"], final_g,
                layer=w["layer"], nb=min(nb, bsz), tt=tt, final=final)
    shape = (bsz, outw, N_KV, HEAD_DIM)
    return y, s_new, k_new.reshape(shape), v_new.reshape(shape)


def kernel(x_prompt, x_sample, c_prompt, c_sample, state_hgrn, cache_win_k, cache_win_v, ada_w, ada_b,
           norm_g, w_in, lb_logits, hgrn_norm_g, sinks, w_branch_a, w_branch_b, w_out, final_norm_g):
    bp, tp = x_prompt.shape[0], x_prompt.shape[1]
    bs, ts = x_sample.shape[0], x_sample.shape[1]

    w_in_p = w_in.astype(BF16)
    w_pa = w_branch_a.astype(BF16)
    w_pb = w_branch_b.astype(BF16)
    w_o = w_out.astype(BF16)
    prob = jax.nn.softmax(lb_logits.astype(F32), axis=0)
    lb = jnp.cumsum(prob, axis=0) - prob[:1]
    final_g = final_norm_g.reshape(1, D_MODEL)

    mod = _modulation(jnp.concatenate([c_prompt, c_sample], axis=0), ada_w, ada_b)
    tab_p = _rope_tables(jnp.arange(tp))
    tab_s = _rope_tables(PAST_LEN + jnp.arange(ts))
    zero_state = jnp.zeros((bp, HGRN_HEADS, HGRN_DK, HGRN_DV), F32)

    hp, hs = x_prompt, x_sample
    outs = [[] for _ in range(6)]
    for l in range(DEPTH):
        w = dict(norm_g=norm_g[l].reshape(1, D_MODEL), w_in=w_in_p, hgrn_g=hgrn_norm_g[l].reshape(1, D_A),
                 sinks=sinks[l], w_pa=w_pa, w_pb=w_pb, w_out=w_o, layer=l)
        lbl = lb[l].reshape(1, D_A)
        final = l == DEPTH - 1
        hp, s_p, k_p, v_p = _layer(hp, mod[l, :bp], w, lbl, zero_state, None, None, tab_p,
                                   final=final, final_g=final_g)
        hs, s_s, k_s, v_s = _layer(hs, mod[l, bp:], w, lbl, state_hgrn[l],
                                   cache_win_k[l].reshape(bs, WINDOW, KV_W),
                                   cache_win_v[l].reshape(bs, WINDOW, KV_W), tab_s,
                                   final=final, final_g=final_g)
        for acc, val in zip(outs, (s_p, k_p, v_p, s_s, k_s, v_s)):
            acc.append(val)
    return (hp, hs) + tuple(jnp.stack(o) for o in outs)
```
